```python
import math
import jax
import jax.numpy as jnp
from jax import lax
import numpy as np

D_MODEL = 2048
BATCH = 1
SEQ = 8192
DEPTH = 1
DEC_BATCH = 32
DEC_SEQ = 1
PAST_LEN = 8192
PAGE_SIZE = 128

ATT_HEADS = 8
ATT_HEAD_DIM = 64
ATT_WIDTH = ATT_HEADS * 2 * ATT_HEAD_DIM
ROPE_THETA = 10000.0
Q_BLOCK = 128
HG_HEADS = 8
HG_KEY = 128
HG_VAL = 128
HG_WIDTH = HG_HEADS * HG_VAL
HG_CHUNK = 64
MEM_LEN = 256
X_HEADS = 4
X_HEAD_DIM = 128
X_WIDTH = X_HEADS * X_HEAD_DIM
N_GROUPS = 4
EXPERTS_PER_GROUP = 8
N_EXPERTS = N_GROUPS * EXPERTS_PER_GROUP
TOP_K = 2
EXPERT_FF = 512
MOE_BLOCK = 128

RMS_EPS = 1e-6
NEG_INF = -1e30
IN_SPLITS = (ATT_WIDTH, ATT_WIDTH, ATT_WIDTH, HG_HEADS * HG_KEY, HG_HEADS * HG_KEY,
             HG_WIDTH, HG_WIDTH, D_MODEL, D_MODEL)
N_IN = sum(IN_SPLITS)

kernel_name = 'hybrid_diffattn_hgrn2_hmoe_step'


def rmsnorm(x, g):
    xf = x.astype(jnp.float32)
    y = xf * lax.rsqrt(jnp.mean(xf * xf, axis=-1, keepdims=True) + RMS_EPS)
    return (y * g.astype(jnp.float32)).astype(x.dtype)


def rope(x, pos):
    half = x.shape[-1] // 2
    freqs = ROPE_THETA ** (-jnp.arange(half, dtype=jnp.float32) / half)
    ang = pos.astype(jnp.float32)[:, None] * freqs[None, :]
    cos = jnp.cos(ang)[None, :, None, None, :]
    sin = jnp.sin(ang)[None, :, None, None, :]
    xf = x.astype(jnp.float32)
    x1, x2 = xf[..., :half], xf[..., half:]
    return jnp.concatenate([x1 * cos - x2 * sin, x2 * cos + x1 * sin], axis=-1).astype(x.dtype)


def diff_attention(q, k, v, q_pos, k_pos, lam):
    B, Lq, H, M, dh = q.shape
    qb = math.gcd(Lq, Q_BLOCK)
    nb = Lq // qb
    qs = q.reshape(B, nb, qb, H, M, dh).swapaxes(0, 1)
    ps = q_pos.reshape(nb, qb)
    scale = dh ** -0.5

    def block(args):
        qi, pi = args
        s = jnp.einsum('bqhmd,bkhmd->bhmqk', qi, k).astype(jnp.float32) * scale
        s = jnp.where(k_pos[None, :] <= pi[:, None], s, NEG_INF)
        p = jax.nn.softmax(s, axis=-1)
        w = p[:, :, 0] - lam * p[:, :, 1]
        return jnp.einsum('bhqk,bkhe->bqhe', w.astype(v.dtype), v)

    o = lax.map(block, (qs, ps))
    return o.swapaxes(0, 1).reshape(B, Lq, H, v.shape[-1])


def hgrn2_recurrence(q, log_f, k, v, s0):
    B, L, H, DK = q.shape
    DV = v.shape[-1]
    c = math.gcd(L, HG_CHUNK)
    n = L // c

    def chunks(t):
        return t.reshape(B, n, c, H, t.shape[-1]).swapaxes(0, 1)

    causal = jnp.tril(jnp.ones((c, c), dtype=bool))

    def step(S, inp):
        qc, lfc, kc, vc = inp
        b = jnp.cumsum(lfc, axis=1)
        o_inter = jnp.einsum('bthk,bhkv->bthv', qc * jnp.exp(b), S)
        diff = b[:, :, None] - b[:, None, :]
        decay = jnp.where(causal[None, :, :, None, None], jnp.exp(jnp.minimum(diff, 0.0)), 0.0)
        a = jnp.einsum('bthk,btshk,bshk->bths', qc, decay, kc)
        o = o_inter + jnp.einsum('bths,bshv->bthv', a, vc)
        b_last = b[:, -1]
        S = jnp.exp(b_last)[..., None] * S + jnp.einsum(
            'bshk,bshv->bhkv', kc * jnp.exp(b_last[:, None] - b), vc)
        return S, o

    S, o = lax.scan(step, s0, (chunks(q), chunks(log_f), chunks(k), chunks(v)))
    return o.swapaxes(0, 1).reshape(B, L, H, DV), S


def parallel_mixer(u, pos, k_pos, k_past, v_past, s0, lambda_init, w_in, lq1, lk1, lq2, lk2,
                   subln, lb, hg_norm, w_pa, w_pb, w_out):
    B, L, _ = u.shape
    offs = np.cumsum(IN_SPLITS)[:-1].tolist()
    q_a, k_a, v_a, q_h, f_h, i_h, g_h, gate_a, gate_b = jnp.split(u @ w_in, offs, axis=-1)

    q_a = rope(q_a.reshape(B, L, ATT_HEADS, 2, ATT_HEAD_DIM), pos)
    k_a = rope(k_a.reshape(B, L, ATT_HEADS, 2, ATT_HEAD_DIM), pos)
    v_a = v_a.reshape(B, L, ATT_HEADS, 2 * ATT_HEAD_DIM)
    if k_past is None:
        k_all, v_all = k_a, v_a
    else:
        k_all = jnp.concatenate([k_past.astype(k_a.dtype), k_a], axis=1)
        v_all = jnp.concatenate([v_past.astype(v_a.dtype), v_a], axis=1)
    f32 = jnp.float32
    lam = (jnp.exp(jnp.sum(lq1.astype(f32) * lk1.astype(f32)))
           - jnp.exp(jnp.sum(lq2.astype(f32) * lk2.astype(f32))) + lambda_init)
    o_a = diff_attention(q_a, k_all, v_all, pos, k_pos, lam)
    o_a = (rmsnorm(o_a, subln) * (1.0 - lambda_init)).reshape(B, L, ATT_WIDTH)

    lbr = lb.reshape(HG_HEADS, HG_KEY)
    zf = f_h.astype(f32).reshape(B, L, HG_HEADS, HG_KEY)
    log_f = jnp.log(lbr + (1.0 - lbr) * jax.nn.sigmoid(zf))
    k_h = (1.0 - lbr) * jax.nn.sigmoid(-zf)
    q_hf = q_h.astype(f32).reshape(B, L, HG_HEADS, HG_KEY)
    v_hf = i_h.astype(f32).reshape(B, L, HG_HEADS, HG_VAL)
    o_h, s_new = hgrn2_recurrence(q_hf, log_f, k_h, v_hf, s0.astype(f32))
    o_h = rmsnorm(o_h, hg_norm).astype(u.dtype) * jax.nn.silu(g_h.reshape(B, L, HG_HEADS, HG_VAL))
    o_h = o_h.reshape(B, L, HG_WIDTH)

    merged = jax.nn.sigmoid(gate_a) * (o_a @ w_pa) + jax.nn.sigmoid(gate_b) * (o_h @ w_pb)
    return merged @ w_out, k_a, v_a, s_new


def cross_attention(u, mem_k, mem_v, w_cq, w_co):
    B, L, _ = u.shape
    q = (u @ w_cq).reshape(B, L, X_HEADS, X_HEAD_DIM)
    s = jnp.einsum('bqhd,bkhd->bhqk', q, mem_k.astype(q.dtype)).astype(jnp.float32) * (X_HEAD_DIM ** -0.5)
    p = jax.nn.softmax(s, axis=-1)
    o = jnp.einsum('bhqk,bkhd->bqhd', p.astype(u.dtype), mem_v.astype(u.dtype)).reshape(B, L, X_WIDTH)
    return o @ w_co


def hier_moe(u2d, wr_g, br_g, wr_e, br_e, we_g, we_u, we_d):
    N, D = u2d.shape
    gl = (u2d @ wr_g).astype(jnp.float32) + br_g.astype(jnp.float32)
    _, gi = lax.top_k(gl, 1)
    g_sel = gi[:, 0]
    rows = jnp.arange(N)
    p_group = jax.nn.softmax(gl, axis=-1)[rows, g_sel]
    el = ((u2d @ wr_e).astype(jnp.float32) + br_e.astype(jnp.float32)).reshape(N, N_GROUPS, EXPERTS_PER_GROUP)
    el_sel = el[rows, g_sel]
    tv, ti = lax.top_k(el_sel, TOP_K)
    w = jax.nn.softmax(tv, axis=-1) * p_group[:, None]
    eid = g_sel[:, None] * EXPERTS_PER_GROUP + ti

    A = N * TOP_K
    e = eid.reshape(A)
    tok = jnp.repeat(rows, TOP_K)
    wt = w.reshape(A)
    order = jnp.argsort(e)
    e_s = e[order]
    counts = jnp.bincount(e, length=N_EXPERTS)
    starts = jnp.cumsum(counts) - counts
    pcounts = ((counts + MOE_BLOCK - 1) // MOE_BLOCK) * MOE_BLOCK
    pends = jnp.cumsum(pcounts)
    pstarts = pends - pcounts
    dest = pstarts[e_s] + jnp.arange(A) - starts[e_s]
    n_blocks = (A + N_EXPERTS * (MOE_BLOCK - 1) + MOE_BLOCK - 1) // MOE_BLOCK
    R = n_blocks * MOE_BLOCK
    row_tok = jnp.zeros((R,), jnp.int32).at[dest].set(tok[order].astype(jnp.int32))
    row_w = jnp.zeros((R,), jnp.float32).at[dest].set(wt[order])
    blk_exp = jnp.minimum(jnp.searchsorted(pends, jnp.arange(n_blocks) * MOE_BLOCK, side='right'),
                          N_EXPERTS - 1)

    def run(args):
        toks, ex = args
        xb = u2d[toks]
        h = jax.nn.silu(xb @ we_g[ex]) * (xb @ we_u[ex])
        return h @ we_d[ex]

    yb = lax.map(run, (row_tok.reshape(n_blocks, MOE_BLOCK), blk_exp))
    y = jnp.zeros((N, D), jnp.float32).at[row_tok].add(yb.reshape(R, D).astype(jnp.float32) * row_w[:, None])
    return y.astype(u2d.dtype)


def trunk_layer(x, pos, k_pos, k_past, v_past, s0, mem_k, mem_v, lambda_init, g_mix, w_in,
                lq1, lk1, lq2, lk2, subln, lb, hg_norm, w_pa, w_pb, w_out, g_cross, w_cq, w_co,
                g_ffn, wr_g, br_g, wr_e, br_e, we_g, we_u, we_d):
    mix, k_new, v_new, s_new = parallel_mixer(rmsnorm(x, g_mix), pos, k_pos, k_past, v_past, s0,
                                              lambda_init, w_in, lq1, lk1, lq2, lk2, subln, lb,
                                              hg_norm, w_pa, w_pb, w_out)
    h = x + mix
    h = h + cross_attention(rmsnorm(h, g_cross), mem_k, mem_v, w_cq, w_co)
    B, L, D = h.shape
    h = h + hier_moe(rmsnorm(h, g_ffn).reshape(B * L, D), wr_g, br_g, wr_e, br_e,
                     we_g, we_u, we_d).reshape(B, L, D)
    return h, k_new, v_new, s_new


def setup_inputs(seed: int = 0) -> dict:
    key = jax.random.key(seed)
    ks = iter(jax.random.split(key, 48))

    def nrm(shape, scale):
        return jax.random.normal(next(ks), shape, jnp.float32) * scale

    def gain(shape):
        return 1.0 + nrm(shape, 0.02)

    n_pages = PAST_LEN // PAGE_SIZE
    n_pool = (5 * DEC_BATCH * n_pages + 3) // 4
    page_table = jax.random.permutation(next(ks), n_pool)[:DEC_BATCH * n_pages]
    page_table = page_table.reshape(DEC_BATCH, n_pages).astype(jnp.int32)
    D = D_MODEL
    return {
        'x_prompt': nrm((BATCH, SEQ, D), 1.0),
        'x_sample': nrm((DEC_BATCH, DEC_SEQ, D), 1.0),
        'mem_prompt': nrm((BATCH, MEM_LEN, D), 1.0),
        'cache_k': nrm((DEPTH, n_pool, PAGE_SIZE, ATT_HEADS, 2, ATT_HEAD_DIM), 1.0),
        'cache_v': nrm((DEPTH, n_pool, PAGE_SIZE, ATT_HEADS, 2 * ATT_HEAD_DIM), 1.0),
        'cache_mem_k': nrm((DEPTH, DEC_BATCH, MEM_LEN, X_HEADS, X_HEAD_DIM), 1.0),
        'cache_mem_v': nrm((DEPTH, DEC_BATCH, MEM_LEN, X_HEADS, X_HEAD_DIM), 1.0),
        'state_hgrn': nrm((DEPTH, DEC_BATCH, HG_HEADS, HG_KEY, HG_VAL), 0.3),
        'page_table': page_table,
        'norm_mix': gain((DEPTH, D)),
        'w_in': nrm((DEPTH, D, N_IN), D ** -0.5),
        'lambda_q1': nrm((DEPTH, ATT_HEAD_DIM), 0.1),
        'lambda_k1': nrm((DEPTH, ATT_HEAD_DIM), 0.1),
        'lambda_q2': nrm((DEPTH, ATT_HEAD_DIM), 0.1),
        'lambda_k2': nrm((DEPTH, ATT_HEAD_DIM), 0.1),
        'subln': gain((DEPTH, 2 * ATT_HEAD_DIM)),
        'hgrn_lb': nrm((DEPTH + 1, HG_HEADS * HG_KEY), 0.1),
        'hgrn_norm': gain((DEPTH, HG_VAL)),
        'w_pa': nrm((DEPTH, ATT_WIDTH, D), ATT_WIDTH ** -0.5),
        'w_pb': nrm((DEPTH, HG_WIDTH, D), HG_WIDTH ** -0.5),
        'w_out': nrm((DEPTH, D, D), D ** -0.5),
        'norm_cross': gain((DEPTH, D)),
        'w_cq': nrm((DEPTH, D, X_WIDTH), D ** -0.5),
        'w_ck': nrm((DEPTH, D, X_WIDTH), D ** -0.5),
        'w_cv': nrm((DEPTH, D, X_WIDTH), D ** -0.5),
        'w_co': nrm((DEPTH, X_WIDTH, D), X_WIDTH ** -0.5),
        'norm_ffn': gain((DEPTH, D)),
        'w_router_group': nrm((DEPTH, D, N_GROUPS), D ** -0.5),
        'b_router_group': nrm((DEPTH, N_GROUPS), 0.01),
        'w_router_expert': nrm((DEPTH, D, N_EXPERTS), D ** -0.5),
        'b_router_expert': nrm((DEPTH, N_EXPERTS), 0.01),
        'w_e_gate': nrm((DEPTH, N_EXPERTS, D, EXPERT_FF), D ** -0.5),
        'w_e_up': nrm((DEPTH, N_EXPERTS, D, EXPERT_FF), D ** -0.5),
        'w_e_down': nrm((DEPTH, N_EXPERTS, EXPERT_FF, D), EXPERT_FF ** -0.5),
        'norm_final': gain((D,)),
    }


def reference(x_prompt, x_sample, mem_prompt, cache_k, cache_v, cache_mem_k, cache_mem_v,
              state_hgrn, page_table, norm_mix, w_in, lambda_q1, lambda_k1, lambda_q2, lambda_k2,
              subln, hgrn_lb, hgrn_norm, w_pa, w_pb, w_out, norm_cross, w_cq, w_ck, w_cv, w_co,
              norm_ffn, w_router_group, b_router_group, w_router_expert, b_router_expert,
              w_e_gate, w_e_up, w_e_down, norm_final):
    Bp, S, _ = x_prompt.shape
    Bd, Ld, _ = x_sample.shape
    n_pages = page_table.shape[1]
    past_len = n_pages * cache_k.shape[2]
    pos_p = jnp.arange(S, dtype=jnp.int32)
    pos_d = past_len + jnp.arange(Ld, dtype=jnp.int32)
    kpos_d = jnp.concatenate([jnp.arange(past_len, dtype=jnp.int32), pos_d])
    lbs = jnp.cumsum(jax.nn.softmax(hgrn_lb.astype(jnp.float32), axis=0), axis=0)

    hp, hd = x_prompt, x_sample
    kp_l, vp_l, kd_l, vd_l, sp_l, sd_l, mkp_l, mvp_l = [], [], [], [], [], [], [], []
    for l in range(DEPTH):
        lambda_init = 0.8 - 0.6 * math.exp(-0.3 * l)
        shared = (lambda_init, norm_mix[l], w_in[l], lambda_q1[l], lambda_k1[l], lambda_q2[l],
                  lambda_k2[l], subln[l], lbs[l], hgrn_norm[l], w_pa[l], w_pb[l], w_out[l],
                  norm_cross[l], w_cq[l], w_co[l], norm_ffn[l], w_router_group[l],
                  b_router_group[l], w_router_expert[l], b_router_expert[l],
                  w_e_gate[l], w_e_up[l], w_e_down[l])
        mk_p = (mem_prompt @ w_ck[l]).reshape(Bp, -1, X_HEADS, X_HEAD_DIM)
        mv_p = (mem_prompt @ w_cv[l]).reshape(Bp, -1, X_HEADS, X_HEAD_DIM)
        s0_p = jnp.zeros((Bp, HG_HEADS, HG_KEY, HG_VAL), jnp.float32)
        hp, kpn, vpn, spn = trunk_layer(hp, pos_p, pos_p, None, None, s0_p, mk_p, mv_p, *shared)
        k_past = cache_k[l, page_table].reshape(Bd, past_len, ATT_HEADS, 2, ATT_HEAD_DIM)
        v_past = cache_v[l, page_table].reshape(Bd, past_len, ATT_HEADS, 2 * ATT_HEAD_DIM)
        hd, kdn, vdn, sdn = trunk_layer(hd, pos_d, kpos_d, k_past, v_past, state_hgrn[l],
                                        cache_mem_k[l], cache_mem_v[l], *shared)
        kp_l.append(kpn)
        vp_l.append(vpn)
        kd_l.append(kdn)
        vd_l.append(vdn)
        sp_l.append(spn.astype(state_hgrn.dtype))
        sd_l.append(sdn.astype(state_hgrn.dtype))
        mkp_l.append(mk_p)
        mvp_l.append(mv_p)

    y_prompt = rmsnorm(hp, norm_final)
    y_sample = rmsnorm(hd, norm_final)
    return (y_prompt, y_sample, jnp.stack(kp_l), jnp.stack(vp_l), jnp.stack(kd_l), jnp.stack(vd_l),
            jnp.stack(sp_l), jnp.stack(sd_l), jnp.stack(mkp_l), jnp.stack(mvp_l))
```

```python
import functools
import math

import numpy as np
import jax
import jax.numpy as jnp
from jax import lax
from jax.experimental import pallas as pl
from jax.experimental.pallas import tpu as pltpu

F32 = jnp.float32
BF16 = jnp.bfloat16
I32 = jnp.int32

D_MODEL = 2048
ATT_HEADS = 8
ATT_HEAD_DIM = 64
ATT_WIDTH = ATT_HEADS * 2 * ATT_HEAD_DIM
ROPE_THETA = 10000.0
HG_HEADS = 8
HG_KEY = 128
HG_VAL = 128
HG_WIDTH = HG_HEADS * HG_VAL
X_HEADS = 4
X_HEAD_DIM = 128
X_WIDTH = X_HEADS * X_HEAD_DIM
N_GROUPS = 4
EXPERTS_PER_GROUP = 8
N_EXPERTS = N_GROUPS * EXPERTS_PER_GROUP
EXPERT_FF = 512
RMS_EPS = 1e-6
NEG_INF = -1e30
LAMBDA_INIT = 0.8 - 0.6 * math.exp(-0.3 * 0)

LANES = 128
SUBLANES = 8
VMEM_LIMIT = 52 * 1024 * 1024

HG_CHUNK = 64
HG_LEVELS = (16, 32, 64)
MOE_ROWS = 128
PAGES_PER_STEP = 8

_NT = (((1,), (1,)), ((), ()))
_TN = (((0,), (0,)), ((), ()))


def _params(sem):
    return pltpu.CompilerParams(dimension_semantics=sem, vmem_limit_bytes=VMEM_LIMIT)


def _rms(x, g):
    return x * lax.rsqrt(jnp.mean(x * x, axis=-1, keepdims=True) + RMS_EPS) * g


def _mm_kernel(*refs, n_extra, n_out, norm, epilogue):
    x_ref = refs[0]
    pos = 1
    if norm:
        g_ref = refs[1]
        pos = 2
    w_ref = refs[pos]
    extras = refs[pos + 1:pos + 1 + n_extra]
    outs = refs[pos + 1 + n_extra:pos + 1 + n_extra + n_out]
    if norm:
        u_ref = refs[-1]

        @pl.when(pl.program_id(1) == 0)
        def _():
            u_ref[...] = _rms(x_ref[...].astype(F32), g_ref[...]).astype(BF16)

        u = u_ref[...]
    else:
        u = x_ref[...].astype(BF16)
    acc = jnp.dot(u, w_ref[...].astype(BF16), preferred_element_type=F32)
    epilogue(acc, extras, outs)


def _mm(x, w, *, col_off, n_cols, tm, tn, epilogue, out_dtypes, norm_g=None, extras=(), name):
    M, K = x.shape
    assert M % tm == 0 and n_cols % tn == 0 and col_off % tn == 0
    joff = col_off // tn
    if w.ndim == 3:
        w_spec = pl.BlockSpec((None, K, tn), lambda i, j: (0, 0, j + joff))
    else:
        w_spec = pl.BlockSpec((K, tn), lambda i, j: (0, j + joff))
    in_specs = [pl.BlockSpec((tm, K), lambda i, j: (i, 0))]
    args = [x]
    if norm_g is not None:
        in_specs.append(pl.BlockSpec((1, K), lambda i, j: (0, 0)))
        args.append(norm_g.reshape(1, K))
    in_specs.append(w_spec)
    args.append(w)
    for arr, spec in extras:
        in_specs.append(spec)
        args.append(arr)
    out_specs = [pl.BlockSpec((tm, tn), lambda i, j: (i, j)) for _ in out_dtypes]
    out_shape = [jax.ShapeDtypeStruct((M, n_cols), dt) for dt in out_dtypes]
    scratch = [pltpu.VMEM((tm, K), BF16)] if norm_g is not None else []
    kern = functools.partial(_mm_kernel, n_extra=len(extras), n_out=len(out_dtypes),
                             norm=norm_g is not None, epilogue=epilogue)
    res = pl.pallas_call(
        kern, grid=(M // tm, n_cols // tn), in_specs=in_specs, out_specs=out_specs,
        out_shape=out_shape, scratch_shapes=scratch,
        compiler_params=_params(("parallel", "arbitrary")), name=name)(*args)
    return res


def _rope_tile(x, cos, sin_signed):
    first = (lax.broadcasted_iota(I32, (x.shape[0], LANES), 1) % ATT_HEAD_DIM) < ATT_HEAD_DIM // 2
    outs = []
    for c in range(x.shape[1] // LANES):
        xc = x[:, c * LANES:(c + 1) * LANES]
        rot = jnp.where(first, pltpu.roll(xc, LANES - ATT_HEAD_DIM // 2, 1),
                        pltpu.roll(xc, ATT_HEAD_DIM // 2, 1))
        outs.append(xc * cos + rot * sin_signed)
    return outs[0] if len(outs) == 1 else jnp.concatenate(outs, axis=1)


def _epi_q(acc, extras, outs):
    cos_ref, sin_ref = extras
    outs[0][...] = (_rope_tile(acc, cos_ref[...], sin_ref[...]) * (ATT_HEAD_DIM ** -0.5)).astype(BF16)


def _epi_k(acc, extras, outs):
    cos_ref, sin_ref = extras
    r = _rope_tile(acc, cos_ref[...], sin_ref[...])
    outs[0][...] = r
    outs[1][...] = r.astype(BF16)


def _epi_v(acc, extras, outs):
    outs[0][...] = acc
    outs[1][...] = acc.astype(BF16)


def _epi_plain(acc, extras, outs):
    outs[0][...] = acc.astype(outs[0].dtype)


def _epi_sigmoid(acc, extras, outs):
    outs[0][...] = jax.nn.sigmoid(acc)


def _epi_residual(acc, extras, outs):
    outs[0][...] = extras[0][...] + acc


def _rope_tables(pos):
    half = ATT_HEAD_DIM // 2
    freqs = ROPE_THETA ** (-jnp.arange(half, dtype=F32) / half)
    ang = pos.astype(F32)[:, None] * freqs[None, :]
    c, s = jnp.cos(ang), jnp.sin(ang)
    return jnp.tile(c, (1, 4)), jnp.concatenate([-s, s, -s, s], axis=1)


def _in_projection(x, g, w_in, pos, tm, tn, tag):
    cos, sin = _rope_tables(pos)
    rope_extras = [(cos, pl.BlockSpec((tm, LANES), lambda i, j: (i, 0))),
                   (sin, pl.BlockSpec((tm, LANES), lambda i, j: (i, 0)))]
    common = dict(tm=tm, tn=tn, norm_g=g)
    (q,) = _mm(x, w_in, col_off=0, n_cols=ATT_WIDTH, epilogue=_epi_q, out_dtypes=[BF16],
               extras=rope_extras, name=f"inproj_q_{tag}", **common)
    k, k_bf = _mm(x, w_in, col_off=ATT_WIDTH, n_cols=ATT_WIDTH, epilogue=_epi_k, out_dtypes=[F32, BF16],
                  extras=rope_extras, name=f"inproj_k_{tag}", **common)
    v, v_bf = _mm(x, w_in, col_off=2 * ATT_WIDTH, n_cols=ATT_WIDTH, epilogue=_epi_v,
                  out_dtypes=[F32, BF16], name=f"inproj_v_{tag}", **common)
    (hg,) = _mm(x, w_in, col_off=3 * ATT_WIDTH, n_cols=4 * HG_WIDTH, epilogue=_epi_plain,
                out_dtypes=[F32], name=f"inproj_h_{tag}", **common)
    (gates,) = _mm(x, w_in, col_off=3 * ATT_WIDTH + 4 * HG_WIDTH, n_cols=2 * D_MODEL,
                   epilogue=_epi_sigmoid, out_dtypes=[F32], name=f"inproj_g_{tag}", **common)
    return q, k, k_bf, v, v_bf, hg, gates


def _lambda_value(lq1, lk1, lq2, lk2):
    return (jnp.exp(jnp.sum(lq1[...] * lk1[...], axis=-1, keepdims=True))
            - jnp.exp(jnp.sum(lq2[...] * lk2[...], axis=-1, keepdims=True)) + LAMBDA_INIT)


def _attn_kernel(q_ref, k_ref, v_ref, lq1, lk1, lq2, lk2, subln_ref, o_ref, *, tq):
    i = pl.program_id(1)
    q = q_ref[...]
    lane = lax.broadcasted_iota(I32, q.shape, 1)
    zero = jnp.zeros_like(q)
    qm = (jnp.where(lane < ATT_HEAD_DIM, q, zero), jnp.where(lane >= ATT_HEAD_DIM, q, zero))

    def update(kt, vt, carry, mask):
        new = []
        for m in range(2):
            mx, l, acc = carry[3 * m:3 * m + 3]
            s = lax.dot_general(qm[m], kt, _NT, preferred_element_type=F32)
            if mask is not None:
                s = jnp.where(mask, s, NEG_INF)
            mn = jnp.maximum(mx, jnp.max(s, axis=1, keepdims=True))
            alpha = jnp.exp(mx - mn)
            p = jnp.exp(s - mn)
            l = alpha * l + jnp.sum(p, axis=1, keepdims=True)
            acc = alpha * acc + jnp.dot(p.astype(BF16), vt, preferred_element_type=F32)
            new += [mn, l, acc]
        return tuple(new)

    def body(j, carry):
        off = pl.multiple_of(j * tq, tq)
        return update(k_ref[pl.ds(off, tq), :], v_ref[pl.ds(off, tq), :], carry, None)

    init = []
    for _ in range(2):
        init += [jnp.full((tq, 1), NEG_INF, F32), jnp.zeros((tq, 1), F32), jnp.zeros((tq, LANES), F32)]
    carry = lax.fori_loop(0, i, body, tuple(init))
    off = pl.multiple_of(i * tq, tq)
    causal = (lax.broadcasted_iota(I32, (tq, tq), 1) <= lax.broadcasted_iota(I32, (tq, tq), 0))
    _, l0, a0, _, l1, a1 = update(k_ref[pl.ds(off, tq), :], v_ref[pl.ds(off, tq), :], carry, causal)

    lam = _lambda_value(lq1, lk1, lq2, lk2)
    o = a0 / l0 - lam * (a1 / l1)
    o_ref[...] = (_rms(o, subln_ref[...]) * (1.0 - LAMBDA_INIT)).astype(BF16)


def _small_spec(shape):
    nd = len(shape)
    return pl.BlockSpec(shape, lambda *_: (0,) * nd)


def _attention_prompt(q, k_bf, v_bf, lq1, lk1, lq2, lk2, subln, tq):
    S = q.shape[0]
    lam_specs = [_small_spec((1, ATT_HEAD_DIM))] * 4
    return pl.pallas_call(
        functools.partial(_attn_kernel, tq=tq),
        grid=(ATT_HEADS, S // tq),
        in_specs=[pl.BlockSpec((tq, LANES), lambda h, i: (i, h)),
                  pl.BlockSpec((S, LANES), lambda h, i: (0, h)),
                  pl.BlockSpec((S, LANES), lambda h, i: (0, h))] + lam_specs
                 + [_small_spec((1, LANES))],
        out_specs=pl.BlockSpec((tq, LANES), lambda h, i: (i, h)),
        out_shape=jax.ShapeDtypeStruct((S, ATT_WIDTH), BF16),
        compiler_params=_params(("parallel", "arbitrary")), name="attn_prompt",
    )(q, k_bf, v_bf, lq1, lk1, lq2, lk2, subln)


def _decode_attn_kernel(pt_ref, q_ref, kn_ref, vn_ref, lq1, lk1, lq2, lk2, subln_ref, *rest, n_pg):
    k_refs = rest[:n_pg]
    v_refs = rest[n_pg:2 * n_pg]
    o_ref = rest[2 * n_pg]
    qr_ref, m_ref, l_ref, acc_ref = rest[2 * n_pg + 1:]
    j = pl.program_id(1)
    nrow = 2 * ATT_HEADS
    row = lax.broadcasted_iota(I32, (nrow, ATT_WIDTH), 0)
    lane = lax.broadcasted_iota(I32, (nrow, ATT_WIDTH), 1)
    lane_head = lane // LANES
    lane_map = (lane // ATT_HEAD_DIM) % 2

    @pl.when(j == 0)
    def _():
        sel = (lane_head == row % ATT_HEADS) & (lane_map == row // ATT_HEADS)
        qb = jnp.broadcast_to(q_ref[...].astype(F32), (nrow, ATT_WIDTH))
        qr_ref[...] = jnp.where(sel, qb, 0.0).astype(BF16)
        m_ref[...] = jnp.full(m_ref.shape, NEG_INF, F32)
        l_ref[...] = jnp.zeros(l_ref.shape, F32)
        acc_ref[...] = jnp.zeros(acc_ref.shape, F32)

    qr = qr_ref[...]
    s = jnp.concatenate(
        [lax.dot_general(qr, kr[...].astype(BF16), _NT, preferred_element_type=F32) for kr in k_refs],
        axis=1)
    mx = m_ref[...]
    mn = jnp.maximum(mx, jnp.max(s, axis=1, keepdims=True))
    alpha = jnp.exp(mx - mn)
    p = jnp.exp(s - mn)
    l_ref[...] = alpha * l_ref[...] + jnp.sum(p, axis=1, keepdims=True)
    pb = p.astype(BF16)
    page = s.shape[1] // n_pg
    pv = jnp.zeros((nrow, ATT_WIDTH), F32)
    for g, vr in enumerate(v_refs):
        pv = pv + jnp.dot(pb[:, g * page:(g + 1) * page], vr[...].astype(BF16), preferred_element_type=F32)
    acc_ref[...] = alpha * acc_ref[...] + pv
    m_ref[...] = mn

    @pl.when(j == pl.num_programs(1) - 1)
    def _():
        kn = kn_ref[...].astype(F32)
        vn = vn_ref[...].astype(F32)
        sn = jnp.sum(qr.astype(F32) * kn, axis=1, keepdims=True)
        mx2 = m_ref[...]
        mn2 = jnp.maximum(mx2, sn)
        a2 = jnp.exp(mx2 - mn2)
        pn = jnp.exp(sn - mn2)
        l = a2 * l_ref[...] + pn
        acc = a2 * acc_ref[...] + pn * vn
        on = acc / l
        lam = _lambda_value(lq1, lk1, lq2, lk2)
        od = on[:ATT_HEADS] - lam * on[ATT_HEADS:]
        own = (lax.broadcasted_iota(I32, od.shape, 1) // LANES) == lax.broadcasted_iota(I32, od.shape, 0)
        od = jnp.where(own, od, 0.0)
        ms = jnp.sum(od * od, axis=1, keepdims=True) / (2 * ATT_HEAD_DIM)
        y = od * lax.rsqrt(ms + RMS_EPS) * subln_ref[...] * (1.0 - LAMBDA_INIT)
        o_ref[...] = jnp.sum(y, axis=0, keepdims=True).astype(BF16)


def _attention_decode(q, k_new_bf, v_new_bf, cache_k, cache_v, page_table, lq1, lk1, lq2, lk2, subln):
    Bd = q.shape[0]
    n_pool, page = cache_k.shape[0], cache_k.shape[1]
    n_pages = page_table.shape[1]
    n_pg = PAGES_PER_STEP
    assert n_pages % n_pg == 0
    subln_w = jnp.tile(subln, (1, ATT_HEADS))

    def page_spec(g):
        return pl.BlockSpec((None, page, ATT_WIDTH), lambda b, j, pt: (pt[b * n_pages + j * n_pg + g], 0, 0))

    row_spec = pl.BlockSpec((None, 1, ATT_WIDTH), lambda b, j, pt: (b, 0, 0))
    small = lambda shape: pl.BlockSpec(shape, lambda b, j, pt: (0,) * len(shape))
    grid_spec = pltpu.PrefetchScalarGridSpec(
        num_scalar_prefetch=1, grid=(Bd, n_pages // n_pg),
        in_specs=[row_spec, row_spec, row_spec] + [small((1, ATT_HEAD_DIM))] * 4 + [small((1, ATT_WIDTH))]
                 + [page_spec(g) for g in range(n_pg)] * 2,
        out_specs=row_spec,
        scratch_shapes=[pltpu.VMEM((2 * ATT_HEADS, ATT_WIDTH), BF16),
                        pltpu.VMEM((2 * ATT_HEADS, 1), F32), pltpu.VMEM((2 * ATT_HEADS, 1), F32),
                        pltpu.VMEM((2 * ATT_HEADS, ATT_WIDTH), F32)])
    out = pl.pallas_call(
        functools.partial(_decode_attn_kernel, n_pg=n_pg), grid_spec=grid_spec,
        out_shape=jax.ShapeDtypeStruct((Bd, 1, ATT_WIDTH), BF16),
        compiler_params=_params(("parallel", "arbitrary")), name="attn_decode",
    )(page_table.reshape(-1), q.reshape(Bd, 1, -1), k_new_bf.reshape(Bd, 1, -1), v_new_bf.reshape(Bd, 1, -1),
      lq1, lk1, lq2, lk2, subln_w, *([cache_k] * n_pg), *([cache_v] * n_pg))
    return out.reshape(Bd, ATT_WIDTH)


def _hgrn_masks():
    C = HG_CHUNK
    t = np.arange(C)[:, None]
    r = np.arange(C)[None, :]
    blocks = [(r <= t), (r > t)]
    for B in HG_LEVELS:
        mid = (t // B) * B + B // 2 - 1
        second = (t % B) >= B // 2
        blocks.append(np.where(second, (r > mid) & (r <= t), (r > t) & (r <= mid)))
    return jnp.asarray(np.concatenate(blocks, axis=0).astype(np.float32), dtype=BF16)


def _lower_bound(lb_ref):
    a = lb_ref[...].astype(F32)
    e = jnp.exp(a - jnp.max(a, axis=0, keepdims=True))
    return e[0:1] / jnp.sum(e, axis=0, keepdims=True)


def _group_rows(x, j):
    return jnp.broadcast_to(x[:, j:j + 1, :], x.shape)


def _split3(x):
    hi = x.astype(BF16)
    r1 = x - hi.astype(F32)
    mid = r1.astype(BF16)
    lo = (r1 - mid.astype(F32)).astype(BF16)
    return hi, mid, lo


def _hgrn_chunk(q, z, v, lb, msk, s_t):
    C = HG_CHUNK
    logf = jnp.log(lb + (1.0 - lb) * jax.nn.sigmoid(z))
    kk = (1.0 - lb) * jax.nn.sigmoid(-z)
    e = sum(jnp.dot(msk, part, preferred_element_type=F32) for part in _split3(logf))
    b = e[0:C]
    qi = (q * jnp.exp(b)).astype(BF16)
    o = lax.dot_general(qi, s_t.astype(BF16), _NT, preferred_element_type=F32)
    kl = (kk * jnp.exp(e[C:2 * C])).astype(BF16)
    v_bf = v.astype(BF16)

    t_idx = lax.broadcasted_iota(I32, (C, 1), 0)
    row = lax.broadcasted_iota(I32, (C, C), 0)
    col = lax.broadcasted_iota(I32, (C, C), 1)
    a = jnp.zeros((C, C), F32)
    for li, B in enumerate(HG_LEVELS):
        x = jnp.exp(e[(2 + li) * C:(3 + li) * C])
        second = (t_idx % B) >= B // 2
        qt = jnp.where(second, q * x, 0.0).astype(BF16)
        kt = jnp.where(second, 0.0, kk * x).astype(BF16)
        al = lax.dot_general(qt, kt, _NT, preferred_element_type=F32)
        if B < C:
            al = jnp.where(row // B == col // B, al, 0.0)
        a = a + al
    o = o + jnp.dot(a.astype(BF16), v_bf, preferred_element_type=F32)

    G = C // SUBLANES
    q3, k3, v3, b3 = (x.reshape(G, SUBLANES, LANES) for x in (q, kk, v, b))
    p3 = lax.broadcasted_iota(I32, (G, SUBLANES, 1), 1)
    o3 = o.reshape(G, SUBLANES, LANES)
    for j in range(SUBLANES):
        term = q3 * _group_rows(k3, j) * jnp.exp(jnp.minimum(b3 - _group_rows(b3, j), 0.0))
        aj = jnp.where(p3 >= j, jnp.sum(term, axis=-1, keepdims=True), 0.0)
        o3 = o3 + aj * _group_rows(v3, j)
    o = o3.reshape(C, LANES)

    s_new = s_t * jnp.exp(b[C - 1:C, :]) + lax.dot_general(v_bf, kl, _TN, preferred_element_type=F32)
    return o, s_new


def _hgrn_kernel(q_ref, z_ref, v_ref, g_ref, lb_ref, hgn_ref, msk_ref, o_ref, st_ref, s_scr, *, n_chunks):
    i = pl.program_id(1)

    @pl.when(i == 0)
    def _():
        s_scr[...] = jnp.zeros(s_scr.shape, F32)

    lb = _lower_bound(lb_ref)
    msk = msk_ref[...]
    hgn = hgn_ref[...]

    def body(c, carry):
        off = pl.multiple_of(c * HG_CHUNK, HG_CHUNK)
        sl = pl.ds(off, HG_CHUNK)
        o, s_new = _hgrn_chunk(q_ref[sl, :], z_ref[sl, :], v_ref[sl, :], lb, msk, s_scr[...])
        s_scr[...] = s_new
        g = g_ref[sl, :]
        o_ref[sl, :] = (_rms(o, hgn) * (g * jax.nn.sigmoid(g))).astype(BF16)
        return carry

    lax.fori_loop(0, n_chunks, body, 0)

    @pl.when(i == pl.num_programs(1) - 1)
    def _():
        st_ref[...] = s_scr[...].T


def _hgrn_prompt(hg, hgrn_lb, hgrn_norm, tt):
    S = hg.shape[0]
    msk = _hgrn_masks()
    col = lambda seg: pl.BlockSpec((tt, LANES), lambda h, i, seg=seg: (i, seg * HG_HEADS + h))
    o, st = pl.pallas_call(
        functools.partial(_hgrn_kernel, n_chunks=tt // HG_CHUNK),
        grid=(HG_HEADS, S // tt),
        in_specs=[col(0), col(1), col(2), col(3),
                  pl.BlockSpec((hgrn_lb.shape[0], LANES), lambda h, i: (0, h)),
                  _small_spec((1, HG_VAL)), _small_spec(tuple(msk.shape))],
        out_specs=[pl.BlockSpec((tt, LANES), lambda h, i: (i, h)),
                   pl.BlockSpec((None, HG_KEY, HG_VAL), lambda h, i: (h, 0, 0))],
        out_shape=[jax.ShapeDtypeStruct((S, HG_WIDTH), BF16),
                   jax.ShapeDtypeStruct((HG_HEADS, HG_KEY, HG_VAL), F32)],
        scratch_shapes=[pltpu.VMEM((HG_VAL, HG_KEY), F32)],
        compiler_params=_params(("parallel", "arbitrary")), name="hgrn_prompt",
    )(hg, hg, hg, hg, hgrn_lb, hgrn_norm, msk)
    return o, st


def _hgrn_decode_kernel(hg_ref, s_ref, lb_ref, hgn_ref, o_ref, sn_ref):
    W = HG_WIDTH
    row = hg_ref[...]
    q, z, v, g = (row[:, s * W:(s + 1) * W] for s in range(4))
    lb = _lower_bound(lb_ref)
    f = lb + (1.0 - lb) * jax.nn.sigmoid(z)
    kk = (1.0 - lb) * jax.nn.sigmoid(-z)
    qf = q * f
    qk = q * kk
    pad = jnp.zeros((SUBLANES - 3, LANES), F32)
    outs = []
    for h in range(HG_HEADS):
        hs = slice(h * LANES, (h + 1) * LANES)
        cols = jnp.concatenate([f[:, hs], kk[:, hs], qf[:, hs], pad], axis=0).T
        f_c, k_c, qf_c = cols[:, 0:1], cols[:, 1:2], cols[:, 2:3]
        s0 = s_ref[h]
        vh = v[:, hs]
        sn_ref[h] = f_c * s0 + k_c * vh
        o = jnp.sum(qf_c * s0, axis=0, keepdims=True) + jnp.sum(qk[:, hs], axis=1, keepdims=True) * vh
        gh = g[:, hs]
        outs.append(_rms(o, hgn_ref[...]) * (gh * jax.nn.sigmoid(gh)))
    o_ref[...] = jnp.concatenate(outs, axis=1).astype(BF16)


def _hgrn_decode(hg, state, hgrn_lb, hgrn_norm):
    Bd = hg.shape[0]
    o, sn = pl.pallas_call(
        _hgrn_decode_kernel, grid=(Bd,),
        in_specs=[pl.BlockSpec((None, 1, 4 * HG_WIDTH), lambda b: (b, 0, 0)),
                  pl.BlockSpec((None, HG_HEADS, HG_KEY, HG_VAL), lambda b: (b, 0, 0, 0)),
                  _small_spec(tuple(hgrn_lb.shape)), _small_spec((1, HG_VAL))],
        out_specs=[pl.BlockSpec((None, 1, HG_WIDTH), lambda b: (b, 0, 0)),
                   pl.BlockSpec((None, HG_HEADS, HG_KEY, HG_VAL), lambda b: (b, 0, 0, 0))],
        out_shape=[jax.ShapeDtypeStruct((Bd, 1, HG_WIDTH), BF16),
                   jax.ShapeDtypeStruct(state.shape, F32)],
        compiler_params=_params(("parallel",)), name="hgrn_decode",
    )(hg.reshape(Bd, 1, -1), state, hgrn_lb, hgrn_norm)
    return o.reshape(Bd, HG_WIDTH), sn


def _merge_kernel(oa_ref, oh_ref, wa_ref, wb_ref, ga_ref, gb_ref, o_ref):
    a = jnp.dot(oa_ref[...], wa_ref[...].astype(BF16), preferred_element_type=F32)
    b = jnp.dot(oh_ref[...], wb_ref[...].astype(BF16), preferred_element_type=F32)
    o_ref[...] = (ga_ref[...] * a + gb_ref[...] * b).astype(BF16)


def _merge(oa, oh, w_pa, w_pb, gates, tm, tn, tag):
    M = oa.shape[0]
    nj = D_MODEL // tn
    return pl.pallas_call(
        _merge_kernel, grid=(M // tm, nj),
        in_specs=[pl.BlockSpec((tm, ATT_WIDTH), lambda i, j: (i, 0)),
                  pl.BlockSpec((tm, HG_WIDTH), lambda i, j: (i, 0)),
                  pl.BlockSpec((None, ATT_WIDTH, tn), lambda i, j: (0, 0, j)),
                  pl.BlockSpec((None, HG_WIDTH, tn), lambda i, j: (0, 0, j)),
                  pl.BlockSpec((tm, tn), lambda i, j: (i, j)),
                  pl.BlockSpec((tm, tn), lambda i, j: (i, j + nj))],
        out_specs=pl.BlockSpec((tm, tn), lambda i, j: (i, j)),
        out_shape=jax.ShapeDtypeStruct((M, D_MODEL), BF16),
        compiler_params=_params(("parallel", "arbitrary")), name=f"merge_{tag}",
    )(oa, oh, w_pa, w_pb, gates, gates)


def _cross_prompt_kernel(x_ref, g_ref, wq_ref, mk_ref, mv_ref, wo_ref, o_ref):
    x = x_ref[...]
    u = _rms(x, g_ref[...]).astype(BF16)
    q = jnp.dot(u, wq_ref[...], preferred_element_type=F32).astype(BF16)
    heads = []
    for h in range(X_HEADS):
        hs = slice(h * X_HEAD_DIM, (h + 1) * X_HEAD_DIM)
        s = lax.dot_general(q[:, hs], mk_ref[:, hs], _NT, preferred_element_type=F32) * (X_HEAD_DIM ** -0.5)
        p = jnp.exp(s - jnp.max(s, axis=1, keepdims=True))
        p = p / jnp.sum(p, axis=1, keepdims=True)
        heads.append(jnp.dot(p.astype(BF16), mv_ref[:, hs], preferred_element_type=F32))
    o = jnp.concatenate(heads, axis=1).astype(BF16)
    o_ref[...] = x + jnp.dot(o, wo_ref[...], preferred_element_type=F32)


def _cross_prompt(h1, g_cross, w_cq_bf, mk_bf, mv_bf, w_co_bf, tm):
    M = h1.shape[0]
    return pl.pallas_call(
        _cross_prompt_kernel, grid=(M // tm,),
        in_specs=[pl.BlockSpec((tm, D_MODEL), lambda i: (i, 0)), _small_spec((1, D_MODEL)),
                  _small_spec((D_MODEL, X_WIDTH)), _small_spec(tuple(mk_bf.shape)),
                  _small_spec(tuple(mv_bf.shape)), _small_spec((X_WIDTH, D_MODEL))],
        out_specs=pl.BlockSpec((tm, D_MODEL), lambda i: (i, 0)),
        out_shape=jax.ShapeDtypeStruct((M, D_MODEL), F32),
        compiler_params=_params(("parallel",)), name="cross_prompt",
    )(h1, g_cross.reshape(1, -1), w_cq_bf, mk_bf, mv_bf, w_co_bf)


def _cross_decode_kernel(q_ref, mk_ref, mv_ref, o_ref):
    q = q_ref[...]
    outs = []
    for h in range(X_HEADS):
        hs = slice(h * X_HEAD_DIM, (h + 1) * X_HEAD_DIM)
        s = jnp.sum(mk_ref[:, hs] * q[:, hs], axis=1, keepdims=True) * (X_HEAD_DIM ** -0.5)
        p = jnp.exp(s - jnp.max(s, axis=0, keepdims=True))
        p = p / jnp.sum(p, axis=0, keepdims=True)
        outs.append(jnp.sum(p * mv_ref[:, hs], axis=0, keepdims=True))
    o_ref[...] = jnp.concatenate(outs, axis=1).astype(BF16)


def _cross_decode(q, mem_k, mem_v):
    Bd, mem = mem_k.shape[0], mem_k.shape[1]
    row = pl.BlockSpec((None, 1, X_WIDTH), lambda b: (b, 0, 0))
    mem_spec = pl.BlockSpec((None, mem, X_WIDTH), lambda b: (b, 0, 0))
    out = pl.pallas_call(
        _cross_decode_kernel, grid=(Bd,), in_specs=[row, mem_spec, mem_spec], out_specs=row,
        out_shape=jax.ShapeDtypeStruct((Bd, 1, X_WIDTH), BF16),
        compiler_params=_params(("parallel",)), name="cross_decode",
    )(q.reshape(Bd, 1, X_WIDTH), mem_k.reshape(Bd, mem, X_WIDTH), mem_v.reshape(Bd, mem, X_WIDTH))
    return out.reshape(Bd, X_WIDTH)


def _route_kernel(x_ref, g_ref, w_ref, b_ref, cin_ref, uin_ref, u_ref, eid_ref, wt_ref, rank_ref, cnt_ref,
                  w3_scr, c_scr, *, n_rows):
    i = pl.program_id(0)
    tm = x_ref.shape[0]

    @pl.when(i == 0)
    def _():
        for part, ws in zip(_split3(w_ref[...]), (0, 1, 2)):
            w3_scr[ws] = part
        c_scr[...] = cin_ref[...]

    u = _rms(x_ref[...], g_ref[...])
    u_ref[...] = u
    u_hi, u_mid, u_lo = _split3(u)
    dot = lambda a, b: jnp.dot(a, b, preferred_element_type=F32)
    logits = (dot(u_hi, w3_scr[0]) + dot(u_hi, w3_scr[1]) + dot(u_mid, w3_scr[0])
              + dot(u_hi, w3_scr[2]) + dot(u_lo, w3_scr[0]) + dot(u_mid, w3_scr[1])) + b_ref[...]

    lane = lax.broadcasted_iota(I32, (tm, LANES), 1)
    big = jnp.int32(1 << 20)
    first_max = lambda vals, vmax: jnp.min(jnp.where(vals == vmax, lane, big), axis=1, keepdims=True)
    glv = jnp.where(lane < N_GROUPS, logits, -jnp.inf)
    gmax = jnp.max(glv, axis=1, keepdims=True)
    gsel = first_max(glv, gmax)
    p_group = 1.0 / jnp.sum(jnp.exp(glv - gmax), axis=1, keepdims=True)
    e_lane = lane - N_GROUPS
    in_group = (e_lane >= 0) & (e_lane < N_EXPERTS) & (e_lane // EXPERTS_PER_GROUP == gsel)
    ev = jnp.where(in_group, logits, -jnp.inf)
    v1 = jnp.max(ev, axis=1, keepdims=True)
    i1 = first_max(ev, v1)
    ev2 = jnp.where(lane == i1, -jnp.inf, ev)
    v2 = jnp.max(ev2, axis=1, keepdims=True)
    i2 = first_max(ev2, v2)
    t2 = jnp.exp(v2 - v1)
    w1 = p_group / (1.0 + t2)
    w2 = w1 * t2
    e1 = i1 - N_GROUPS
    e2 = i2 - N_GROUPS
    eid_ref[...] = jnp.where(lane == 0, e1, jnp.where(lane == 1, e2, 0))
    wt_ref[...] = jnp.where(lane == 0, w1, jnp.where(lane == 1, w2, 0.0))

    valid = (lax.broadcasted_iota(I32, (tm, 1), 0) + i * tm) < n_rows
    oh1 = jnp.where((lane == e1) & valid, 1.0, 0.0)
    oh2 = jnp.where((lane == e2) & valid, 1.0, 0.0)
    cnt = (oh1 + oh2).astype(BF16)
    r_i = lax.broadcasted_iota(I32, (tm, tm), 0)
    c_i = lax.broadcasted_iota(I32, (tm, tm), 1)
    strict = jnp.where(c_i < r_i, 1.0, 0.0).astype(BF16)
    before = dot(strict, cnt) + c_scr[...]
    r1 = jnp.sum(oh1 * before, axis=1, keepdims=True)
    r2 = jnp.sum(oh2 * before, axis=1, keepdims=True)
    rank_ref[...] = jnp.where(lane == 0, r1, jnp.where(lane == 1, r2, 0.0)).astype(I32)
    c_scr[...] = c_scr[...] + jnp.sum(oh1 + oh2, axis=0, keepdims=True)
    cnt_ref[...] = c_scr[...]


def _route(x, g_ffn, w_router, b_router, counts_in, u_all, row_off, tm, tag):
    M = x.shape[0]
    T = u_all.shape[0]
    boff = row_off // tm
    kern = functools.partial(_route_kernel, n_rows=M)
    lane_out = lambda dt: jax.ShapeDtypeStruct((M, LANES), dt)
    outs = pl.pallas_call(
        kern, grid=(M // tm,),
        in_specs=[pl.BlockSpec((tm, D_MODEL), lambda i: (i, 0)), _small_spec((1, D_MODEL)),
                  _small_spec((D_MODEL, LANES)), _small_spec((1, LANES)), _small_spec((1, LANES)),
                  pl.BlockSpec(memory_space=pl.ANY)],
        out_specs=[pl.BlockSpec((tm, D_MODEL), lambda i: (i + boff, 0)),
                   pl.BlockSpec((tm, LANES), lambda i: (i, 0)), pl.BlockSpec((tm, LANES), lambda i: (i, 0)),
                   pl.BlockSpec((tm, LANES), lambda i: (i, 0)), _small_spec((1, LANES))],
        out_shape=[jax.ShapeDtypeStruct((T, D_MODEL), F32), lane_out(I32), lane_out(F32), lane_out(I32),
                   jax.ShapeDtypeStruct((1, LANES), F32)],
        scratch_shapes=[pltpu.VMEM((3, D_MODEL, LANES), BF16), pltpu.VMEM((1, LANES), F32)],
        input_output_aliases={5: 0},
        compiler_params=_params(("arbitrary",)), name=f"moe_route_{tag}",
    )(x, g_ffn.reshape(1, -1), w_router, b_router, counts_in, u_all)
    return outs


def _plan_kernel(cnt_ref, eid_ref, rank_ref, dest_ref, rowtok_ref, blkexp_ref, nblk_ref, pstart_ref,
                 *, n_assign, n_blocks):
    def starts(e, acc):
        pstart_ref[e] = acc
        c = cnt_ref[e]
        return acc + ((c + MOE_ROWS - 1) // MOE_ROWS) * MOE_ROWS

    total = lax.fori_loop(0, N_EXPERTS, starts, jnp.int32(0))
    used = total // MOE_ROWS
    nblk_ref[0] = used

    def zero(r, c):
        rowtok_ref[r] = 0
        return c

    lax.fori_loop(0, n_blocks * MOE_ROWS, zero, 0)

    def place(a, c):
        d = pstart_ref[eid_ref[a]] + rank_ref[a]
        dest_ref[a] = d
        rowtok_ref[d] = a // 2
        return c

    lax.fori_loop(0, n_assign, place, 0)

    def blocks(e, c):
        lo = pstart_ref[e] // MOE_ROWS
        hi = lo + (cnt_ref[e] + MOE_ROWS - 1) // MOE_ROWS

        def fill(b, cc):
            blkexp_ref[b] = e
            return cc

        lax.fori_loop(lo, hi, fill, 0)
        return c

    def tail(b, c):
        blkexp_ref[b] = N_EXPERTS - 1
        return c

    lax.fori_loop(0, n_blocks, tail, 0)
    lax.fori_loop(0, N_EXPERTS, blocks, 0)


def _plan(counts, eid, rank, n_blocks):
    A = eid.shape[0]
    smem = pl.BlockSpec(memory_space=pltpu.SMEM)
    return pl.pallas_call(
        functools.partial(_plan_kernel, n_assign=A, n_blocks=n_blocks),
        in_specs=[smem, smem, smem], out_specs=[smem, smem, smem, smem],
        out_shape=[jax.ShapeDtypeStruct((A,), I32), jax.ShapeDtypeStruct((n_blocks * MOE_ROWS,), I32),
                   jax.ShapeDtypeStruct((n_blocks,), I32), jax.ShapeDtypeStruct((1,), I32)],
        scratch_shapes=[pltpu.SMEM((N_EXPERTS,), I32)], name="moe_plan",
    )(counts, eid, rank)


def _expert_kernel(rowtok_ref, blkexp_ref, nblk_ref, u_hbm, wg_ref, wu_ref, wd_ref, o_ref,
                   xbuf, sem, wg_s, wu_s, wd_s):
    i = pl.program_id(0)

    @pl.when(i < nblk_ref[0])
    def _():
        base = i * MOE_ROWS

        def row_copy(r):
            return pltpu.make_async_copy(u_hbm.at[pl.ds(rowtok_ref[base + r], 1), :],
                                         xbuf.at[pl.ds(r, 1), :], sem)

        def start(r, c):
            row_copy(r).start()
            return c

        lax.fori_loop(0, MOE_ROWS, start, 0)

        prev = blkexp_ref[jnp.maximum(i - 1, 0)]

        @pl.when((i == 0) | (blkexp_ref[i] != prev))
        def _():
            wg_s[...] = wg_ref[...].astype(BF16)
            wu_s[...] = wu_ref[...].astype(BF16)
            wd_s[...] = wd_ref[...].astype(BF16)

        def wait(r, c):
            row_copy(r).wait()
            return c

        lax.fori_loop(0, MOE_ROWS, wait, 0)
        x = xbuf[...].astype(BF16)
        hg = jnp.dot(x, wg_s[...], preferred_element_type=F32)
        hu = jnp.dot(x, wu_s[...], preferred_element_type=F32)
        h = (hg * jax.nn.sigmoid(hg) * hu).astype(BF16)
        o_ref[...] = jnp.dot(h, wd_s[...], preferred_element_type=F32)

    @pl.when(i >= nblk_ref[0])
    def _():
        o_ref[...] = jnp.zeros(o_ref.shape, F32)


def _experts(u_all, row_tok, blk_exp, n_used, we_g, we_u, we_d, n_blocks):
    wspec = lambda k, n: pl.BlockSpec((None, None, k, n), lambda i, rt, be, nb: (0, be[i], 0, 0))
    grid_spec = pltpu.PrefetchScalarGridSpec(
        num_scalar_prefetch=3, grid=(n_blocks,),
        in_specs=[pl.BlockSpec(memory_space=pl.ANY), wspec(D_MODEL, EXPERT_FF), wspec(D_MODEL, EXPERT_FF),
                  wspec(EXPERT_FF, D_MODEL)],
        out_specs=pl.BlockSpec((MOE_ROWS, D_MODEL), lambda i, rt, be, nb: (i, 0)),
        scratch_shapes=[pltpu.VMEM((MOE_ROWS, D_MODEL), F32), pltpu.SemaphoreType.DMA(()),
                        pltpu.VMEM((D_MODEL, EXPERT_FF), BF16), pltpu.VMEM((D_MODEL, EXPERT_FF), BF16),
                        pltpu.VMEM((EXPERT_FF, D_MODEL), BF16)])
    return pl.pallas_call(
        _expert_kernel, grid_spec=grid_spec,
        out_shape=jax.ShapeDtypeStruct((n_blocks * MOE_ROWS, D_MODEL), F32),
        compiler_params=_params(("arbitrary",)), name="moe_experts",
    )(row_tok, blk_exp, n_used, u_all, we_g, we_u, we_d)


def _combine_kernel(dest_ref, h_ref, wt_ref, gf_ref, yb_hbm, o_ref, ybuf0, ybuf1, sem, *, dest_off):
    i = pl.program_id(0)
    tm = h_ref.shape[0]
    base = dest_off + i * (2 * tm)

    def row_copy(r, k, buf):
        return pltpu.make_async_copy(yb_hbm.at[pl.ds(dest_ref[base + 2 * r + k], 1), :],
                                     buf.at[pl.ds(r, 1), :], sem)

    def start(r, c):
        row_copy(r, 0, ybuf0).start()
        row_copy(r, 1, ybuf1).start()
        return c

    def wait(r, c):
        row_copy(r, 0, ybuf0).wait()
        row_copy(r, 1, ybuf1).wait()
        return c

    lax.fori_loop(0, tm, start, 0)
    lax.fori_loop(0, tm, wait, 0)
    wt = wt_ref[...]
    h = h_ref[...] + (wt[:, 0:1] * ybuf0[...] + wt[:, 1:2] * ybuf1[...])
    o_ref[...] = _rms(h, gf_ref[...])


def _combine(h2, wts, dest, yb, norm_final, dest_off, tm, tag):
    M = h2.shape[0]
    grid_spec = pltpu.PrefetchScalarGridSpec(
        num_scalar_prefetch=1, grid=(M // tm,),
        in_specs=[pl.BlockSpec((tm, D_MODEL), lambda i, d: (i, 0)), pl.BlockSpec((tm, LANES), lambda i, d: (i, 0)),
                  pl.BlockSpec((1, D_MODEL), lambda i, d: (0, 0)), pl.BlockSpec(memory_space=pl.ANY)],
        out_specs=pl.BlockSpec((tm, D_MODEL), lambda i, d: (i, 0)),
        scratch_shapes=[pltpu.VMEM((tm, D_MODEL), F32), pltpu.VMEM((tm, D_MODEL), F32),
                        pltpu.SemaphoreType.DMA(())])
    return pl.pallas_call(
        functools.partial(_combine_kernel, dest_off=dest_off), grid_spec=grid_spec,
        out_shape=jax.ShapeDtypeStruct((M, D_MODEL), F32),
        compiler_params=_params(("arbitrary",)), name=f"moe_combine_{tag}",
    )(dest, h2, wts, norm_final.reshape(1, -1), yb)


def _moe_and_final_norm(h2_p, h2_s, g_ffn, wr_g, br_g, wr_e, br_e, we_g, we_u, we_d, norm_final, tm_p):
    Tp, Ts = h2_p.shape[0], h2_s.shape[0]
    T = Tp + Ts
    pad = LANES - N_GROUPS - N_EXPERTS
    w_router = jnp.concatenate([wr_g, wr_e, jnp.zeros((D_MODEL, pad), F32)], axis=1)
    b_router = jnp.concatenate([br_g, br_e, jnp.zeros((pad,), F32)]).reshape(1, LANES)
    u_all = jnp.zeros((T, D_MODEL), F32)
    zero_counts = jnp.zeros((1, LANES), F32)
    u_all, eid_p, wt_p, rank_p, counts = _route(h2_p, g_ffn, w_router, b_router, zero_counts, u_all, 0, tm_p, "p")
    u_all, eid_s, wt_s, rank_s, counts = _route(h2_s, g_ffn, w_router, b_router, counts, u_all, Tp, Ts, "s")
    eid = jnp.concatenate([eid_p[:, :2].reshape(-1), eid_s[:, :2].reshape(-1)])
    rank = jnp.concatenate([rank_p[:, :2].reshape(-1), rank_s[:, :2].reshape(-1)])
    A = 2 * T
    n_blocks = (A + N_EXPERTS * (MOE_ROWS - 1)) // MOE_ROWS + 1
    dest, row_tok, blk_exp, n_used = _plan(counts[0, :N_EXPERTS].astype(I32), eid, rank, n_blocks)
    yb = _experts(u_all, row_tok, blk_exp, n_used, we_g, we_u, we_d, n_blocks)
    y_p = _combine(h2_p, wt_p, dest, yb, norm_final, 0, 256, "p")
    y_s = _combine(h2_s, wt_s, dest, yb, norm_final, 2 * Tp, Ts, "s")
    return y_p, y_s


def kernel(x_prompt, x_sample, mem_prompt, cache_k, cache_v, cache_mem_k, cache_mem_v, state_hgrn, page_table,
           norm_mix, w_in, lambda_q1, lambda_k1, lambda_q2, lambda_k2, subln, hgrn_lb, hgrn_norm, w_pa, w_pb,
           w_out, norm_cross, w_cq, w_ck, w_cv, w_co, norm_ffn, w_router_group, b_router_group,
           w_router_expert, b_router_expert, w_e_gate, w_e_up, w_e_down, norm_final):
    assert w_in.shape[0] == 1, "single-layer step"
    Bp, S, D = x_prompt.shape
    Bd, Ld, _ = x_sample.shape
    assert Bp == 1 and Ld == 1
    n_pages = page_table.shape[1]
    page = cache_k.shape[2]
    past_len = n_pages * page
    xp = x_prompt.reshape(S, D)
    xs = x_sample.reshape(Bd, D)
    lam = (lambda_q1, lambda_k1, lambda_q2, lambda_k2)
    TM = 512

    q, k, k_bf, v, v_bf, hg, gates = _in_projection(xp, norm_mix[0], w_in, jnp.arange(S, dtype=I32), TM, 512, "p")
    oa = _attention_prompt(q, k_bf, v_bf, *lam, subln, 512)
    oh, st_p = _hgrn_prompt(hg, hgrn_lb, hgrn_norm, 256)
    merged = _merge(oa, oh, w_pa, w_pb, gates, TM, 512, "p")
    (h1_p,) = _mm(merged, w_out, col_off=0, n_cols=D, tm=TM, tn=512, epilogue=_epi_residual, out_dtypes=[F32],
                  extras=[(xp, pl.BlockSpec((TM, 512), lambda i, j: (i, j)))], name="outproj_p")

    pos_s = jnp.full((Bd,), past_len, I32)
    qs, ks, ks_bf, vs, vs_bf, hgs, gates_s = _in_projection(xs, norm_mix[0], w_in, pos_s, Bd, 512, "s")
    ck = cache_k[0].reshape(cache_k.shape[1], page, ATT_WIDTH)
    cv = cache_v[0].reshape(cache_v.shape[1], page, ATT_WIDTH)
    oa_s = _attention_decode(qs, ks_bf, vs_bf, ck, cv, page_table, *lam, subln)
    oh_s, st_s = _hgrn_decode(hgs, state_hgrn[0], hgrn_lb, hgrn_norm)
    merged_s = _merge(oa_s, oh_s, w_pa, w_pb, gates_s, Bd, 512, "s")
    (h1_s,) = _mm(merged_s, w_out, col_off=0, n_cols=D, tm=Bd, tn=512, epilogue=_epi_residual, out_dtypes=[F32],
                  extras=[(xs, pl.BlockSpec((Bd, 512), lambda i, j: (i, j)))], name="outproj_s")

    mem = mem_prompt.reshape(-1, D)
    mk, mk_bf = _mm(mem, w_ck, col_off=0, n_cols=X_WIDTH, tm=mem.shape[0], tn=X_WIDTH, epilogue=_epi_v,
                    out_dtypes=[F32, BF16], name="mem_k")
    mv, mv_bf = _mm(mem, w_cv, col_off=0, n_cols=X_WIDTH, tm=mem.shape[0], tn=X_WIDTH, epilogue=_epi_v,
                    out_dtypes=[F32, BF16], name="mem_v")
    w_cq_bf = w_cq[0].astype(BF16)
    w_co_bf = w_co[0].astype(BF16)
    h2_p = _cross_prompt(h1_p, norm_cross[0], w_cq_bf, mk_bf, mv_bf, w_co_bf, 256)
    (qc_s,) = _mm(h1_s, w_cq, col_off=0, n_cols=X_WIDTH, tm=Bd, tn=X_WIDTH, epilogue=_epi_plain,
                  out_dtypes=[F32], norm_g=norm_cross[0], name="cross_q_s")
    oc_s = _cross_decode(qc_s, cache_mem_k[0], cache_mem_v[0])
    (h2_s,) = _mm(oc_s, w_co, col_off=0, n_cols=D, tm=Bd, tn=512, epilogue=_epi_residual, out_dtypes=[F32],
                  extras=[(h1_s, pl.BlockSpec((Bd, 512), lambda i, j: (i, j)))], name="cross_o_s")

    y_p, y_s = _moe_and_final_norm(h2_p, h2_s, norm_ffn[0], w_router_group[0], b_router_group[0],
                                   w_router_expert[0], b_router_expert[0], w_e_gate, w_e_up, w_e_down,
                                   norm_final, 256)

    return (y_p.reshape(Bp, S, D), y_s.reshape(Bd, Ld, D),
            k.reshape(1, Bp, S, ATT_HEADS, 2, ATT_HEAD_DIM), v.reshape(1, Bp, S, ATT_HEADS, 2 * ATT_HEAD_DIM),
            ks.reshape(1, Bd, Ld, ATT_HEADS, 2, ATT_HEAD_DIM), vs.reshape(1, Bd, Ld, ATT_HEADS, 2 * ATT_HEAD_DIM),
            st_p.reshape(1, Bp, HG_HEADS, HG_KEY, HG_VAL), st_s.reshape(1, Bd, HG_HEADS, HG_KEY, HG_VAL),
            mk.reshape(1, Bp, -1, X_HEADS, X_HEAD_DIM), mv.reshape(1, Bp, -1, X_HEADS, X_HEAD_DIM))
```

```python
import functools
import math

import numpy as np
import jax
import jax.numpy as jnp
from jax import lax
from jax.experimental import pallas as pl
from jax.experimental.pallas import tpu as pltpu

F32 = jnp.float32
BF16 = jnp.bfloat16
I32 = jnp.int32

D_MODEL = 2048
ATT_HEADS = 8
ATT_HEAD_DIM = 64
ATT_WIDTH = ATT_HEADS * 2 * ATT_HEAD_DIM
ROPE_THETA = 10000.0
HG_HEADS = 8
HG_KEY = 128
HG_VAL = 128
HG_WIDTH = HG_HEADS * HG_VAL
X_HEADS = 4
X_HEAD_DIM = 128
X_WIDTH = X_HEADS * X_HEAD_DIM
N_GROUPS = 4
EXPERTS_PER_GROUP = 8
N_EXPERTS = N_GROUPS * EXPERTS_PER_GROUP
EXPERT_FF = 512
RMS_EPS = 1e-6
NEG_INF = -1e30
LAMBDA_INIT = 0.8 - 0.6 * math.exp(-0.3 * 0)

LANES = 128
SUBLANES = 8
VMEM_LIMIT = 52 * 1024 * 1024

HG_CHUNK = 64
HG_LEVELS = (16, 32, 64)
MOE_ROWS = 128
PAGES_PER_STEP = 8

_NT = (((1,), (1,)), ((), ()))
_TN = (((0,), (0,)), ((), ()))


def _params(sem):
    return pltpu.CompilerParams(dimension_semantics=sem, vmem_limit_bytes=VMEM_LIMIT)


def _rms(x, g):
    return x * lax.rsqrt(jnp.mean(x * x, axis=-1, keepdims=True) + RMS_EPS) * g


def _mm_kernel(*refs, n_extra, n_out, norm, epilogue):
    x_ref = refs[0]
    pos = 1
    if norm:
        g_ref = refs[1]
        pos = 2
    w_ref = refs[pos]
    extras = refs[pos + 1:pos + 1 + n_extra]
    outs = refs[pos + 1 + n_extra:pos + 1 + n_extra + n_out]
    if norm:
        u_ref = refs[-1]

        @pl.when(pl.program_id(1) == 0)
        def _():
            u_ref[...] = _rms(x_ref[...].astype(F32), g_ref[...]).astype(BF16)

        u = u_ref[...]
    else:
        u = x_ref[...].astype(BF16)
    acc = jnp.dot(u, w_ref[...].astype(BF16), preferred_element_type=F32)
    epilogue(acc, extras, outs)


def _mm(x, w, *, col_off, n_cols, tm, tn, epilogue, out_dtypes, norm_g=None, extras=(), name):
    M, K = x.shape
    assert M % tm == 0 and n_cols % tn == 0 and col_off % tn == 0
    joff = col_off // tn
    if w.ndim == 3:
        w_spec = pl.BlockSpec((None, K, tn), lambda i, j: (0, 0, j + joff))
    else:
        w_spec = pl.BlockSpec((K, tn), lambda i, j: (0, j + joff))
    in_specs = [pl.BlockSpec((tm, K), lambda i, j: (i, 0))]
    args = [x]
    if norm_g is not None:
        in_specs.append(pl.BlockSpec((1, K), lambda i, j: (0, 0)))
        args.append(norm_g.reshape(1, K))
    in_specs.append(w_spec)
    args.append(w)
    for arr, spec in extras:
        in_specs.append(spec)
        args.append(arr)
    out_specs = [pl.BlockSpec((tm, tn), lambda i, j: (i, j)) for _ in out_dtypes]
    out_shape = [jax.ShapeDtypeStruct((M, n_cols), dt) for dt in out_dtypes]
    scratch = [pltpu.VMEM((tm, K), BF16)] if norm_g is not None else []
    kern = functools.partial(_mm_kernel, n_extra=len(extras), n_out=len(out_dtypes),
                             norm=norm_g is not None, epilogue=epilogue)
    res = pl.pallas_call(
        kern, grid=(M // tm, n_cols // tn), in_specs=in_specs, out_specs=out_specs,
        out_shape=out_shape, scratch_shapes=scratch,
        compiler_params=_params(("parallel", "arbitrary")), name=name)(*args)
    return res


def _rope_tile(x, cos, sin_signed):
    first = (lax.broadcasted_iota(I32, (x.shape[0], LANES), 1) % ATT_HEAD_DIM) < ATT_HEAD_DIM // 2
    outs = []
    for c in range(x.shape[1] // LANES):
        xc = x[:, c * LANES:(c + 1) * LANES]
        rot = jnp.where(first, pltpu.roll(xc, LANES - ATT_HEAD_DIM // 2, 1),
                        pltpu.roll(xc, ATT_HEAD_DIM // 2, 1))
        outs.append(xc * cos + rot * sin_signed)
    return outs[0] if len(outs) == 1 else jnp.concatenate(outs, axis=1)


def _epi_q(acc, extras, outs):
    cos_ref, sin_ref = extras
    outs[0][...] = (_rope_tile(acc, cos_ref[...], sin_ref[...]) * (ATT_HEAD_DIM ** -0.5)).astype(BF16)


def _epi_k(acc, extras, outs):
    cos_ref, sin_ref = extras
    r = _rope_tile(acc, cos_ref[...], sin_ref[...])
    outs[0][...] = r
    outs[1][...] = r.astype(BF16)


def _epi_v(acc, extras, outs):
    outs[0][...] = acc
    outs[1][...] = acc.astype(BF16)


def _epi_plain(acc, extras, outs):
    outs[0][...] = acc.astype(outs[0].dtype)


def _epi_residual(acc, extras, outs):
    outs[0][...] = extras[0][...] + acc


def _rope_tables(pos):
    half = ATT_HEAD_DIM // 2
    freqs = ROPE_THETA ** (-jnp.arange(half, dtype=F32) / half)
    ang = pos.astype(F32)[:, None] * freqs[None, :]
    c, s = jnp.cos(ang), jnp.sin(ang)
    return jnp.tile(c, (1, 4)), jnp.concatenate([-s, s, -s, s], axis=1)


def _inproj_kernel(x_ref, g_ref, w_ref, cos_ref, sin_ref, q_ref, k_ref, kb_ref, v_ref, vb_ref, hg_ref, gt_ref,
                   u_ref, *, bounds):
    j = pl.program_id(1)

    @pl.when(j == 0)
    def _():
        u_ref[...] = _rms(x_ref[...].astype(F32), g_ref[...]).astype(BF16)

    acc = jnp.dot(u_ref[...], w_ref[...], preferred_element_type=F32)
    in_seg = lambda s: (j >= bounds[s]) & (j < bounds[s + 1])

    @pl.when(in_seg(0))
    def _():
        _epi_q(acc, (cos_ref, sin_ref), (q_ref,))

    @pl.when(in_seg(1))
    def _():
        _epi_k(acc, (cos_ref, sin_ref), (k_ref, kb_ref))

    @pl.when(in_seg(2))
    def _():
        _epi_v(acc, (), (v_ref, vb_ref))

    @pl.when(in_seg(3))
    def _():
        hg_ref[...] = acc

    @pl.when(in_seg(4))
    def _():
        gt_ref[...] = acc.astype(BF16)


def _in_projection(x, g, w_in_bf, pos, tm, tn, tag):
    M, K = x.shape
    cos, sin = _rope_tables(pos)
    widths = (ATT_WIDTH, ATT_WIDTH, ATT_WIDTH, 4 * HG_WIDTH, 2 * D_MODEL)
    bounds = tuple(int(b) for b in np.cumsum((0,) + widths) // tn)

    def seg_spec(s):
        lo, n = bounds[s], bounds[s + 1] - bounds[s]
        return pl.BlockSpec((tm, tn), lambda i, j: (i, jnp.clip(j - lo, 0, n - 1)))

    seg_of_out = (0, 1, 1, 2, 2, 3, 4)
    out_dtypes = (BF16, F32, BF16, F32, BF16, F32, BF16)
    row_tbl = pl.BlockSpec((tm, LANES), lambda i, j: (i, 0))
    return pl.pallas_call(
        functools.partial(_inproj_kernel, bounds=bounds),
        grid=(M // tm, bounds[-1]),
        in_specs=[pl.BlockSpec((tm, K), lambda i, j: (i, 0), pipeline_mode=pl.Buffered(1)),
                  pl.BlockSpec((1, K), lambda i, j: (0, 0)),
                  pl.BlockSpec((None, K, tn), lambda i, j: (0, 0, j)), row_tbl, row_tbl],
        out_specs=[seg_spec(s) for s in seg_of_out],
        out_shape=[jax.ShapeDtypeStruct((M, widths[s]), dt) for s, dt in zip(seg_of_out, out_dtypes)],
        scratch_shapes=[pltpu.VMEM((tm, K), BF16)],
        compiler_params=_params(("parallel", "arbitrary")), name=f"inproj_{tag}",
    )(x, g.reshape(1, K), w_in_bf, cos, sin)


def _lambda_value(lq1, lk1, lq2, lk2):
    return (jnp.exp(jnp.sum(lq1[...] * lk1[...], axis=-1, keepdims=True))
            - jnp.exp(jnp.sum(lq2[...] * lk2[...], axis=-1, keepdims=True)) + LAMBDA_INIT)


def _attn_kernel(q_ref, k_ref, v_ref, lq1, lk1, lq2, lk2, subln_ref, o_ref, *, tq):
    i = pl.program_id(1)
    q = q_ref[...]
    lane = lax.broadcasted_iota(I32, q.shape, 1)
    zero = jnp.zeros_like(q)
    qm = (jnp.where(lane < ATT_HEAD_DIM, q, zero), jnp.where(lane >= ATT_HEAD_DIM, q, zero))

    def update(kt, vt, carry, mask):
        new = []
        for m in range(2):
            mx, l, acc = carry[3 * m:3 * m + 3]
            s = lax.dot_general(qm[m], kt, _NT, preferred_element_type=F32)
            if mask is not None:
                s = jnp.where(mask, s, NEG_INF)
            mn = jnp.maximum(mx, jnp.max(s, axis=1, keepdims=True))
            alpha = jnp.exp(mx - mn)
            p = jnp.exp(s - mn)
            l = alpha * l + jnp.sum(p, axis=1, keepdims=True)
            acc = alpha * acc + jnp.dot(p.astype(BF16), vt, preferred_element_type=F32)
            new += [mn, l, acc]
        return tuple(new)

    def body(j, carry):
        off = pl.multiple_of(j * tq, tq)
        return update(k_ref[pl.ds(off, tq), :], v_ref[pl.ds(off, tq), :], carry, None)

    init = []
    for _ in range(2):
        init += [jnp.full((tq, 1), NEG_INF, F32), jnp.zeros((tq, 1), F32), jnp.zeros((tq, LANES), F32)]
    carry = lax.fori_loop(0, i, body, tuple(init))
    off = pl.multiple_of(i * tq, tq)
    causal = (lax.broadcasted_iota(I32, (tq, tq), 1) <= lax.broadcasted_iota(I32, (tq, tq), 0))
    _, l0, a0, _, l1, a1 = update(k_ref[pl.ds(off, tq), :], v_ref[pl.ds(off, tq), :], carry, causal)

    lam = _lambda_value(lq1, lk1, lq2, lk2)
    o = a0 / l0 - lam * (a1 / l1)
    o_ref[...] = (_rms(o, subln_ref[...]) * (1.0 - LAMBDA_INIT)).astype(BF16)


def _small_spec(shape):
    nd = len(shape)
    return pl.BlockSpec(shape, lambda *_: (0,) * nd)


def _attention_prompt(q, k_bf, v_bf, lq1, lk1, lq2, lk2, subln, tq):
    S = q.shape[0]
    lam_specs = [_small_spec((1, ATT_HEAD_DIM))] * 4
    return pl.pallas_call(
        functools.partial(_attn_kernel, tq=tq),
        grid=(ATT_HEADS, S // tq),
        in_specs=[pl.BlockSpec((tq, LANES), lambda h, i: (i, h)),
                  pl.BlockSpec((S, LANES), lambda h, i: (0, h)),
                  pl.BlockSpec((S, LANES), lambda h, i: (0, h))] + lam_specs
                 + [_small_spec((1, LANES))],
        out_specs=pl.BlockSpec((tq, LANES), lambda h, i: (i, h)),
        out_shape=jax.ShapeDtypeStruct((S, ATT_WIDTH), BF16),
        compiler_params=_params(("parallel", "arbitrary")), name="attn_prompt",
    )(q, k_bf, v_bf, lq1, lk1, lq2, lk2, subln)


def _decode_attn_kernel(pt_ref, q_ref, kn_ref, vn_ref, lq1, lk1, lq2, lk2, subln_ref, *rest, n_pg):
    k_refs = rest[:n_pg]
    v_refs = rest[n_pg:2 * n_pg]
    o_ref = rest[2 * n_pg]
    qr_ref, m_ref, l_ref, acc_ref = rest[2 * n_pg + 1:]
    j = pl.program_id(1)
    nrow = 2 * ATT_HEADS
    row = lax.broadcasted_iota(I32, (nrow, ATT_WIDTH), 0)
    lane = lax.broadcasted_iota(I32, (nrow, ATT_WIDTH), 1)
    lane_head = lane // LANES
    lane_map = (lane // ATT_HEAD_DIM) % 2

    @pl.when(j == 0)
    def _():
        sel = (lane_head == row % ATT_HEADS) & (lane_map == row // ATT_HEADS)
        qb = jnp.broadcast_to(q_ref[...].astype(F32), (nrow, ATT_WIDTH))
        qr_ref[...] = jnp.where(sel, qb, 0.0).astype(BF16)
        m_ref[...] = jnp.full(m_ref.shape, NEG_INF, F32)
        l_ref[...] = jnp.zeros(l_ref.shape, F32)
        acc_ref[...] = jnp.zeros(acc_ref.shape, F32)

    qr = qr_ref[...]
    s = jnp.concatenate(
        [jnp.dot(qr, kr[...].astype(BF16), preferred_element_type=F32) for kr in k_refs],
        axis=1)
    mx = m_ref[...]
    mn = jnp.maximum(mx, jnp.max(s, axis=1, keepdims=True))
    alpha = jnp.exp(mx - mn)
    p = jnp.exp(s - mn)
    l_ref[...] = alpha * l_ref[...] + jnp.sum(p, axis=1, keepdims=True)
    pb = p.astype(BF16)
    page = s.shape[1] // n_pg
    heads = []
    for h in range(ATT_HEADS):
        pv = jnp.zeros((nrow, LANES), F32)
        for g, vr in enumerate(v_refs):
            vh = vr[pl.ds(h, page, stride=ATT_HEADS), :]
            pv = pv + jnp.dot(pb[:, g * page:(g + 1) * page], vh.astype(BF16), preferred_element_type=F32)
        heads.append(pv)
    acc_ref[...] = alpha * acc_ref[...] + jnp.concatenate(heads, axis=1)
    m_ref[...] = mn

    @pl.when(j == pl.num_programs(1) - 1)
    def _():
        kn = kn_ref[...].astype(F32)
        vn = vn_ref[...].astype(F32)
        sn = jnp.sum(qr.astype(F32) * kn, axis=1, keepdims=True)
        mx2 = m_ref[...]
        mn2 = jnp.maximum(mx2, sn)
        a2 = jnp.exp(mx2 - mn2)
        pn = jnp.exp(sn - mn2)
        l = a2 * l_ref[...] + pn
        acc = a2 * acc_ref[...] + pn * vn
        on = acc / l
        lam = _lambda_value(lq1, lk1, lq2, lk2)
        od = on[:ATT_HEADS] - lam * on[ATT_HEADS:]
        own = (lax.broadcasted_iota(I32, od.shape, 1) // LANES) == lax.broadcasted_iota(I32, od.shape, 0)
        od = jnp.where(own, od, 0.0)
        ms = jnp.sum(od * od, axis=1, keepdims=True) / (2 * ATT_HEAD_DIM)
        y = od * lax.rsqrt(ms + RMS_EPS) * subln_ref[...] * (1.0 - LAMBDA_INIT)
        o_ref[...] = jnp.sum(y, axis=0, keepdims=True).astype(BF16)


def _attention_decode(q, k_new_bf, v_new_bf, cache_kt, cache_v, page_table, lq1, lk1, lq2, lk2, subln):
    Bd = q.shape[0]
    page = cache_kt.shape[2]
    n_pages = page_table.shape[1]
    n_pg = PAGES_PER_STEP
    assert n_pages % n_pg == 0
    subln_w = jnp.tile(subln, (1, ATT_HEADS))

    def k_spec(g):
        return pl.BlockSpec((None, ATT_WIDTH, page), lambda b, j, pt: (pt[b * n_pages + j * n_pg + g], 0, 0))

    def v_spec(g):
        return pl.BlockSpec((None, page * ATT_HEADS, 2 * ATT_HEAD_DIM),
                            lambda b, j, pt: (pt[b * n_pages + j * n_pg + g], 0, 0))

    row_spec = pl.BlockSpec((None, 1, ATT_WIDTH), lambda b, j, pt: (b, 0, 0))
    small = lambda shape: pl.BlockSpec(shape, lambda b, j, pt: (0,) * len(shape))
    grid_spec = pltpu.PrefetchScalarGridSpec(
        num_scalar_prefetch=1, grid=(Bd, n_pages // n_pg),
        in_specs=[row_spec, row_spec, row_spec] + [small((1, ATT_HEAD_DIM))] * 4 + [small((1, ATT_WIDTH))]
                 + [k_spec(g) for g in range(n_pg)] + [v_spec(g) for g in range(n_pg)],
        out_specs=row_spec,
        scratch_shapes=[pltpu.VMEM((2 * ATT_HEADS, ATT_WIDTH), BF16),
                        pltpu.VMEM((2 * ATT_HEADS, 1), F32), pltpu.VMEM((2 * ATT_HEADS, 1), F32),
                        pltpu.VMEM((2 * ATT_HEADS, ATT_WIDTH), F32)])
    out = pl.pallas_call(
        functools.partial(_decode_attn_kernel, n_pg=n_pg), grid_spec=grid_spec,
        out_shape=jax.ShapeDtypeStruct((Bd, 1, ATT_WIDTH), BF16),
        compiler_params=_params(("parallel", "arbitrary")), name="attn_decode",
    )(page_table.reshape(-1), q.reshape(Bd, 1, -1), k_new_bf.reshape(Bd, 1, -1), v_new_bf.reshape(Bd, 1, -1),
      lq1, lk1, lq2, lk2, subln_w, *([cache_kt] * n_pg), *([cache_v] * n_pg))
    return out.reshape(Bd, ATT_WIDTH)


def _hgrn_masks():
    C = HG_CHUNK
    t = np.arange(C)[:, None]
    r = np.arange(C)[None, :]
    blocks = [(r <= t), (r > t)]
    for B in HG_LEVELS:
        mid = (t // B) * B + B // 2 - 1
        second = (t % B) >= B // 2
        blocks.append(np.where(second, (r > mid) & (r <= t), (r > t) & (r <= mid)))
    return jnp.asarray(np.concatenate(blocks, axis=0).astype(np.float32), dtype=BF16)


def _lower_bound(lb_ref):
    a = lb_ref[...].astype(F32)
    e = jnp.exp(a - jnp.max(a, axis=0, keepdims=True))
    return e[0:1] / jnp.sum(e, axis=0, keepdims=True)


def _group_rows(x, j):
    return jnp.broadcast_to(x[:, j:j + 1, :], x.shape)


def _split3(x):
    hi = x.astype(BF16)
    r1 = x - hi.astype(F32)
    mid = r1.astype(BF16)
    lo = (r1 - mid.astype(F32)).astype(BF16)
    return hi, mid, lo


def _hgrn_tile(q, z, v, lb, msk, s_t):
    C = HG_CHUNK
    n = q.shape[0] // C
    dot = lambda a, b: jnp.dot(a, b, preferred_element_type=F32)
    rows = lambda x, c: x[c * C:(c + 1) * C]
    chunks = range(n)

    logf = jnp.log(lb + (1.0 - lb) * jax.nn.sigmoid(z))
    kk = (1.0 - lb) * jax.nn.sigmoid(-z)
    v_bf = v.astype(BF16)
    parts = _split3(logf)
    e = [sum(dot(msk, rows(p, c)) for p in parts) for c in chunks]
    b = [ec[0:C] for ec in e]
    qi = [(rows(q, c) * jnp.exp(b[c])).astype(BF16) for c in chunks]
    kl = [(rows(kk, c) * jnp.exp(e[c][C:2 * C])).astype(BF16) for c in chunks]
    kv = [lax.dot_general(rows(v_bf, c), kl[c], _TN, preferred_element_type=F32) for c in chunks]

    t_idx = lax.broadcasted_iota(I32, (C, 1), 0)
    row = lax.broadcasted_iota(I32, (C, C), 0)
    col = lax.broadcasted_iota(I32, (C, C), 1)
    a = [jnp.zeros((C, C), F32) for _ in chunks]
    for li, B in enumerate(HG_LEVELS):
        second = (t_idx % B) >= B // 2
        same_block = (row // B) == (col // B)
        for c in chunks:
            x = jnp.exp(e[c][(2 + li) * C:(3 + li) * C])
            qt = jnp.where(second, rows(q, c) * x, 0.0).astype(BF16)
            kt = jnp.where(second, 0.0, rows(kk, c) * x).astype(BF16)
            al = lax.dot_general(qt, kt, _NT, preferred_element_type=F32)
            a[c] = a[c] + (jnp.where(same_block, al, 0.0) if B < C else al)

    states = [s_t]
    for c in chunks:
        states.append(states[-1] * jnp.exp(b[c][C - 1:C, :]) + kv[c])
    o = [lax.dot_general(qi[c], states[c].astype(BF16), _NT, preferred_element_type=F32)
         + dot(a[c].astype(BF16), rows(v_bf, c)) for c in chunks]

    G = n * C // SUBLANES
    b_all = jnp.concatenate(b, axis=0) if n > 1 else b[0]
    q3, k3, v3, b3 = (x.reshape(G, SUBLANES, LANES) for x in (q, kk, v, b_all))
    p3 = lax.broadcasted_iota(I32, (G, SUBLANES, 1), 1)
    o3 = (jnp.concatenate(o, axis=0) if n > 1 else o[0]).reshape(G, SUBLANES, LANES)
    for j in range(SUBLANES):
        term = q3 * _group_rows(k3, j) * jnp.exp(jnp.minimum(b3 - _group_rows(b3, j), 0.0))
        aj = jnp.where(p3 >= j, jnp.sum(term, axis=-1, keepdims=True), 0.0)
        o3 = o3 + aj * _group_rows(v3, j)
    return o3.reshape(n * C, LANES), states[-1]


def _hgrn_kernel(q_ref, z_ref, v_ref, g_ref, lb_ref, hgn_ref, msk_ref, o_ref, st_ref, s_scr):
    i = pl.program_id(1)

    @pl.when(i == 0)
    def _():
        s_scr[...] = jnp.zeros(s_scr.shape, F32)

    o, s_t = _hgrn_tile(q_ref[...], z_ref[...], v_ref[...], _lower_bound(lb_ref), msk_ref[...], s_scr[...])
    s_scr[...] = s_t
    g = g_ref[...]
    o_ref[...] = (_rms(o, hgn_ref[...]) * (g * jax.nn.sigmoid(g))).astype(BF16)

    @pl.when(i == pl.num_programs(1) - 1)
    def _():
        st_ref[...] = s_scr[...].T


def _hgrn_prompt(hg, hgrn_lb, hgrn_norm, tt):
    S = hg.shape[0]
    msk = _hgrn_masks()
    col = lambda seg: pl.BlockSpec((tt, LANES), lambda h, i, seg=seg: (i, seg * HG_HEADS + h))
    o, st = pl.pallas_call(
        _hgrn_kernel, grid=(HG_HEADS, S // tt),
        in_specs=[col(0), col(1), col(2), col(3),
                  pl.BlockSpec((hgrn_lb.shape[0], LANES), lambda h, i: (0, h)),
                  _small_spec((1, HG_VAL)), _small_spec(tuple(msk.shape))],
        out_specs=[pl.BlockSpec((tt, LANES), lambda h, i: (i, h)),
                   pl.BlockSpec((None, HG_KEY, HG_VAL), lambda h, i: (h, 0, 0))],
        out_shape=[jax.ShapeDtypeStruct((S, HG_WIDTH), BF16),
                   jax.ShapeDtypeStruct((HG_HEADS, HG_KEY, HG_VAL), F32)],
        scratch_shapes=[pltpu.VMEM((HG_VAL, HG_KEY), F32)],
        compiler_params=_params(("parallel", "arbitrary")), name="hgrn_prompt",
    )(hg, hg, hg, hg, hgrn_lb, hgrn_norm, msk)
    return o, st


def _hgrn_decode_kernel(hg_ref, s_ref, lb_ref, hgn_ref, o_ref, sn_ref):
    W = HG_WIDTH
    row = hg_ref[...]
    q, z, v, g = (row[:, s * W:(s + 1) * W] for s in range(4))
    lb = _lower_bound(lb_ref)
    f = lb + (1.0 - lb) * jax.nn.sigmoid(z)
    kk = (1.0 - lb) * jax.nn.sigmoid(-z)
    qf = q * f
    qk = q * kk
    pad = jnp.zeros((SUBLANES - 3, LANES), F32)
    outs = []
    for h in range(HG_HEADS):
        hs = slice(h * LANES, (h + 1) * LANES)
        cols = jnp.concatenate([f[:, hs], kk[:, hs], qf[:, hs], pad], axis=0).T
        f_c, k_c, qf_c = cols[:, 0:1], cols[:, 1:2], cols[:, 2:3]
        s0 = s_ref[h]
        vh = v[:, hs]
        sn_ref[h] = f_c * s0 + k_c * vh
        o = jnp.sum(qf_c * s0, axis=0, keepdims=True) + jnp.sum(qk[:, hs], axis=1, keepdims=True) * vh
        gh = g[:, hs]
        outs.append(_rms(o, hgn_ref[...]) * (gh * jax.nn.sigmoid(gh)))
    o_ref[...] = jnp.concatenate(outs, axis=1).astype(BF16)


def _hgrn_decode(hg, state, hgrn_lb, hgrn_norm):
    Bd = hg.shape[0]
    o, sn = pl.pallas_call(
        _hgrn_decode_kernel, grid=(Bd,),
        in_specs=[pl.BlockSpec((None, 1, 4 * HG_WIDTH), lambda b: (b, 0, 0)),
                  pl.BlockSpec((None, HG_HEADS, HG_KEY, HG_VAL), lambda b: (b, 0, 0, 0)),
                  _small_spec(tuple(hgrn_lb.shape)), _small_spec((1, HG_VAL))],
        out_specs=[pl.BlockSpec((None, 1, HG_WIDTH), lambda b: (b, 0, 0)),
                   pl.BlockSpec((None, HG_HEADS, HG_KEY, HG_VAL), lambda b: (b, 0, 0, 0))],
        out_shape=[jax.ShapeDtypeStruct((Bd, 1, HG_WIDTH), BF16),
                   jax.ShapeDtypeStruct(state.shape, F32)],
        compiler_params=_params(("parallel",)), name="hgrn_decode",
    )(hg.reshape(Bd, 1, -1), state, hgrn_lb, hgrn_norm)
    return o.reshape(Bd, HG_WIDTH), sn


def _merge_kernel(oa_ref, oh_ref, wa_ref, wb_ref, ga_ref, gb_ref, o_ref):
    a = jnp.dot(oa_ref[...], wa_ref[...].astype(BF16), preferred_element_type=F32)
    b = jnp.dot(oh_ref[...], wb_ref[...].astype(BF16), preferred_element_type=F32)
    ga = jax.nn.sigmoid(ga_ref[...].astype(F32))
    gb = jax.nn.sigmoid(gb_ref[...].astype(F32))
    o_ref[...] = (ga * a + gb * b).astype(BF16)


def _merge(oa, oh, w_pa, w_pb, gates, tm, tn, tag):
    M = oa.shape[0]
    nj = D_MODEL // tn
    return pl.pallas_call(
        _merge_kernel, grid=(M // tm, nj),
        in_specs=[pl.BlockSpec((tm, ATT_WIDTH), lambda i, j: (i, 0)),
                  pl.BlockSpec((tm, HG_WIDTH), lambda i, j: (i, 0)),
                  pl.BlockSpec((None, ATT_WIDTH, tn), lambda i, j: (0, 0, j)),
                  pl.BlockSpec((None, HG_WIDTH, tn), lambda i, j: (0, 0, j)),
                  pl.BlockSpec((tm, tn), lambda i, j: (i, j)),
                  pl.BlockSpec((tm, tn), lambda i, j: (i, j + nj))],
        out_specs=pl.BlockSpec((tm, tn), lambda i, j: (i, j)),
        out_shape=jax.ShapeDtypeStruct((M, D_MODEL), BF16),
        compiler_params=_params(("parallel", "arbitrary")), name=f"merge_{tag}",
    )(oa, oh, w_pa, w_pb, gates, gates)


def _cross_prompt_kernel(x_ref, g_ref, wq_ref, mk_ref, mv_ref, wo_ref, o_ref):
    x = x_ref[...]
    u = _rms(x, g_ref[...]).astype(BF16)
    q = jnp.dot(u, wq_ref[...], preferred_element_type=F32).astype(BF16)
    heads = []
    for h in range(X_HEADS):
        hs = slice(h * X_HEAD_DIM, (h + 1) * X_HEAD_DIM)
        s = lax.dot_general(q[:, hs], mk_ref[:, hs], _NT, preferred_element_type=F32) * (X_HEAD_DIM ** -0.5)
        p = jnp.exp(s - jnp.max(s, axis=1, keepdims=True))
        p = p / jnp.sum(p, axis=1, keepdims=True)
        heads.append(jnp.dot(p.astype(BF16), mv_ref[:, hs], preferred_element_type=F32))
    o = jnp.concatenate(heads, axis=1).astype(BF16)
    o_ref[...] = x + jnp.dot(o, wo_ref[...], preferred_element_type=F32)


def _cross_prompt(h1, g_cross, w_cq_bf, mk_bf, mv_bf, w_co_bf, tm):
    M = h1.shape[0]
    return pl.pallas_call(
        _cross_prompt_kernel, grid=(M // tm,),
        in_specs=[pl.BlockSpec((tm, D_MODEL), lambda i: (i, 0)), _small_spec((1, D_MODEL)),
                  _small_spec((D_MODEL, X_WIDTH)), _small_spec(tuple(mk_bf.shape)),
                  _small_spec(tuple(mv_bf.shape)), _small_spec((X_WIDTH, D_MODEL))],
        out_specs=pl.BlockSpec((tm, D_MODEL), lambda i: (i, 0)),
        out_shape=jax.ShapeDtypeStruct((M, D_MODEL), F32),
        compiler_params=_params(("parallel",)), name="cross_prompt",
    )(h1, g_cross.reshape(1, -1), w_cq_bf, mk_bf, mv_bf, w_co_bf)


def _cross_decode_kernel(q_ref, mk_ref, mv_ref, o_ref):
    q = q_ref[...]
    outs = []
    for h in range(X_HEADS):
        hs = slice(h * X_HEAD_DIM, (h + 1) * X_HEAD_DIM)
        s = jnp.sum(mk_ref[:, hs] * q[:, hs], axis=1, keepdims=True) * (X_HEAD_DIM ** -0.5)
        p = jnp.exp(s - jnp.max(s, axis=0, keepdims=True))
        p = p / jnp.sum(p, axis=0, keepdims=True)
        outs.append(jnp.sum(p * mv_ref[:, hs], axis=0, keepdims=True))
    o_ref[...] = jnp.concatenate(outs, axis=1).astype(BF16)


def _cross_decode(q, mem_k, mem_v):
    Bd, mem = mem_k.shape[0], mem_k.shape[1]
    row = pl.BlockSpec((None, 1, X_WIDTH), lambda b: (b, 0, 0))
    mem_spec = pl.BlockSpec((None, mem, X_WIDTH), lambda b: (b, 0, 0))
    out = pl.pallas_call(
        _cross_decode_kernel, grid=(Bd,), in_specs=[row, mem_spec, mem_spec], out_specs=row,
        out_shape=jax.ShapeDtypeStruct((Bd, 1, X_WIDTH), BF16),
        compiler_params=_params(("parallel",)), name="cross_decode",
    )(q.reshape(Bd, 1, X_WIDTH), mem_k.reshape(Bd, mem, X_WIDTH), mem_v.reshape(Bd, mem, X_WIDTH))
    return out.reshape(Bd, X_WIDTH)


def _route_kernel(x_ref, g_ref, w_ref, b_ref, cin_ref, uin_ref, u_ref, eid_ref, wt_ref, rank_ref, cnt_ref,
                  w3_scr, c_scr, *, n_rows):
    i = pl.program_id(0)
    tm = x_ref.shape[0]

    @pl.when(i == 0)
    def _():
        for part, ws in zip(_split3(w_ref[...]), (0, 1, 2)):
            w3_scr[ws] = part
        c_scr[...] = cin_ref[...]

    u = _rms(x_ref[...], g_ref[...])
    u_ref[...] = u
    u_hi, u_mid, u_lo = _split3(u)
    dot = lambda a, b: jnp.dot(a, b, preferred_element_type=F32)
    logits = (dot(u_hi, w3_scr[0]) + dot(u_hi, w3_scr[1]) + dot(u_mid, w3_scr[0])
              + dot(u_hi, w3_scr[2]) + dot(u_lo, w3_scr[0]) + dot(u_mid, w3_scr[1])) + b_ref[...]

    lane = lax.broadcasted_iota(I32, (tm, LANES), 1)
    big = jnp.int32(1 << 20)
    first_max = lambda vals, vmax: jnp.min(jnp.where(vals == vmax, lane, big), axis=1, keepdims=True)
    glv = jnp.where(lane < N_GROUPS, logits, -jnp.inf)
    gmax = jnp.max(glv, axis=1, keepdims=True)
    gsel = first_max(glv, gmax)
    p_group = 1.0 / jnp.sum(jnp.exp(glv - gmax), axis=1, keepdims=True)
    e_lane = lane - N_GROUPS
    in_group = (e_lane >= 0) & (e_lane < N_EXPERTS) & (e_lane // EXPERTS_PER_GROUP == gsel)
    ev = jnp.where(in_group, logits, -jnp.inf)
    v1 = jnp.max(ev, axis=1, keepdims=True)
    i1 = first_max(ev, v1)
    ev2 = jnp.where(lane == i1, -jnp.inf, ev)
    v2 = jnp.max(ev2, axis=1, keepdims=True)
    i2 = first_max(ev2, v2)
    t2 = jnp.exp(v2 - v1)
    w1 = p_group / (1.0 + t2)
    w2 = w1 * t2
    e1 = i1 - N_GROUPS
    e2 = i2 - N_GROUPS
    eid_ref[...] = jnp.where(lane == 0, e1, jnp.where(lane == 1, e2, 0))
    wt_ref[...] = jnp.where(lane == 0, w1, jnp.where(lane == 1, w2, 0.0))

    valid = (lax.broadcasted_iota(I32, (tm, 1), 0) + i * tm) < n_rows
    oh1 = jnp.where((lane == e1) & valid, 1.0, 0.0)
    oh2 = jnp.where((lane == e2) & valid, 1.0, 0.0)
    cnt = (oh1 + oh2).astype(BF16)
    r_i = lax.broadcasted_iota(I32, (tm, tm), 0)
    c_i = lax.broadcasted_iota(I32, (tm, tm), 1)
    strict = jnp.where(c_i < r_i, 1.0, 0.0).astype(BF16)
    before = dot(strict, cnt) + c_scr[...]
    r1 = jnp.sum(oh1 * before, axis=1, keepdims=True)
    r2 = jnp.sum(oh2 * before, axis=1, keepdims=True)
    rank_ref[...] = jnp.where(lane == 0, r1, jnp.where(lane == 1, r2, 0.0)).astype(I32)
    c_scr[...] = c_scr[...] + jnp.sum(oh1 + oh2, axis=0, keepdims=True)
    cnt_ref[...] = c_scr[...]


def _route(x, g_ffn, w_router, b_router, counts_in, u_all, row_off, tm, tag):
    M = x.shape[0]
    T = u_all.shape[0]
    boff = row_off // tm
    kern = functools.partial(_route_kernel, n_rows=M)
    lane_out = lambda dt: jax.ShapeDtypeStruct((M, LANES), dt)
    outs = pl.pallas_call(
        kern, grid=(M // tm,),
        in_specs=[pl.BlockSpec((tm, D_MODEL), lambda i: (i, 0)), _small_spec((1, D_MODEL)),
                  _small_spec((D_MODEL, LANES)), _small_spec((1, LANES)), _small_spec((1, LANES)),
                  pl.BlockSpec(memory_space=pl.ANY)],
        out_specs=[pl.BlockSpec((tm, D_MODEL), lambda i: (i + boff, 0)),
                   pl.BlockSpec((tm, LANES), lambda i: (i, 0)), pl.BlockSpec((tm, LANES), lambda i: (i, 0)),
                   pl.BlockSpec((tm, LANES), lambda i: (i, 0)), _small_spec((1, LANES))],
        out_shape=[jax.ShapeDtypeStruct((T, D_MODEL), F32), lane_out(I32), lane_out(F32), lane_out(I32),
                   jax.ShapeDtypeStruct((1, LANES), F32)],
        scratch_shapes=[pltpu.VMEM((3, D_MODEL, LANES), BF16), pltpu.VMEM((1, LANES), F32)],
        input_output_aliases={5: 0},
        compiler_params=_params(("arbitrary",)), name=f"moe_route_{tag}",
    )(x, g_ffn.reshape(1, -1), w_router, b_router, counts_in, u_all)
    return outs


def _plan_kernel(cnt_ref, pstart_ref, blkexp_ref, nblk_ref, *, n_blocks):
    def per_expert(e, acc):
        pstart_ref[e] = acc
        nb = (cnt_ref[e] + MOE_ROWS - 1) // MOE_ROWS
        first = acc // MOE_ROWS

        def fill(b, c):
            blkexp_ref[b] = e
            return c

        lax.fori_loop(first, first + nb, fill, 0)
        return acc + nb * MOE_ROWS

    used = lax.fori_loop(0, N_EXPERTS, per_expert, jnp.int32(0)) // MOE_ROWS
    nblk_ref[0] = used
    last = blkexp_ref[jnp.maximum(used - 1, 0)]

    def tail(b, c):
        blkexp_ref[b] = last
        return c

    lax.fori_loop(used, n_blocks, tail, 0)


def _plan(counts, n_blocks):
    smem = pl.BlockSpec(memory_space=pltpu.SMEM)
    return pl.pallas_call(
        functools.partial(_plan_kernel, n_blocks=n_blocks),
        in_specs=[smem], out_specs=[smem, smem, smem],
        out_shape=[jax.ShapeDtypeStruct((N_EXPERTS,), I32), jax.ShapeDtypeStruct((n_blocks,), I32),
                   jax.ShapeDtypeStruct((1,), I32)],
        name="moe_plan",
    )(counts)


def _dispatch_kernel(eid_ref, rank_ref, pstart_ref, cnt_ref, nblk_ref, u_hbm, z_hbm, xs_hbm, sem_z, sem,
                     *, n_assign, n_blocks, batch_rows):
    R = MOE_ROWS

    def zero_copy(b):
        return pltpu.make_async_copy(z_hbm, xs_hbm.at[pl.ds(pl.multiple_of(b * R, R), R), :], sem_z)

    def partial_block(e):
        c = cnt_ref[e]
        return (c % R) != 0, (pstart_ref[e] + c) // R

    def zero_partial(start):
        def body(e, n):
            has, b = partial_block(e)

            @pl.when(has)
            def _():
                zero_copy(b).start() if start else zero_copy(b).wait()

            return n
        return body

    def zero_unused(start):
        def body(b, n):
            zero_copy(b).start() if start else zero_copy(b).wait()
            return n
        return body

    for start in (True, False):
        lax.fori_loop(0, N_EXPERTS, zero_partial(start), 0)
        lax.fori_loop(nblk_ref[0], n_blocks, zero_unused(start), 0)

    def batch(k, start):
        def body(r, n):
            a = k * batch_rows + r
            d = pstart_ref[eid_ref[a]] + rank_ref[a]
            cp = pltpu.make_async_copy(u_hbm.at[pl.ds(a // 2, 1), :], xs_hbm.at[pl.ds(d, 1), :], sem.at[k % 2])
            cp.start() if start else cp.wait()
            return n
        lax.fori_loop(0, batch_rows, body, 0, unroll=4)

    n_batches = n_assign // batch_rows

    def step(k, n):
        batch(k, True)

        @pl.when(k > 0)
        def _():
            batch(k - 1, False)

        return n

    lax.fori_loop(0, n_batches, step, 0)
    batch(n_batches - 1, False)


def _largest_divisor(n, limit):
    return max(d for d in range(1, limit + 1) if n % d == 0)


def _dispatch(eid, rank, pstart, counts, n_used, u_all, n_blocks):
    smem = pl.BlockSpec(memory_space=pltpu.SMEM)
    hbm = pl.BlockSpec(memory_space=pl.ANY)
    zeros = jnp.zeros((MOE_ROWS, D_MODEL), F32)
    A = eid.shape[0]
    return pl.pallas_call(
        functools.partial(_dispatch_kernel, n_assign=A, n_blocks=n_blocks,
                          batch_rows=_largest_divisor(A, 4 * MOE_ROWS)),
        in_specs=[smem] * 5 + [hbm, hbm], out_specs=hbm,
        out_shape=jax.ShapeDtypeStruct((n_blocks * MOE_ROWS, D_MODEL), F32),
        scratch_shapes=[pltpu.SemaphoreType.DMA(()), pltpu.SemaphoreType.DMA((2,))],
        name="moe_dispatch",
    )(eid, rank, pstart, counts, n_used, u_all, zeros)


def _expert_kernel(blkexp_ref, nblk_ref, x_ref, wg_ref, wu_ref, wd_ref, o_ref, wg_s, wu_s, wd_s):
    i = pl.program_id(0)

    @pl.when(i < nblk_ref[0])
    def _():
        prev = blkexp_ref[jnp.maximum(i - 1, 0)]

        @pl.when((i == 0) | (blkexp_ref[i] != prev))
        def _():
            wg_s[...] = wg_ref[...].astype(BF16)
            wu_s[...] = wu_ref[...].astype(BF16)
            wd_s[...] = wd_ref[...].astype(BF16)

        x = x_ref[...].astype(BF16)
        hg = jnp.dot(x, wg_s[...], preferred_element_type=F32)
        hu = jnp.dot(x, wu_s[...], preferred_element_type=F32)
        h = (hg * jax.nn.sigmoid(hg) * hu).astype(BF16)
        o_ref[...] = jnp.dot(h, wd_s[...], preferred_element_type=F32)

    @pl.when(i >= nblk_ref[0])
    def _():
        o_ref[...] = jnp.zeros(o_ref.shape, F32)


def _experts(xs, blk_exp, n_used, we_g, we_u, we_d, n_blocks):
    wspec = lambda k, n: pl.BlockSpec((None, None, k, n), lambda i, be, nb: (0, be[i], 0, 0))
    grid_spec = pltpu.PrefetchScalarGridSpec(
        num_scalar_prefetch=2, grid=(n_blocks,),
        in_specs=[pl.BlockSpec((MOE_ROWS, D_MODEL), lambda i, be, nb: (jnp.minimum(i, nb[0] - 1), 0)),
                  wspec(D_MODEL, EXPERT_FF), wspec(D_MODEL, EXPERT_FF), wspec(EXPERT_FF, D_MODEL)],
        out_specs=pl.BlockSpec((MOE_ROWS, D_MODEL), lambda i, be, nb: (i, 0)),
        scratch_shapes=[pltpu.VMEM((D_MODEL, EXPERT_FF), BF16), pltpu.VMEM((D_MODEL, EXPERT_FF), BF16),
                        pltpu.VMEM((EXPERT_FF, D_MODEL), BF16)])
    return pl.pallas_call(
        _expert_kernel, grid_spec=grid_spec,
        out_shape=jax.ShapeDtypeStruct((n_blocks * MOE_ROWS, D_MODEL), F32),
        compiler_params=_params(("arbitrary",)), name="moe_experts",
    )(blk_exp, n_used, xs, we_g, we_u, we_d)


def _combine_kernel(eid_ref, rank_ref, pstart_ref, h_ref, wt_ref, gf_ref, yb_hbm, o_ref, ybuf0, ybuf1, sem,
                    *, dest_off):
    i = pl.program_id(0)
    tm = h_ref.shape[0]
    base = dest_off + i * (2 * tm)

    def row_copy(r, k, buf):
        a = base + 2 * r + k
        d = pstart_ref[eid_ref[a]] + rank_ref[a]
        return pltpu.make_async_copy(yb_hbm.at[pl.ds(d, 1), :], buf.at[pl.ds(r, 1), :], sem)

    def start(r, c):
        row_copy(r, 0, ybuf0).start()
        row_copy(r, 1, ybuf1).start()
        return c

    def wait(r, c):
        row_copy(r, 0, ybuf0).wait()
        row_copy(r, 1, ybuf1).wait()
        return c

    lax.fori_loop(0, tm, start, 0, unroll=4)
    lax.fori_loop(0, tm, wait, 0, unroll=4)
    wt = wt_ref[...]
    h = h_ref[...] + (wt[:, 0:1] * ybuf0[...] + wt[:, 1:2] * ybuf1[...])
    o_ref[...] = _rms(h, gf_ref[...])


def _combine(h2, wts, eid, rank, pstart, yb, norm_final, dest_off, tm, tag):
    M = h2.shape[0]
    grid_spec = pltpu.PrefetchScalarGridSpec(
        num_scalar_prefetch=3, grid=(M // tm,),
        in_specs=[pl.BlockSpec((tm, D_MODEL), lambda i, *_: (i, 0)), pl.BlockSpec((tm, LANES), lambda i, *_: (i, 0)),
                  pl.BlockSpec((1, D_MODEL), lambda i, *_: (0, 0)), pl.BlockSpec(memory_space=pl.ANY)],
        out_specs=pl.BlockSpec((tm, D_MODEL), lambda i, *_: (i, 0)),
        scratch_shapes=[pltpu.VMEM((tm, D_MODEL), F32), pltpu.VMEM((tm, D_MODEL), F32),
                        pltpu.SemaphoreType.DMA(())])
    return pl.pallas_call(
        functools.partial(_combine_kernel, dest_off=dest_off), grid_spec=grid_spec,
        out_shape=jax.ShapeDtypeStruct((M, D_MODEL), F32),
        compiler_params=_params(("arbitrary",)), name=f"moe_combine_{tag}",
    )(eid, rank, pstart, h2, wts, norm_final.reshape(1, -1), yb)


def _moe_and_final_norm(h2_p, h2_s, g_ffn, wr_g, br_g, wr_e, br_e, we_g, we_u, we_d, norm_final, tm_p):
    Tp, Ts = h2_p.shape[0], h2_s.shape[0]
    T = Tp + Ts
    pad = LANES - N_GROUPS - N_EXPERTS
    w_router = jnp.concatenate([wr_g, wr_e, jnp.zeros((D_MODEL, pad), F32)], axis=1)
    b_router = jnp.concatenate([br_g, br_e, jnp.zeros((pad,), F32)]).reshape(1, LANES)
    u_all = jnp.zeros((T, D_MODEL), F32)
    zero_counts = jnp.zeros((1, LANES), F32)
    u_all, eid_p, wt_p, rank_p, counts = _route(h2_p, g_ffn, w_router, b_router, zero_counts, u_all, 0, tm_p, "p")
    u_all, eid_s, wt_s, rank_s, counts = _route(h2_s, g_ffn, w_router, b_router, counts, u_all, Tp, Ts, "s")
    eid = jnp.concatenate([eid_p[:, :2].reshape(-1), eid_s[:, :2].reshape(-1)])
    rank = jnp.concatenate([rank_p[:, :2].reshape(-1), rank_s[:, :2].reshape(-1)])
    A = 2 * T
    n_blocks = (A + N_EXPERTS * (MOE_ROWS - 1)) // MOE_ROWS + 1
    cnt = counts[0, :N_EXPERTS].astype(I32)
    pstart, blk_exp, n_used = _plan(cnt, n_blocks)
    xs = _dispatch(eid, rank, pstart, cnt, n_used, u_all, n_blocks)
    yb = _experts(xs, blk_exp, n_used, we_g, we_u, we_d, n_blocks)
    y_p = _combine(h2_p, wt_p, eid, rank, pstart, yb, norm_final, 0, 256, "p")
    y_s = _combine(h2_s, wt_s, eid, rank, pstart, yb, norm_final, 2 * Tp, Ts, "s")
    return y_p, y_s


def kernel(x_prompt, x_sample, mem_prompt, cache_k, cache_v, cache_mem_k, cache_mem_v, state_hgrn, page_table,
           norm_mix, w_in, lambda_q1, lambda_k1, lambda_q2, lambda_k2, subln, hgrn_lb, hgrn_norm, w_pa, w_pb,
           w_out, norm_cross, w_cq, w_ck, w_cv, w_co, norm_ffn, w_router_group, b_router_group,
           w_router_expert, b_router_expert, w_e_gate, w_e_up, w_e_down, norm_final):
    assert w_in.shape[0] == 1, "single-layer step"
    Bp, S, D = x_prompt.shape
    Bd, Ld, _ = x_sample.shape
    assert Bp == 1 and Ld == 1
    n_pages = page_table.shape[1]
    page = cache_k.shape[2]
    past_len = n_pages * page
    xp = x_prompt.reshape(S, D)
    xs = x_sample.reshape(Bd, D)
    lam = (lambda_q1, lambda_k1, lambda_q2, lambda_k2)
    TM = 1024
    w_in, w_pa, w_pb, w_out = (w.astype(BF16) for w in (w_in, w_pa, w_pb, w_out))

    q, k, k_bf, v, v_bf, hg, gates = _in_projection(xp, norm_mix[0], w_in, jnp.arange(S, dtype=I32), TM, 512, "p")
    oa = _attention_prompt(q, k_bf, v_bf, *lam, subln, 512)
    oh, st_p = _hgrn_prompt(hg, hgrn_lb, hgrn_norm, 256)
    merged = _merge(oa, oh, w_pa, w_pb, gates, TM, 512, "p")
    (h1_p,) = _mm(merged, w_out, col_off=0, n_cols=D, tm=TM, tn=512, epilogue=_epi_residual, out_dtypes=[F32],
                  extras=[(xp, pl.BlockSpec((TM, 512), lambda i, j: (i, j)))], name="outproj_p")

    pos_s = jnp.full((Bd,), past_len, I32)
    qs, ks, ks_bf, vs, vs_bf, hgs, gates_s = _in_projection(xs, norm_mix[0], w_in, pos_s, Bd, 512, "s")
    ck = jnp.transpose(cache_k[0], (0, 2, 3, 4, 1)).reshape(cache_k.shape[1], ATT_WIDTH, page)
    cv = cache_v[0].reshape(cache_v.shape[1], page * ATT_HEADS, 2 * ATT_HEAD_DIM)
    oa_s = _attention_decode(qs, ks_bf, vs_bf, ck, cv, page_table, *lam, subln)
    oh_s, st_s = _hgrn_decode(hgs, state_hgrn[0], hgrn_lb, hgrn_norm)
    merged_s = _merge(oa_s, oh_s, w_pa, w_pb, gates_s, Bd, 512, "s")
    (h1_s,) = _mm(merged_s, w_out, col_off=0, n_cols=D, tm=Bd, tn=512, epilogue=_epi_residual, out_dtypes=[F32],
                  extras=[(xs, pl.BlockSpec((Bd, 512), lambda i, j: (i, j)))], name="outproj_s")

    mem = mem_prompt.reshape(-1, D)
    mk, mk_bf = _mm(mem, w_ck, col_off=0, n_cols=X_WIDTH, tm=mem.shape[0], tn=X_WIDTH, epilogue=_epi_v,
                    out_dtypes=[F32, BF16], name="mem_k")
    mv, mv_bf = _mm(mem, w_cv, col_off=0, n_cols=X_WIDTH, tm=mem.shape[0], tn=X_WIDTH, epilogue=_epi_v,
                    out_dtypes=[F32, BF16], name="mem_v")
    w_cq_bf = w_cq[0].astype(BF16)
    w_co_bf = w_co[0].astype(BF16)
    h2_p = _cross_prompt(h1_p, norm_cross[0], w_cq_bf, mk_bf, mv_bf, w_co_bf, 256)
    (qc_s,) = _mm(h1_s, w_cq, col_off=0, n_cols=X_WIDTH, tm=Bd, tn=X_WIDTH, epilogue=_epi_plain,
                  out_dtypes=[F32], norm_g=norm_cross[0], name="cross_q_s")
    oc_s = _cross_decode(qc_s, cache_mem_k[0], cache_mem_v[0])
    (h2_s,) = _mm(oc_s, w_co, col_off=0, n_cols=D, tm=Bd, tn=512, epilogue=_epi_residual, out_dtypes=[F32],
                  extras=[(h1_s, pl.BlockSpec((Bd, 512), lambda i, j: (i, j)))], name="cross_o_s")

    y_p, y_s = _moe_and_final_norm(h2_p, h2_s, norm_ffn[0], w_router_group[0], b_router_group[0],
                                   w_router_expert[0], b_router_expert[0], w_e_gate, w_e_up, w_e_down,
                                   norm_final, 256)

    return (y_p.reshape(Bp, S, D), y_s.reshape(Bd, Ld, D),
            k.reshape(1, Bp, S, ATT_HEADS, 2, ATT_HEAD_DIM), v.reshape(1, Bp, S, ATT_HEADS, 2 * ATT_HEAD_DIM),
            ks.reshape(1, Bd, Ld, ATT_HEADS, 2, ATT_HEAD_DIM), vs.reshape(1, Bd, Ld, ATT_HEADS, 2 * ATT_HEAD_DIM),
            st_p.reshape(1, Bp, HG_HEADS, HG_KEY, HG_VAL), st_s.reshape(1, Bd, HG_HEADS, HG_KEY, HG_VAL),
            mk.reshape(1, Bp, -1, X_HEADS, X_HEAD_DIM), mv.reshape(1, Bp, -1, X_HEADS, X_HEAD_DIM))
```

```python
import functools
import math

import numpy as np
import jax
import jax.numpy as jnp
from jax import lax
from jax.experimental import pallas as pl
from jax.experimental.pallas import tpu as pltpu

F32 = jnp.float32
BF16 = jnp.bfloat16
I32 = jnp.int32

D_MODEL = 2048
ATT_HEADS = 8
ATT_HEAD_DIM = 64
ATT_WIDTH = ATT_HEADS * 2 * ATT_HEAD_DIM
ROPE_THETA = 10000.0
HG_HEADS = 8
HG_KEY = 128
HG_VAL = 128
HG_WIDTH = HG_HEADS * HG_VAL
X_HEADS = 4
X_HEAD_DIM = 128
X_WIDTH = X_HEADS * X_HEAD_DIM
N_GROUPS = 4
EXPERTS_PER_GROUP = 8
N_EXPERTS = N_GROUPS * EXPERTS_PER_GROUP
EXPERT_FF = 512
RMS_EPS = 1e-6
NEG_INF = -1e30
LAMBDA_INIT = 0.8 - 0.6 * math.exp(-0.3 * 0)

LANES = 128
SUBLANES = 8
VMEM_LIMIT = 52 * 1024 * 1024

HG_CHUNK = 64
HG_LEVELS = (16, 32, 64)
MOE_ROWS = 128
PAGES_PER_STEP = 8
ATTN_TQ = 512
ATTN_TK = 512

_NT = (((1,), (1,)), ((), ()))
_TN = (((0,), (0,)), ((), ()))


def _params(sem):
    return pltpu.CompilerParams(dimension_semantics=sem, vmem_limit_bytes=VMEM_LIMIT)


def _rms(x, g):
    return x * lax.rsqrt(jnp.mean(x * x, axis=-1, keepdims=True) + RMS_EPS) * g


def _mm_kernel(*refs, n_extra, n_out, norm, epilogue):
    x_ref = refs[0]
    pos = 1
    if norm:
        g_ref = refs[1]
        pos = 2
    w_ref = refs[pos]
    extras = refs[pos + 1:pos + 1 + n_extra]
    outs = refs[pos + 1 + n_extra:pos + 1 + n_extra + n_out]
    if norm:
        u_ref = refs[-1]

        @pl.when(pl.program_id(1) == 0)
        def _():
            u_ref[...] = _rms(x_ref[...].astype(F32), g_ref[...]).astype(BF16)

        u = u_ref[...]
    else:
        u = x_ref[...].astype(BF16)
    acc = jnp.dot(u, w_ref[...].astype(BF16), preferred_element_type=F32)
    epilogue(acc, extras, outs)


def _mm(x, w, *, col_off, n_cols, tm, tn, epilogue, out_dtypes, norm_g=None, extras=(), name):
    M, K = x.shape
    assert M % tm == 0 and n_cols % tn == 0 and col_off % tn == 0
    joff = col_off // tn
    if w.ndim == 3:
        w_spec = pl.BlockSpec((None, K, tn), lambda i, j: (0, 0, j + joff))
    else:
        w_spec = pl.BlockSpec((K, tn), lambda i, j: (0, j + joff))
    in_specs = [pl.BlockSpec((tm, K), lambda i, j: (i, 0))]
    args = [x]
    if norm_g is not None:
        in_specs.append(pl.BlockSpec((1, K), lambda i, j: (0, 0)))
        args.append(norm_g.reshape(1, K))
    in_specs.append(w_spec)
    args.append(w)
    for arr, spec in extras:
        in_specs.append(spec)
        args.append(arr)
    out_specs = [pl.BlockSpec((tm, tn), lambda i, j: (i, j)) for _ in out_dtypes]
    out_shape = [jax.ShapeDtypeStruct((M, n_cols), dt) for dt in out_dtypes]
    scratch = [pltpu.VMEM((tm, K), BF16)] if norm_g is not None else []
    kern = functools.partial(_mm_kernel, n_extra=len(extras), n_out=len(out_dtypes),
                             norm=norm_g is not None, epilogue=epilogue)
    res = pl.pallas_call(
        kern, grid=(M // tm, n_cols // tn), in_specs=in_specs, out_specs=out_specs,
        out_shape=out_shape, scratch_shapes=scratch,
        compiler_params=_params(("parallel", "arbitrary")), name=name)(*args)
    return res


def _rope_tile(x, cos, sin_signed):
    first = (lax.broadcasted_iota(I32, (x.shape[0], LANES), 1) % ATT_HEAD_DIM) < ATT_HEAD_DIM // 2
    outs = []
    for c in range(x.shape[1] // LANES):
        xc = x[:, c * LANES:(c + 1) * LANES]
        rot = jnp.where(first, pltpu.roll(xc, LANES - ATT_HEAD_DIM // 2, 1),
                        pltpu.roll(xc, ATT_HEAD_DIM // 2, 1))
        outs.append(xc * cos + rot * sin_signed)
    return outs[0] if len(outs) == 1 else jnp.concatenate(outs, axis=1)


def _epi_q(acc, extras, outs):
    cos_ref, sin_ref = extras
    outs[0][...] = (_rope_tile(acc, cos_ref[...], sin_ref[...]) * (ATT_HEAD_DIM ** -0.5)).astype(BF16)


def _epi_k(acc, extras, outs):
    cos_ref, sin_ref = extras
    r = _rope_tile(acc, cos_ref[...], sin_ref[...])
    outs[0][...] = r
    outs[1][...] = r.astype(BF16)


def _epi_v(acc, extras, outs):
    outs[0][...] = acc
    outs[1][...] = acc.astype(BF16)


def _epi_plain(acc, extras, outs):
    outs[0][...] = acc.astype(outs[0].dtype)


def _epi_residual(acc, extras, outs):
    outs[0][...] = extras[0][...] + acc


def _rope_tables(pos):
    half = ATT_HEAD_DIM // 2
    freqs = ROPE_THETA ** (-jnp.arange(half, dtype=F32) / half)
    ang = pos.astype(F32)[:, None] * freqs[None, :]
    c, s = jnp.cos(ang), jnp.sin(ang)
    return jnp.tile(c, (1, 4)), jnp.concatenate([-s, s, -s, s], axis=1)


def _inproj_kernel(x_ref, g_ref, w_ref, cos_ref, sin_ref, q_ref, k_ref, kb_ref, v_ref, vb_ref, hg_ref, gt_ref,
                   u_ref, *, bounds):
    j = pl.program_id(1)

    @pl.when(j == 0)
    def _():
        u_ref[...] = _rms(x_ref[...].astype(F32), g_ref[...]).astype(BF16)

    acc = jnp.dot(u_ref[...], w_ref[...], preferred_element_type=F32)
    in_seg = lambda s: (j >= bounds[s]) & (j < bounds[s + 1])

    @pl.when(in_seg(0))
    def _():
        _epi_q(acc, (cos_ref, sin_ref), (q_ref,))

    @pl.when(in_seg(1))
    def _():
        _epi_k(acc, (cos_ref, sin_ref), (k_ref, kb_ref))

    @pl.when(in_seg(2))
    def _():
        _epi_v(acc, (), (v_ref, vb_ref))

    @pl.when(in_seg(3))
    def _():
        hg_ref[...] = acc

    @pl.when(in_seg(4))
    def _():
        gt_ref[...] = acc.astype(gt_ref.dtype)


def _in_projection(x, g, w_in_bf, pos, tm, tn, tag, gate_dtype):
    M, K = x.shape
    cos, sin = _rope_tables(pos)
    widths = (ATT_WIDTH, ATT_WIDTH, ATT_WIDTH, 4 * HG_WIDTH, 2 * D_MODEL)
    bounds = tuple(int(b) for b in np.cumsum((0,) + widths) // tn)

    def seg_spec(s):
        lo, n = bounds[s], bounds[s + 1] - bounds[s]
        return pl.BlockSpec((tm, tn), lambda i, j: (i, jnp.clip(j - lo, 0, n - 1)))

    seg_of_out = (0, 1, 1, 2, 2, 3, 4)
    out_dtypes = (BF16, F32, BF16, F32, BF16, F32, gate_dtype)
    row_tbl = pl.BlockSpec((tm, LANES), lambda i, j: (i, 0))
    return pl.pallas_call(
        functools.partial(_inproj_kernel, bounds=bounds),
        grid=(M // tm, bounds[-1]),
        in_specs=[pl.BlockSpec((tm, K), lambda i, j: (i, 0), pipeline_mode=pl.Buffered(1)),
                  pl.BlockSpec((1, K), lambda i, j: (0, 0)),
                  pl.BlockSpec((None, K, tn), lambda i, j: (0, 0, j)), row_tbl, row_tbl],
        out_specs=[seg_spec(s) for s in seg_of_out],
        out_shape=[jax.ShapeDtypeStruct((M, widths[s]), dt) for s, dt in zip(seg_of_out, out_dtypes)],
        scratch_shapes=[pltpu.VMEM((tm, K), BF16)],
        compiler_params=_params(("parallel", "arbitrary")), name=f"inproj_{tag}",
    )(x, g.reshape(1, K), w_in_bf, cos, sin)


def _lambda_value(lq1, lk1, lq2, lk2):
    return (jnp.exp(jnp.sum(lq1[...] * lk1[...], axis=-1, keepdims=True))
            - jnp.exp(jnp.sum(lq2[...] * lk2[...], axis=-1, keepdims=True)) + LAMBDA_INIT)


def _attn_kernel(q_ref, k_ref, v_ref, lq1, lk1, lq2, lk2, subln_ref, o_ref, *, tq, tk):
    i = pl.program_id(1)
    q = q_ref[...]
    lane = lax.broadcasted_iota(I32, q.shape, 1)
    zero = jnp.zeros_like(q)
    qm = (jnp.where(lane < ATT_HEAD_DIM, q, zero), jnp.where(lane >= ATT_HEAD_DIM, q, zero))

    def update(off, carry, mask):
        off = pl.multiple_of(off, tk)
        kt = k_ref[pl.ds(off, tk), :]
        vt = v_ref[pl.ds(off, tk), :]
        new = []
        for m in range(2):
            mx, l, acc = carry[3 * m:3 * m + 3]
            s = lax.dot_general(qm[m], kt, _NT, preferred_element_type=F32)
            if mask is not None:
                s = jnp.where(mask, s, NEG_INF)
            mn = jnp.maximum(mx, jnp.max(s, axis=1, keepdims=True))
            alpha = jnp.exp(mx - mn)
            p = jnp.exp(s - mn)
            l = alpha * l + jnp.sum(p, axis=1, keepdims=True)
            acc = alpha * acc + jnp.dot(p.astype(BF16), vt, preferred_element_type=F32)
            new += [mn, l, acc]
        return tuple(new)

    def body(j, carry):
        return update(j * tk, carry, None)

    init = (jnp.full((tq, 1), NEG_INF, F32), jnp.zeros((tq, 1), F32), jnp.zeros((tq, LANES), F32)) * 2
    carry = lax.fori_loop(0, i * (tq // tk), body, init)
    row = lax.broadcasted_iota(I32, (tq, tk), 0)
    col = lax.broadcasted_iota(I32, (tq, tk), 1)
    for d in range(tq // tk):
        carry = update(i * tq + d * tk, carry, col + d * tk <= row)
    _, l0, a0, _, l1, a1 = carry

    lam = _lambda_value(lq1, lk1, lq2, lk2)
    o = a0 / l0 - lam * (a1 / l1)
    o_ref[...] = (_rms(o, subln_ref[...]) * (1.0 - LAMBDA_INIT)).astype(BF16)


def _small_spec(shape):
    nd = len(shape)
    return pl.BlockSpec(shape, lambda *_: (0,) * nd)


def _attention_prompt(q, k_bf, v_bf, lq1, lk1, lq2, lk2, subln, tq, tk):
    S = q.shape[0]
    assert tq % tk == 0
    lam_specs = [_small_spec((1, ATT_HEAD_DIM))] * 4
    return pl.pallas_call(
        functools.partial(_attn_kernel, tq=tq, tk=tk),
        grid=(ATT_HEADS, S // tq),
        in_specs=[pl.BlockSpec((tq, LANES), lambda h, i: (i, h)),
                  pl.BlockSpec((S, LANES), lambda h, i: (0, h)),
                  pl.BlockSpec((S, LANES), lambda h, i: (0, h))] + lam_specs
                 + [_small_spec((1, LANES))],
        out_specs=pl.BlockSpec((tq, LANES), lambda h, i: (i, h)),
        out_shape=jax.ShapeDtypeStruct((S, ATT_WIDTH), BF16),
        compiler_params=_params(("parallel", "arbitrary")), name="attn_prompt",
    )(q, k_bf, v_bf, lq1, lk1, lq2, lk2, subln)


def _decode_attn_kernel(pt_ref, q_ref, kn_ref, vn_ref, lq1, lk1, lq2, lk2, subln_ref, *rest, n_pg):
    k_refs = rest[:n_pg]
    v_refs = rest[n_pg:2 * n_pg]
    o_ref = rest[2 * n_pg]
    qr_ref, m_ref, l_ref, acc_ref = rest[2 * n_pg + 1:]
    j = pl.program_id(1)
    nrow = 2 * ATT_HEADS
    row = lax.broadcasted_iota(I32, (nrow, ATT_WIDTH), 0)
    lane = lax.broadcasted_iota(I32, (nrow, ATT_WIDTH), 1)
    lane_head = lane // LANES
    lane_map = (lane // ATT_HEAD_DIM) % 2

    @pl.when(j == 0)
    def _():
        sel = (lane_head == row % ATT_HEADS) & (lane_map == row // ATT_HEADS)
        qb = jnp.broadcast_to(q_ref[...].astype(F32), (nrow, ATT_WIDTH))
        qr_ref[...] = jnp.where(sel, qb, 0.0).astype(BF16)
        m_ref[...] = jnp.full(m_ref.shape, NEG_INF, F32)
        l_ref[...] = jnp.zeros(l_ref.shape, F32)
        acc_ref[...] = jnp.zeros(acc_ref.shape, F32)

    qr = qr_ref[...]
    s = jnp.concatenate(
        [jnp.dot(qr, kr[...].astype(BF16), preferred_element_type=F32) for kr in k_refs],
        axis=1)
    mx = m_ref[...]
    mn = jnp.maximum(mx, jnp.max(s, axis=1, keepdims=True))
    alpha = jnp.exp(mx - mn)
    p = jnp.exp(s - mn)
    l_ref[...] = alpha * l_ref[...] + jnp.sum(p, axis=1, keepdims=True)
    pb = p.astype(BF16)
    page = s.shape[1] // n_pg
    heads = []
    for h in range(ATT_HEADS):
        pv = jnp.zeros((nrow, LANES), F32)
        for g, vr in enumerate(v_refs):
            vh = vr[pl.ds(h, page, stride=ATT_HEADS), :]
            pv = pv + jnp.dot(pb[:, g * page:(g + 1) * page], vh.astype(BF16), preferred_element_type=F32)
        heads.append(pv)
    acc_ref[...] = alpha * acc_ref[...] + jnp.concatenate(heads, axis=1)
    m_ref[...] = mn

    @pl.when(j == pl.num_programs(1) - 1)
    def _():
        kn = kn_ref[...].astype(F32)
        vn = vn_ref[...].astype(F32)
        sn = jnp.sum(qr.astype(F32) * kn, axis=1, keepdims=True)
        mx2 = m_ref[...]
        mn2 = jnp.maximum(mx2, sn)
        a2 = jnp.exp(mx2 - mn2)
        pn = jnp.exp(sn - mn2)
        l = a2 * l_ref[...] + pn
        acc = a2 * acc_ref[...] + pn * vn
        on = acc / l
        lam = _lambda_value(lq1, lk1, lq2, lk2)
        od = on[:ATT_HEADS] - lam * on[ATT_HEADS:]
        own = (lax.broadcasted_iota(I32, od.shape, 1) // LANES) == lax.broadcasted_iota(I32, od.shape, 0)
        od = jnp.where(own, od, 0.0)
        ms = jnp.sum(od * od, axis=1, keepdims=True) / (2 * ATT_HEAD_DIM)
        y = od * lax.rsqrt(ms + RMS_EPS) * subln_ref[...] * (1.0 - LAMBDA_INIT)
        o_ref[...] = jnp.sum(y, axis=0, keepdims=True).astype(BF16)


def _attention_decode(q, k_new_bf, v_new_bf, cache_kt, cache_v, page_table, lq1, lk1, lq2, lk2, subln):
    Bd = q.shape[0]
    page = cache_kt.shape[2]
    n_pages = page_table.shape[1]
    n_pg = PAGES_PER_STEP
    assert n_pages % n_pg == 0
    subln_w = jnp.tile(subln, (1, ATT_HEADS))

    def k_spec(g):
        return pl.BlockSpec((None, ATT_WIDTH, page), lambda b, j, pt: (pt[b * n_pages + j * n_pg + g], 0, 0))

    def v_spec(g):
        return pl.BlockSpec((None, page * ATT_HEADS, 2 * ATT_HEAD_DIM),
                            lambda b, j, pt: (pt[b * n_pages + j * n_pg + g], 0, 0))

    row_spec = pl.BlockSpec((None, 1, ATT_WIDTH), lambda b, j, pt: (b, 0, 0))
    small = lambda shape: pl.BlockSpec(shape, lambda b, j, pt: (0,) * len(shape))
    grid_spec = pltpu.PrefetchScalarGridSpec(
        num_scalar_prefetch=1, grid=(Bd, n_pages // n_pg),
        in_specs=[row_spec, row_spec, row_spec] + [small((1, ATT_HEAD_DIM))] * 4 + [small((1, ATT_WIDTH))]
                 + [k_spec(g) for g in range(n_pg)] + [v_spec(g) for g in range(n_pg)],
        out_specs=row_spec,
        scratch_shapes=[pltpu.VMEM((2 * ATT_HEADS, ATT_WIDTH), BF16),
                        pltpu.VMEM((2 * ATT_HEADS, 1), F32), pltpu.VMEM((2 * ATT_HEADS, 1), F32),
                        pltpu.VMEM((2 * ATT_HEADS, ATT_WIDTH), F32)])
    out = pl.pallas_call(
        functools.partial(_decode_attn_kernel, n_pg=n_pg), grid_spec=grid_spec,
        out_shape=jax.ShapeDtypeStruct((Bd, 1, ATT_WIDTH), BF16),
        compiler_params=_params(("parallel", "arbitrary")), name="attn_decode",
    )(page_table.reshape(-1), q.reshape(Bd, 1, -1), k_new_bf.reshape(Bd, 1, -1), v_new_bf.reshape(Bd, 1, -1),
      lq1, lk1, lq2, lk2, subln_w, *([cache_kt] * n_pg), *([cache_v] * n_pg))
    return out.reshape(Bd, ATT_WIDTH)


def _hgrn_masks():
    C = HG_CHUNK
    t = np.arange(C)[:, None]
    r = np.arange(C)[None, :]
    blocks = [(r <= t), (r > t)]
    for B in HG_LEVELS:
        mid = (t // B) * B + B // 2 - 1
        second = (t % B) >= B // 2
        blocks.append(np.where(second, (r > mid) & (r <= t), (r > t) & (r <= mid)))
    return jnp.asarray(np.concatenate(blocks, axis=0).astype(np.float32), dtype=BF16)


def _lower_bound(lb_ref):
    a = lb_ref[...].astype(F32)
    e = jnp.exp(a - jnp.max(a, axis=0, keepdims=True))
    return e[0:1] / jnp.sum(e, axis=0, keepdims=True)


def _group_rows(x, j):
    return jnp.broadcast_to(x[:, j:j + 1, :], x.shape)


def _split3(x):
    hi = x.astype(BF16)
    r1 = x - hi.astype(F32)
    mid = r1.astype(BF16)
    lo = (r1 - mid.astype(F32)).astype(BF16)
    return hi, mid, lo


def _hgrn_tile(q, z, v, lb, msk, s_t):
    C = HG_CHUNK
    n = q.shape[0] // C
    dot = lambda a, b: jnp.dot(a, b, preferred_element_type=F32)
    rows = lambda x, c: x[c * C:(c + 1) * C]
    chunks = range(n)

    logf = jnp.log(lb + (1.0 - lb) * jax.nn.sigmoid(z))
    kk = (1.0 - lb) * jax.nn.sigmoid(-z)
    v_bf = v.astype(BF16)
    parts = _split3(logf)
    e = [sum(dot(msk, rows(p, c)) for p in parts) for c in chunks]
    b = [ec[0:C] for ec in e]
    qi = [(rows(q, c) * jnp.exp(b[c])).astype(BF16) for c in chunks]
    kl = [(rows(kk, c) * jnp.exp(e[c][C:2 * C])).astype(BF16) for c in chunks]
    kv = [lax.dot_general(rows(v_bf, c), kl[c], _TN, preferred_element_type=F32) for c in chunks]

    t_idx = lax.broadcasted_iota(I32, (C, 1), 0)
    row = lax.broadcasted_iota(I32, (C, C), 0)
    col = lax.broadcasted_iota(I32, (C, C), 1)
    a = [jnp.zeros((C, C), F32) for _ in chunks]
    for li, B in enumerate(HG_LEVELS):
        second = (t_idx % B) >= B // 2
        same_block = (row // B) == (col // B)
        for c in chunks:
            x = jnp.exp(e[c][(2 + li) * C:(3 + li) * C])
            qt = jnp.where(second, rows(q, c) * x, 0.0).astype(BF16)
            kt = jnp.where(second, 0.0, rows(kk, c) * x).astype(BF16)
            al = lax.dot_general(qt, kt, _NT, preferred_element_type=F32)
            a[c] = a[c] + (jnp.where(same_block, al, 0.0) if B < C else al)

    states = [s_t]
    for c in chunks:
        states.append(states[-1] * jnp.exp(b[c][C - 1:C, :]) + kv[c])
    o = [lax.dot_general(qi[c], states[c].astype(BF16), _NT, preferred_element_type=F32)
         + dot(a[c].astype(BF16), rows(v_bf, c)) for c in chunks]

    G = n * C // SUBLANES
    b_all = jnp.concatenate(b, axis=0) if n > 1 else b[0]
    q3, k3, v3, b3 = (x.reshape(G, SUBLANES, LANES) for x in (q, kk, v, b_all))
    p3 = lax.broadcasted_iota(I32, (G, SUBLANES, 1), 1)
    o3 = (jnp.concatenate(o, axis=0) if n > 1 else o[0]).reshape(G, SUBLANES, LANES)
    for j in range(SUBLANES):
        term = q3 * _group_rows(k3, j) * jnp.exp(jnp.minimum(b3 - _group_rows(b3, j), 0.0))
        aj = jnp.where(p3 >= j, jnp.sum(term, axis=-1, keepdims=True), 0.0)
        o3 = o3 + aj * _group_rows(v3, j)
    return o3.reshape(n * C, LANES), states[-1]


def _hgrn_kernel(q_ref, z_ref, v_ref, g_ref, lb_ref, hgn_ref, msk_ref, o_ref, st_ref, s_scr):
    i = pl.program_id(1)

    @pl.when(i == 0)
    def _():
        s_scr[...] = jnp.zeros(s_scr.shape, F32)

    o, s_t = _hgrn_tile(q_ref[...], z_ref[...], v_ref[...], _lower_bound(lb_ref), msk_ref[...], s_scr[...])
    s_scr[...] = s_t
    g = g_ref[...]
    o_ref[...] = (_rms(o, hgn_ref[...]) * (g * jax.nn.sigmoid(g))).astype(BF16)

    @pl.when(i == pl.num_programs(1) - 1)
    def _():
        st_ref[...] = s_scr[...].T


def _hgrn_prompt(hg, hgrn_lb, hgrn_norm, tt):
    S = hg.shape[0]
    msk = _hgrn_masks()
    col = lambda seg: pl.BlockSpec((tt, LANES), lambda h, i, seg=seg: (i, seg * HG_HEADS + h))
    o, st = pl.pallas_call(
        _hgrn_kernel, grid=(HG_HEADS, S // tt),
        in_specs=[col(0), col(1), col(2), col(3),
                  pl.BlockSpec((hgrn_lb.shape[0], LANES), lambda h, i: (0, h)),
                  _small_spec((1, HG_VAL)), _small_spec(tuple(msk.shape))],
        out_specs=[pl.BlockSpec((tt, LANES), lambda h, i: (i, h)),
                   pl.BlockSpec((None, HG_KEY, HG_VAL), lambda h, i: (h, 0, 0))],
        out_shape=[jax.ShapeDtypeStruct((S, HG_WIDTH), BF16),
                   jax.ShapeDtypeStruct((HG_HEADS, HG_KEY, HG_VAL), F32)],
        scratch_shapes=[pltpu.VMEM((HG_VAL, HG_KEY), F32)],
        compiler_params=_params(("parallel", "arbitrary")), name="hgrn_prompt",
    )(hg, hg, hg, hg, hgrn_lb, hgrn_norm, msk)
    return o, st


def _hgrn_decode_kernel(hg_ref, s_ref, lb_ref, hgn_ref, o_ref, sn_ref):
    W = HG_WIDTH
    row = hg_ref[...]
    q, z, v, g = (row[:, s * W:(s + 1) * W] for s in range(4))
    lb = _lower_bound(lb_ref)
    f = lb + (1.0 - lb) * jax.nn.sigmoid(z)
    kk = (1.0 - lb) * jax.nn.sigmoid(-z)
    rnd = lambda x: x.astype(BF16).astype(F32)
    qf = rnd(q * f)
    qk = rnd(q) * rnd(kk)
    pad = jnp.zeros((SUBLANES - 3, LANES), F32)
    outs = []
    for h in range(HG_HEADS):
        hs = slice(h * LANES, (h + 1) * LANES)
        cols = jnp.concatenate([f[:, hs], kk[:, hs], qf[:, hs], pad], axis=0).T
        f_c, k_c, qf_c = cols[:, 0:1], cols[:, 1:2], cols[:, 2:3]
        s0 = s_ref[h]
        vh = v[:, hs]
        sn_ref[h] = f_c * s0 + k_c * vh
        o = (jnp.sum(qf_c * rnd(s0), axis=0, keepdims=True)
             + rnd(jnp.sum(qk[:, hs], axis=1, keepdims=True)) * rnd(vh))
        gh = g[:, hs]
        outs.append(_rms(o, hgn_ref[...]) * (gh * jax.nn.sigmoid(gh)))
    o_ref[...] = jnp.concatenate(outs, axis=1).astype(BF16)


def _hgrn_decode(hg, state, hgrn_lb, hgrn_norm):
    Bd = hg.shape[0]
    o, sn = pl.pallas_call(
        _hgrn_decode_kernel, grid=(Bd,),
        in_specs=[pl.BlockSpec((None, 1, 4 * HG_WIDTH), lambda b: (b, 0, 0)),
                  pl.BlockSpec((None, HG_HEADS, HG_KEY, HG_VAL), lambda b: (b, 0, 0, 0)),
                  _small_spec(tuple(hgrn_lb.shape)), _small_spec((1, HG_VAL))],
        out_specs=[pl.BlockSpec((None, 1, HG_WIDTH), lambda b: (b, 0, 0)),
                   pl.BlockSpec((None, HG_HEADS, HG_KEY, HG_VAL), lambda b: (b, 0, 0, 0))],
        out_shape=[jax.ShapeDtypeStruct((Bd, 1, HG_WIDTH), BF16),
                   jax.ShapeDtypeStruct(state.shape, F32)],
        compiler_params=_params(("parallel",)), name="hgrn_decode",
    )(hg.reshape(Bd, 1, -1), state, hgrn_lb, hgrn_norm)
    return o.reshape(Bd, HG_WIDTH), sn


def _merge_kernel(oa_ref, oh_ref, wa_ref, wb_ref, ga_ref, gb_ref, o_ref):
    a = jnp.dot(oa_ref[...], wa_ref[...].astype(BF16), preferred_element_type=F32)
    b = jnp.dot(oh_ref[...], wb_ref[...].astype(BF16), preferred_element_type=F32)
    ga = jax.nn.sigmoid(ga_ref[...].astype(F32))
    gb = jax.nn.sigmoid(gb_ref[...].astype(F32))
    o_ref[...] = (ga * a + gb * b).astype(BF16)


def _merge(oa, oh, w_pa, w_pb, gates, tm, tn, tag):
    M = oa.shape[0]
    nj = D_MODEL // tn
    return pl.pallas_call(
        _merge_kernel, grid=(M // tm, nj),
        in_specs=[pl.BlockSpec((tm, ATT_WIDTH), lambda i, j: (i, 0)),
                  pl.BlockSpec((tm, HG_WIDTH), lambda i, j: (i, 0)),
                  pl.BlockSpec((None, ATT_WIDTH, tn), lambda i, j: (0, 0, j)),
                  pl.BlockSpec((None, HG_WIDTH, tn), lambda i, j: (0, 0, j)),
                  pl.BlockSpec((tm, tn), lambda i, j: (i, j)),
                  pl.BlockSpec((tm, tn), lambda i, j: (i, j + nj))],
        out_specs=pl.BlockSpec((tm, tn), lambda i, j: (i, j)),
        out_shape=jax.ShapeDtypeStruct((M, D_MODEL), BF16),
        compiler_params=_params(("parallel", "arbitrary")), name=f"merge_{tag}",
    )(oa, oh, w_pa, w_pb, gates, gates)


def _cross_prompt_kernel(x_ref, g_ref, wq_ref, mk_ref, mv_ref, wo_ref, o_ref):
    x = x_ref[...]
    u = _rms(x, g_ref[...]).astype(BF16)
    q = jnp.dot(u, wq_ref[...], preferred_element_type=F32).astype(BF16)
    heads = []
    for h in range(X_HEADS):
        hs = slice(h * X_HEAD_DIM, (h + 1) * X_HEAD_DIM)
        s = lax.dot_general(q[:, hs], mk_ref[:, hs], _NT, preferred_element_type=F32) * (X_HEAD_DIM ** -0.5)
        p = jnp.exp(s - jnp.max(s, axis=1, keepdims=True))
        p = p / jnp.sum(p, axis=1, keepdims=True)
        heads.append(jnp.dot(p.astype(BF16), mv_ref[:, hs], preferred_element_type=F32))
    o = jnp.concatenate(heads, axis=1).astype(BF16)
    o_ref[...] = x + jnp.dot(o, wo_ref[...], preferred_element_type=F32)


def _cross_prompt(h1, g_cross, w_cq_bf, mk_bf, mv_bf, w_co_bf, tm):
    M = h1.shape[0]
    return pl.pallas_call(
        _cross_prompt_kernel, grid=(M // tm,),
        in_specs=[pl.BlockSpec((tm, D_MODEL), lambda i: (i, 0)), _small_spec((1, D_MODEL)),
                  _small_spec((D_MODEL, X_WIDTH)), _small_spec(tuple(mk_bf.shape)),
                  _small_spec(tuple(mv_bf.shape)), _small_spec((X_WIDTH, D_MODEL))],
        out_specs=pl.BlockSpec((tm, D_MODEL), lambda i: (i, 0)),
        out_shape=jax.ShapeDtypeStruct((M, D_MODEL), F32),
        compiler_params=_params(("parallel",)), name="cross_prompt",
    )(h1, g_cross.reshape(1, -1), w_cq_bf, mk_bf, mv_bf, w_co_bf)


def _cross_decode_kernel(q_ref, mk_ref, mv_ref, o_ref):
    rnd = lambda x: x.astype(BF16).astype(F32)
    q = rnd(q_ref[...])
    outs = []
    for h in range(X_HEADS):
        hs = slice(h * X_HEAD_DIM, (h + 1) * X_HEAD_DIM)
        s = jnp.sum(rnd(mk_ref[:, hs]) * q[:, hs], axis=1, keepdims=True) * (X_HEAD_DIM ** -0.5)
        p = jnp.exp(s - jnp.max(s, axis=0, keepdims=True))
        p = p / jnp.sum(p, axis=0, keepdims=True)
        outs.append(jnp.sum(rnd(p) * rnd(mv_ref[:, hs]), axis=0, keepdims=True))
    o_ref[...] = jnp.concatenate(outs, axis=1).astype(BF16)


def _cross_decode(q, mem_k, mem_v):
    Bd, mem = mem_k.shape[0], mem_k.shape[1]
    row = pl.BlockSpec((None, 1, X_WIDTH), lambda b: (b, 0, 0))
    mem_spec = pl.BlockSpec((None, mem, X_WIDTH), lambda b: (b, 0, 0))
    out = pl.pallas_call(
        _cross_decode_kernel, grid=(Bd,), in_specs=[row, mem_spec, mem_spec], out_specs=row,
        out_shape=jax.ShapeDtypeStruct((Bd, 1, X_WIDTH), BF16),
        compiler_params=_params(("parallel",)), name="cross_decode",
    )(q.reshape(Bd, 1, X_WIDTH), mem_k.reshape(Bd, mem, X_WIDTH), mem_v.reshape(Bd, mem, X_WIDTH))
    return out.reshape(Bd, X_WIDTH)


def _route_kernel(x_ref, g_ref, w_ref, b_ref, cin_ref, uin_ref, u_ref, eid_ref, wt_ref, rank_ref, cnt_ref,
                  c_scr, *, n_rows):
    i = pl.program_id(0)
    tm = x_ref.shape[0]

    @pl.when(i == 0)
    def _():
        c_scr[...] = cin_ref[...]

    u = _rms(x_ref[...], g_ref[...])
    u_ref[...] = u
    dot = lambda a, b: jnp.dot(a, b, preferred_element_type=F32)
    logits = dot(u.astype(BF16), w_ref[...]) + b_ref[...]

    lane = lax.broadcasted_iota(I32, (tm, LANES), 1)
    big = jnp.int32(1 << 20)
    first_max = lambda vals, vmax: jnp.min(jnp.where(vals == vmax, lane, big), axis=1, keepdims=True)
    glv = jnp.where(lane < N_GROUPS, logits, -jnp.inf)
    gmax = jnp.max(glv, axis=1, keepdims=True)
    gsel = first_max(glv, gmax)
    p_group = 1.0 / jnp.sum(jnp.exp(glv - gmax), axis=1, keepdims=True)
    e_lane = lane - N_GROUPS
    in_group = (e_lane >= 0) & (e_lane < N_EXPERTS) & (e_lane // EXPERTS_PER_GROUP == gsel)
    ev = jnp.where(in_group, logits, -jnp.inf)
    v1 = jnp.max(ev, axis=1, keepdims=True)
    i1 = first_max(ev, v1)
    ev2 = jnp.where(lane == i1, -jnp.inf, ev)
    v2 = jnp.max(ev2, axis=1, keepdims=True)
    i2 = first_max(ev2, v2)
    t2 = jnp.exp(v2 - v1)
    w1 = p_group / (1.0 + t2)
    w2 = w1 * t2
    e1 = i1 - N_GROUPS
    e2 = i2 - N_GROUPS
    eid_ref[...] = jnp.where(lane == 0, e1, jnp.where(lane == 1, e2, 0))
    wt_ref[...] = jnp.where(lane == 0, w1, jnp.where(lane == 1, w2, 0.0))

    valid = (lax.broadcasted_iota(I32, (tm, 1), 0) + i * tm) < n_rows
    oh1 = jnp.where((lane == e1) & valid, 1.0, 0.0)
    oh2 = jnp.where((lane == e2) & valid, 1.0, 0.0)
    cnt = (oh1 + oh2).astype(BF16)
    r_i = lax.broadcasted_iota(I32, (tm, tm), 0)
    c_i = lax.broadcasted_iota(I32, (tm, tm), 1)
    strict = jnp.where(c_i < r_i, 1.0, 0.0).astype(BF16)
    before = dot(strict, cnt) + c_scr[...]
    r1 = jnp.sum(oh1 * before, axis=1, keepdims=True)
    r2 = jnp.sum(oh2 * before, axis=1, keepdims=True)
    rank_ref[...] = jnp.where(lane == 0, r1, jnp.where(lane == 1, r2, 0.0)).astype(I32)
    c_scr[...] = c_scr[...] + jnp.sum(oh1 + oh2, axis=0, keepdims=True)
    cnt_ref[...] = c_scr[...]


def _route(x, g_ffn, w_router, b_router, counts_in, u_all, row_off, tm, tag):
    M = x.shape[0]
    T = u_all.shape[0]
    boff = row_off // tm
    kern = functools.partial(_route_kernel, n_rows=M)
    lane_out = lambda dt: jax.ShapeDtypeStruct((M, LANES), dt)
    outs = pl.pallas_call(
        kern, grid=(M // tm,),
        in_specs=[pl.BlockSpec((tm, D_MODEL), lambda i: (i, 0)), _small_spec((1, D_MODEL)),
                  _small_spec((D_MODEL, LANES)), _small_spec((1, LANES)), _small_spec((1, LANES)),
                  pl.BlockSpec(memory_space=pl.ANY)],
        out_specs=[pl.BlockSpec((tm, D_MODEL), lambda i: (i + boff, 0)),
                   pl.BlockSpec((tm, LANES), lambda i: (i, 0)), pl.BlockSpec((tm, LANES), lambda i: (i, 0)),
                   pl.BlockSpec((tm, LANES), lambda i: (i, 0)), _small_spec((1, LANES))],
        out_shape=[jax.ShapeDtypeStruct((T, D_MODEL), F32), lane_out(I32), lane_out(F32), lane_out(I32),
                   jax.ShapeDtypeStruct((1, LANES), F32)],
        scratch_shapes=[pltpu.VMEM((1, LANES), F32)],
        input_output_aliases={5: 0},
        compiler_params=_params(("arbitrary",)), name=f"moe_route_{tag}",
    )(x, g_ffn.reshape(1, -1), w_router, b_router, counts_in, u_all)
    return outs


def _plan_kernel(cnt_ref, pstart_ref, blkexp_ref, nblk_ref, *, n_blocks):
    def per_expert(e, acc):
        pstart_ref[e] = acc
        nb = (cnt_ref[e] + MOE_ROWS - 1) // MOE_ROWS
        first = acc // MOE_ROWS

        def fill(b, c):
            blkexp_ref[b] = e
            return c

        lax.fori_loop(first, first + nb, fill, 0)
        return acc + nb * MOE_ROWS

    used = lax.fori_loop(0, N_EXPERTS, per_expert, jnp.int32(0)) // MOE_ROWS
    nblk_ref[0] = used
    last = blkexp_ref[jnp.maximum(used - 1, 0)]

    def tail(b, c):
        blkexp_ref[b] = last
        return c

    lax.fori_loop(used, n_blocks, tail, 0)


def _plan(counts, n_blocks):
    smem = pl.BlockSpec(memory_space=pltpu.SMEM)
    return pl.pallas_call(
        functools.partial(_plan_kernel, n_blocks=n_blocks),
        in_specs=[smem], out_specs=[smem, smem, smem],
        out_shape=[jax.ShapeDtypeStruct((N_EXPERTS,), I32), jax.ShapeDtypeStruct((n_blocks,), I32),
                   jax.ShapeDtypeStruct((1,), I32)],
        name="moe_plan",
    )(counts)


def _dispatch_kernel(eid_ref, rank_ref, pstart_ref, cnt_ref, nblk_ref, u_ref, xs_hbm, zbuf, sem_z, sem,
                     *, n_tokens, n_blocks):
    R = MOE_ROWS
    i = pl.program_id(0)
    tm = u_ref.shape[0]

    def zero_copy(b):
        return pltpu.make_async_copy(zbuf, xs_hbm.at[pl.ds(pl.multiple_of(b * R, R), R), :], sem_z)

    def partial_block(e):
        c = cnt_ref[e]
        return (c % R) != 0, (pstart_ref[e] + c) // R

    def zero_partial(start):
        def body(e, n):
            has, b = partial_block(e)

            @pl.when(has)
            def _():
                zero_copy(b).start() if start else zero_copy(b).wait()

            return n
        return body

    def zero_unused(start):
        def body(b, n):
            zero_copy(b).start() if start else zero_copy(b).wait()
            return n
        return body

    @pl.when(i == 0)
    def _():
        zbuf[...] = jnp.zeros(zbuf.shape, F32)
        for start in (True, False):
            lax.fori_loop(0, N_EXPERTS, zero_partial(start), 0)
            lax.fori_loop(nblk_ref[0], n_blocks, zero_unused(start), 0)

    def row_copy(r, k):
        a = 2 * (i * tm + r) + k
        d = pstart_ref[eid_ref[a]] + rank_ref[a]
        return pltpu.make_async_copy(u_ref.at[pl.ds(r, 1), :], xs_hbm.at[pl.ds(d, 1), :], sem)

    def start_row(r, n):
        row_copy(r, 0).start()
        row_copy(r, 1).start()
        return n

    def wait_row(r, n):
        row_copy(r, 0).wait()
        row_copy(r, 1).wait()
        return n

    n_rows = jnp.minimum(tm, n_tokens - i * tm)
    lax.fori_loop(0, n_rows, start_row, 0)
    lax.fori_loop(0, n_rows, wait_row, 0)


def _dispatch(eid, rank, pstart, counts, n_used, u_all, n_blocks, tm):
    T = u_all.shape[0]
    grid_spec = pltpu.PrefetchScalarGridSpec(
        num_scalar_prefetch=5, grid=(pl.cdiv(T, tm),),
        in_specs=[pl.BlockSpec((tm, D_MODEL), lambda i, *_: (i, 0))],
        out_specs=pl.BlockSpec(memory_space=pl.ANY),
        scratch_shapes=[pltpu.VMEM((MOE_ROWS, D_MODEL), F32), pltpu.SemaphoreType.DMA(()),
                        pltpu.SemaphoreType.DMA(())])
    return pl.pallas_call(
        functools.partial(_dispatch_kernel, n_tokens=T, n_blocks=n_blocks), grid_spec=grid_spec,
        out_shape=jax.ShapeDtypeStruct((n_blocks * MOE_ROWS, D_MODEL), F32),
        compiler_params=_params(("arbitrary",)), name="moe_dispatch",
    )(eid, rank, pstart, counts, n_used, u_all)


def _expert_kernel(blkexp_ref, nblk_ref, x_ref, wg_ref, wu_ref, wd_ref, o_ref, wg_s, wu_s, wd_s):
    i = pl.program_id(0)

    @pl.when(i < nblk_ref[0])
    def _():
        prev = blkexp_ref[jnp.maximum(i - 1, 0)]

        @pl.when((i == 0) | (blkexp_ref[i] != prev))
        def _():
            wg_s[...] = wg_ref[...].astype(BF16)
            wu_s[...] = wu_ref[...].astype(BF16)
            wd_s[...] = wd_ref[...].astype(BF16)

        x = x_ref[...].astype(BF16)
        hg = jnp.dot(x, wg_s[...], preferred_element_type=F32)
        hu = jnp.dot(x, wu_s[...], preferred_element_type=F32)
        h = (hg * jax.nn.sigmoid(hg) * hu).astype(BF16)
        o_ref[...] = jnp.dot(h, wd_s[...], preferred_element_type=F32)

    @pl.when(i >= nblk_ref[0])
    def _():
        o_ref[...] = jnp.zeros(o_ref.shape, F32)


def _experts(xs, blk_exp, n_used, we_g, we_u, we_d, n_blocks):
    wspec = lambda k, n: pl.BlockSpec((None, None, k, n), lambda i, be, nb: (0, be[i], 0, 0))
    grid_spec = pltpu.PrefetchScalarGridSpec(
        num_scalar_prefetch=2, grid=(n_blocks,),
        in_specs=[pl.BlockSpec((MOE_ROWS, D_MODEL), lambda i, be, nb: (jnp.maximum(jnp.minimum(i, nb[0] - 1), 0), 0)),
                  wspec(D_MODEL, EXPERT_FF), wspec(D_MODEL, EXPERT_FF), wspec(EXPERT_FF, D_MODEL)],
        out_specs=pl.BlockSpec((MOE_ROWS, D_MODEL), lambda i, be, nb: (i, 0)),
        scratch_shapes=[pltpu.VMEM((D_MODEL, EXPERT_FF), BF16), pltpu.VMEM((D_MODEL, EXPERT_FF), BF16),
                        pltpu.VMEM((EXPERT_FF, D_MODEL), BF16)])
    return pl.pallas_call(
        _expert_kernel, grid_spec=grid_spec,
        out_shape=jax.ShapeDtypeStruct((n_blocks * MOE_ROWS, D_MODEL), F32),
        compiler_params=_params(("arbitrary",)), name="moe_experts",
    )(blk_exp, n_used, xs, we_g, we_u, we_d)


def _combine_kernel(eid_ref, rank_ref, pstart_ref, h_ref, wt_ref, gf_ref, yb_hbm, o_ref, ybuf0, ybuf1, sem,
                    *, dest_off):
    i = pl.program_id(0)
    tm = h_ref.shape[0]
    base = dest_off + i * (2 * tm)

    def row_copy(r, k, buf):
        a = base + 2 * r + k
        d = pstart_ref[eid_ref[a]] + rank_ref[a]
        return pltpu.make_async_copy(yb_hbm.at[pl.ds(d, 1), :], buf.at[pl.ds(r, 1), :], sem)

    def start(r, c):
        row_copy(r, 0, ybuf0).start()
        row_copy(r, 1, ybuf1).start()
        return c

    def wait(r, c):
        row_copy(r, 0, ybuf0).wait()
        row_copy(r, 1, ybuf1).wait()
        return c

    lax.fori_loop(0, tm, start, 0, unroll=4)
    lax.fori_loop(0, tm, wait, 0, unroll=4)
    wt = wt_ref[...]
    h = h_ref[...] + (wt[:, 0:1] * ybuf0[...] + wt[:, 1:2] * ybuf1[...])
    o_ref[...] = _rms(h, gf_ref[...])


def _combine(h2, wts, eid, rank, pstart, yb, norm_final, dest_off, tm, tag):
    M = h2.shape[0]
    grid_spec = pltpu.PrefetchScalarGridSpec(
        num_scalar_prefetch=3, grid=(M // tm,),
        in_specs=[pl.BlockSpec((tm, D_MODEL), lambda i, *_: (i, 0)), pl.BlockSpec((tm, LANES), lambda i, *_: (i, 0)),
                  pl.BlockSpec((1, D_MODEL), lambda i, *_: (0, 0)), pl.BlockSpec(memory_space=pl.ANY)],
        out_specs=pl.BlockSpec((tm, D_MODEL), lambda i, *_: (i, 0)),
        scratch_shapes=[pltpu.VMEM((tm, D_MODEL), F32), pltpu.VMEM((tm, D_MODEL), F32),
                        pltpu.SemaphoreType.DMA(())])
    return pl.pallas_call(
        functools.partial(_combine_kernel, dest_off=dest_off), grid_spec=grid_spec,
        out_shape=jax.ShapeDtypeStruct((M, D_MODEL), F32),
        compiler_params=_params(("arbitrary",)), name=f"moe_combine_{tag}",
    )(eid, rank, pstart, h2, wts, norm_final.reshape(1, -1), yb)


def _moe_and_final_norm(h2_p, h2_s, g_ffn, wr_g, br_g, wr_e, br_e, we_g, we_u, we_d, norm_final, tm_p):
    Tp, Ts = h2_p.shape[0], h2_s.shape[0]
    T = Tp + Ts
    pad = LANES - N_GROUPS - N_EXPERTS
    w_router = jnp.concatenate([wr_g, wr_e, jnp.zeros((D_MODEL, pad), F32)], axis=1).astype(BF16)
    b_router = jnp.concatenate([br_g, br_e, jnp.zeros((pad,), F32)]).reshape(1, LANES)
    u_all = jnp.zeros((T, D_MODEL), F32)
    zero_counts = jnp.zeros((1, LANES), F32)
    u_all, eid_p, wt_p, rank_p, counts = _route(h2_p, g_ffn, w_router, b_router, zero_counts, u_all, 0, tm_p, "p")
    u_all, eid_s, wt_s, rank_s, counts = _route(h2_s, g_ffn, w_router, b_router, counts, u_all, Tp, Ts, "s")
    eid = jnp.concatenate([eid_p[:, :2].reshape(-1), eid_s[:, :2].reshape(-1)])
    rank = jnp.concatenate([rank_p[:, :2].reshape(-1), rank_s[:, :2].reshape(-1)])
    A = 2 * T
    n_blocks = (A + N_EXPERTS * (MOE_ROWS - 1)) // MOE_ROWS + 1
    cnt = counts[0, :N_EXPERTS].astype(I32)
    pstart, blk_exp, n_used = _plan(cnt, n_blocks)
    xs = _dispatch(eid, rank, pstart, cnt, n_used, u_all, n_blocks, 256)
    yb = _experts(xs, blk_exp, n_used, we_g, we_u, we_d, n_blocks)
    y_p = _combine(h2_p, wt_p, eid, rank, pstart, yb, norm_final, 0, 256, "p")
    y_s = _combine(h2_s, wt_s, eid, rank, pstart, yb, norm_final, 2 * Tp, Ts, "s")
    return y_p, y_s


def kernel(x_prompt, x_sample, mem_prompt, cache_k, cache_v, cache_mem_k, cache_mem_v, state_hgrn, page_table,
           norm_mix, w_in, lambda_q1, lambda_k1, lambda_q2, lambda_k2, subln, hgrn_lb, hgrn_norm, w_pa, w_pb,
           w_out, norm_cross, w_cq, w_ck, w_cv, w_co, norm_ffn, w_router_group, b_router_group,
           w_router_expert, b_router_expert, w_e_gate, w_e_up, w_e_down, norm_final):
    assert w_in.shape[0] == 1, "single-layer step"
    Bp, S, D = x_prompt.shape
    Bd, Ld, _ = x_sample.shape
    assert Bp == 1 and Ld == 1
    n_pages = page_table.shape[1]
    page = cache_k.shape[2]
    past_len = n_pages * page
    xp = x_prompt.reshape(S, D)
    xs = x_sample.reshape(Bd, D)
    lam = (lambda_q1, lambda_k1, lambda_q2, lambda_k2)
    TM = 1024
    w_in, w_pa, w_pb, w_out = (w.astype(BF16) for w in (w_in, w_pa, w_pb, w_out))

    q, k, k_bf, v, v_bf, hg, gates = _in_projection(xp, norm_mix[0], w_in, jnp.arange(S, dtype=I32), TM, 512, "p",
                                                    BF16)
    oa = _attention_prompt(q, k_bf, v_bf, *lam, subln, ATTN_TQ, ATTN_TK)
    oh, st_p = _hgrn_prompt(hg, hgrn_lb, hgrn_norm, 256)
    merged = _merge(oa, oh, w_pa, w_pb, gates, TM, 512, "p")
    (h1_p,) = _mm(merged, w_out, col_off=0, n_cols=D, tm=TM, tn=512, epilogue=_epi_residual, out_dtypes=[F32],
                  extras=[(xp, pl.BlockSpec((TM, 512), lambda i, j: (i, j)))], name="outproj_p")

    pos_s = jnp.full((Bd,), past_len, I32)
    qs, ks, ks_bf, vs, vs_bf, hgs, gates_s = _in_projection(xs, norm_mix[0], w_in, pos_s, Bd, 512, "s", F32)
    ck = jnp.transpose(cache_k[0], (0, 2, 3, 4, 1)).reshape(cache_k.shape[1], ATT_WIDTH, page)
    cv = cache_v[0].reshape(cache_v.shape[1], page * ATT_HEADS, 2 * ATT_HEAD_DIM)
    oa_s = _attention_decode(qs, ks_bf, vs_bf, ck, cv, page_table, *lam, subln)
    oh_s, st_s = _hgrn_decode(hgs, state_hgrn[0], hgrn_lb, hgrn_norm)
    merged_s = _merge(oa_s, oh_s, w_pa, w_pb, gates_s, Bd, 512, "s")
    (h1_s,) = _mm(merged_s, w_out, col_off=0, n_cols=D, tm=Bd, tn=512, epilogue=_epi_residual, out_dtypes=[F32],
                  extras=[(xs, pl.BlockSpec((Bd, 512), lambda i, j: (i, j)))], name="outproj_s")

    mem = mem_prompt.reshape(-1, D)
    mk, mk_bf = _mm(mem, w_ck, col_off=0, n_cols=X_WIDTH, tm=mem.shape[0], tn=X_WIDTH, epilogue=_epi_v,
                    out_dtypes=[F32, BF16], name="mem_k")
    mv, mv_bf = _mm(mem, w_cv, col_off=0, n_cols=X_WIDTH, tm=mem.shape[0], tn=X_WIDTH, epilogue=_epi_v,
                    out_dtypes=[F32, BF16], name="mem_v")
    w_cq_bf = w_cq[0].astype(BF16)
    w_co_bf = w_co[0].astype(BF16)
    h2_p = _cross_prompt(h1_p, norm_cross[0], w_cq_bf, mk_bf, mv_bf, w_co_bf, 256)
    (qc_s,) = _mm(h1_s, w_cq, col_off=0, n_cols=X_WIDTH, tm=Bd, tn=X_WIDTH, epilogue=_epi_plain,
                  out_dtypes=[F32], norm_g=norm_cross[0], name="cross_q_s")
    oc_s = _cross_decode(qc_s, cache_mem_k[0], cache_mem_v[0])
    (h2_s,) = _mm(oc_s, w_co, col_off=0, n_cols=D, tm=Bd, tn=512, epilogue=_epi_residual, out_dtypes=[F32],
                  extras=[(h1_s, pl.BlockSpec((Bd, 512), lambda i, j: (i, j)))], name="cross_o_s")

    y_p, y_s = _moe_and_final_norm(h2_p, h2_s, norm_ffn[0], w_router_group[0], b_router_group[0],
                                   w_router_expert[0], b_router_expert[0], w_e_gate, w_e_up, w_e_down,
                                   norm_final, 256)

    return (y_p.reshape(Bp, S, D), y_s.reshape(Bd, Ld, D),
            k.reshape(1, Bp, S, ATT_HEADS, 2, ATT_HEAD_DIM), v.reshape(1, Bp, S, ATT_HEADS, 2 * ATT_HEAD_DIM),
            ks.reshape(1, Bd, Ld, ATT_HEADS, 2, ATT_HEAD_DIM), vs.reshape(1, Bd, Ld, ATT_HEADS, 2 * ATT_HEAD_DIM),
            st_p.reshape(1, Bp, HG_HEADS, HG_KEY, HG_VAL), st_s.reshape(1, Bd, HG_HEADS, HG_KEY, HG_VAL),
            mk.reshape(1, Bp, -1, X_HEADS, X_HEAD_DIM), mv.reshape(1, Bp, -1, X_HEADS, X_HEAD_DIM))
```

```python
import functools
import math

import numpy as np
import jax
import jax.numpy as jnp
from jax import lax
from jax.experimental import pallas as pl
from jax.experimental.pallas import tpu as pltpu

F32 = jnp.float32
BF16 = jnp.bfloat16
I32 = jnp.int32

D_MODEL = 2048
ATT_HEADS = 8
ATT_HEAD_DIM = 64
ATT_WIDTH = ATT_HEADS * 2 * ATT_HEAD_DIM
ROPE_THETA = 10000.0
HG_HEADS = 8
HG_KEY = 128
HG_VAL = 128
HG_WIDTH = HG_HEADS * HG_VAL
X_HEADS = 4
X_HEAD_DIM = 128
X_WIDTH = X_HEADS * X_HEAD_DIM
N_GROUPS = 4
EXPERTS_PER_GROUP = 8
N_EXPERTS = N_GROUPS * EXPERTS_PER_GROUP
EXPERT_FF = 512
RMS_EPS = 1e-6
NEG_INF = -1e30
LAMBDA_INIT = 0.8 - 0.6 * math.exp(-0.3 * 0)

LANES = 128
SUBLANES = 8
VMEM_LIMIT = 52 * 1024 * 1024

HG_CHUNK = 64
HG_LEVELS = (16, 32, 64)
MOE_ROWS = 128
PAGES_PER_STEP = 16
ATTN_TQ = 512
ATTN_TK = 512

_NT = (((1,), (1,)), ((), ()))
_TN = (((0,), (0,)), ((), ()))


def _params(sem):
    return pltpu.CompilerParams(dimension_semantics=sem, vmem_limit_bytes=VMEM_LIMIT)


def _rms(x, g):
    return x * lax.rsqrt(jnp.mean(x * x, axis=-1, keepdims=True) + RMS_EPS) * g


def _mm_kernel(*refs, n_extra, n_out, norm, epilogue):
    x_ref = refs[0]
    pos = 1
    if norm:
        g_ref = refs[1]
        pos = 2
    w_ref = refs[pos]
    extras = refs[pos + 1:pos + 1 + n_extra]
    outs = refs[pos + 1 + n_extra:pos + 1 + n_extra + n_out]
    if norm:
        u_ref = refs[-1]

        @pl.when(pl.program_id(1) == 0)
        def _():
            u_ref[...] = _rms(x_ref[...].astype(F32), g_ref[...]).astype(BF16)

        u = u_ref[...]
    else:
        u = x_ref[...].astype(BF16)
    acc = jnp.dot(u, w_ref[...].astype(BF16), preferred_element_type=F32)
    epilogue(acc, extras, outs)


def _mm(x, w, *, col_off, n_cols, tm, tn, epilogue, out_dtypes, norm_g=None, extras=(), name):
    M, K = x.shape
    assert M % tm == 0 and n_cols % tn == 0 and col_off % tn == 0
    joff = col_off // tn
    if w.ndim == 3:
        w_spec = pl.BlockSpec((None, K, tn), lambda i, j: (0, 0, j + joff))
    else:
        w_spec = pl.BlockSpec((K, tn), lambda i, j: (0, j + joff))
    in_specs = [pl.BlockSpec((tm, K), lambda i, j: (i, 0))]
    args = [x]
    if norm_g is not None:
        in_specs.append(pl.BlockSpec((1, K), lambda i, j: (0, 0)))
        args.append(norm_g.reshape(1, K))
    in_specs.append(w_spec)
    args.append(w)
    for arr, spec in extras:
        in_specs.append(spec)
        args.append(arr)
    out_specs = [pl.BlockSpec((tm, tn), lambda i, j: (i, j)) for _ in out_dtypes]
    out_shape = [jax.ShapeDtypeStruct((M, n_cols), dt) for dt in out_dtypes]
    scratch = [pltpu.VMEM((tm, K), BF16)] if norm_g is not None else []
    kern = functools.partial(_mm_kernel, n_extra=len(extras), n_out=len(out_dtypes),
                             norm=norm_g is not None, epilogue=epilogue)
    res = pl.pallas_call(
        kern, grid=(M // tm, n_cols // tn), in_specs=in_specs, out_specs=out_specs,
        out_shape=out_shape, scratch_shapes=scratch,
        compiler_params=_params(("parallel", "arbitrary")), name=name)(*args)
    return res


def _rope_tile(x, cos, sin_signed):
    first = (lax.broadcasted_iota(I32, (x.shape[0], LANES), 1) % ATT_HEAD_DIM) < ATT_HEAD_DIM // 2
    outs = []
    for c in range(x.shape[1] // LANES):
        xc = x[:, c * LANES:(c + 1) * LANES]
        rot = jnp.where(first, pltpu.roll(xc, LANES - ATT_HEAD_DIM // 2, 1),
                        pltpu.roll(xc, ATT_HEAD_DIM // 2, 1))
        outs.append(xc * cos + rot * sin_signed)
    return outs[0] if len(outs) == 1 else jnp.concatenate(outs, axis=1)


def _epi_q(acc, extras, outs):
    cos_ref, sin_ref = extras
    outs[0][...] = (_rope_tile(acc, cos_ref[...], sin_ref[...]) * (ATT_HEAD_DIM ** -0.5)).astype(BF16)


def _epi_k(acc, extras, outs):
    cos_ref, sin_ref = extras
    r = _rope_tile(acc, cos_ref[...], sin_ref[...])
    outs[0][...] = r
    outs[1][...] = r.astype(BF16)


def _epi_v(acc, extras, outs):
    outs[0][...] = acc
    outs[1][...] = acc.astype(BF16)


def _epi_plain(acc, extras, outs):
    outs[0][...] = acc.astype(outs[0].dtype)


def _epi_residual(acc, extras, outs):
    outs[0][...] = extras[0][...] + acc


def _rope_tables(pos):
    half = ATT_HEAD_DIM // 2
    freqs = ROPE_THETA ** (-jnp.arange(half, dtype=F32) / half)
    ang = pos.astype(F32)[:, None] * freqs[None, :]
    c, s = jnp.cos(ang), jnp.sin(ang)
    return jnp.tile(c, (1, 4)), jnp.concatenate([-s, s, -s, s], axis=1)


def _inproj_kernel(x_ref, g_ref, w_ref, cos_ref, sin_ref, q_ref, k_ref, kb_ref, v_ref, vb_ref, hg_ref, gt_ref,
                   u_ref, *, bounds):
    j = pl.program_id(1)

    @pl.when(j == 0)
    def _():
        u_ref[...] = _rms(x_ref[...].astype(F32), g_ref[...]).astype(BF16)

    acc = jnp.dot(u_ref[...], w_ref[...], preferred_element_type=F32)
    in_seg = lambda s: (j >= bounds[s]) & (j < bounds[s + 1])

    @pl.when(in_seg(0))
    def _():
        _epi_q(acc, (cos_ref, sin_ref), (q_ref,))

    @pl.when(in_seg(1))
    def _():
        _epi_k(acc, (cos_ref, sin_ref), (k_ref, kb_ref))

    @pl.when(in_seg(2))
    def _():
        _epi_v(acc, (), (v_ref, vb_ref))

    @pl.when(in_seg(3))
    def _():
        hg_ref[...] = acc

    @pl.when(in_seg(4))
    def _():
        gt_ref[...] = acc.astype(gt_ref.dtype)


def _in_projection(x, g, w_in_bf, pos, tm, tn, tag, gate_dtype):
    M, K = x.shape
    cos, sin = _rope_tables(pos)
    widths = (ATT_WIDTH, ATT_WIDTH, ATT_WIDTH, 4 * HG_WIDTH, 2 * D_MODEL)
    bounds = tuple(int(b) for b in np.cumsum((0,) + widths) // tn)

    def seg_spec(s):
        lo, n = bounds[s], bounds[s + 1] - bounds[s]
        return pl.BlockSpec((tm, tn), lambda i, j: (i, jnp.clip(j - lo, 0, n - 1)))

    seg_of_out = (0, 1, 1, 2, 2, 3, 4)
    out_dtypes = (BF16, F32, BF16, F32, BF16, F32, gate_dtype)
    row_tbl = pl.BlockSpec((tm, LANES), lambda i, j: (i, 0))
    return pl.pallas_call(
        functools.partial(_inproj_kernel, bounds=bounds),
        grid=(M // tm, bounds[-1]),
        in_specs=[pl.BlockSpec((tm, K), lambda i, j: (i, 0), pipeline_mode=pl.Buffered(1)),
                  pl.BlockSpec((1, K), lambda i, j: (0, 0)),
                  pl.BlockSpec((None, K, tn), lambda i, j: (0, 0, j)), row_tbl, row_tbl],
        out_specs=[seg_spec(s) for s in seg_of_out],
        out_shape=[jax.ShapeDtypeStruct((M, widths[s]), dt) for s, dt in zip(seg_of_out, out_dtypes)],
        scratch_shapes=[pltpu.VMEM((tm, K), BF16)],
        compiler_params=_params(("parallel", "arbitrary")), name=f"inproj_{tag}",
    )(x, g.reshape(1, K), w_in_bf, cos, sin)


def _lambda_value(lq1, lk1, lq2, lk2):
    return (jnp.exp(jnp.sum(lq1[...] * lk1[...], axis=-1, keepdims=True))
            - jnp.exp(jnp.sum(lq2[...] * lk2[...], axis=-1, keepdims=True)) + LAMBDA_INIT)


def _attn_kernel(q_ref, k_ref, v_ref, lq1, lk1, lq2, lk2, subln_ref, o_ref, *, tq, tk):
    i = pl.program_id(1)
    q = q_ref[...]
    lane = lax.broadcasted_iota(I32, q.shape, 1)
    zero = jnp.zeros_like(q)
    qm = (jnp.where(lane < ATT_HEAD_DIM, q, zero), jnp.where(lane >= ATT_HEAD_DIM, q, zero))

    def update(off, carry, mask):
        off = pl.multiple_of(off, tk)
        kt = k_ref[pl.ds(off, tk), :]
        vt = v_ref[pl.ds(off, tk), :]
        new = []
        for m in range(2):
            mx, l, acc = carry[3 * m:3 * m + 3]
            s = lax.dot_general(qm[m], kt, _NT, preferred_element_type=F32)
            if mask is not None:
                s = jnp.where(mask, s, NEG_INF)
            mn = jnp.maximum(mx, jnp.max(s, axis=1, keepdims=True))
            alpha = jnp.exp(mx - mn)
            p = jnp.exp(s - mn)
            l = alpha * l + jnp.sum(p, axis=1, keepdims=True)
            acc = alpha * acc + jnp.dot(p.astype(BF16), vt, preferred_element_type=F32)
            new += [mn, l, acc]
        return tuple(new)

    def body(j, carry):
        return update(j * tk, carry, None)

    init = (jnp.full((tq, 1), NEG_INF, F32), jnp.zeros((tq, 1), F32), jnp.zeros((tq, LANES), F32)) * 2
    carry = lax.fori_loop(0, i * (tq // tk), body, init)
    row = lax.broadcasted_iota(I32, (tq, tk), 0)
    col = lax.broadcasted_iota(I32, (tq, tk), 1)
    for d in range(tq // tk):
        carry = update(i * tq + d * tk, carry, col + d * tk <= row)
    _, l0, a0, _, l1, a1 = carry

    lam = _lambda_value(lq1, lk1, lq2, lk2)
    o = a0 / l0 - lam * (a1 / l1)
    o_ref[...] = (_rms(o, subln_ref[...]) * (1.0 - LAMBDA_INIT)).astype(BF16)


def _small_spec(shape):
    nd = len(shape)
    return pl.BlockSpec(shape, lambda *_: (0,) * nd)


def _attention_prompt(q, k_bf, v_bf, lq1, lk1, lq2, lk2, subln, tq, tk):
    S = q.shape[0]
    assert tq % tk == 0
    lam_specs = [_small_spec((1, ATT_HEAD_DIM))] * 4
    return pl.pallas_call(
        functools.partial(_attn_kernel, tq=tq, tk=tk),
        grid=(ATT_HEADS, S // tq),
        in_specs=[pl.BlockSpec((tq, LANES), lambda h, i: (i, h)),
                  pl.BlockSpec((S, LANES), lambda h, i: (0, h)),
                  pl.BlockSpec((S, LANES), lambda h, i: (0, h))] + lam_specs
                 + [_small_spec((1, LANES))],
        out_specs=pl.BlockSpec((tq, LANES), lambda h, i: (i, h)),
        out_shape=jax.ShapeDtypeStruct((S, ATT_WIDTH), BF16),
        compiler_params=_params(("parallel", "arbitrary")), name="attn_prompt",
    )(q, k_bf, v_bf, lq1, lk1, lq2, lk2, subln)


def _decode_attn_kernel(pt_ref, q_ref, kn_ref, vn_ref, lq1, lk1, lq2, lk2, subln_ref, *rest, n_pg):
    k_refs = rest[:n_pg]
    v_refs = rest[n_pg:2 * n_pg]
    o_ref = rest[2 * n_pg]
    qr_ref, s_scr, w_scr, wn_scr, acc_ref = rest[2 * n_pg + 1:]
    phase = pl.program_id(1)
    j = pl.program_id(2)
    last = pl.num_programs(2) - 1
    nrow = 2 * ATT_HEADS

    @pl.when((phase == 0) & (j == 0))
    def _():
        row = lax.broadcasted_iota(I32, (nrow, ATT_WIDTH), 0)
        lane = lax.broadcasted_iota(I32, (nrow, ATT_WIDTH), 1)
        sel = (lane // LANES == row % ATT_HEADS) & ((lane // ATT_HEAD_DIM) % 2 == row // ATT_HEADS)
        qb = jnp.broadcast_to(q_ref[...].astype(F32), (nrow, ATT_WIDTH))
        qr_ref[...] = jnp.where(sel, qb, 0.0).astype(BF16)

    @pl.when(phase == 0)
    def _():
        qr = qr_ref[...]
        s_scr[j] = jnp.concatenate(
            [jnp.dot(qr, kr[...].astype(BF16), preferred_element_type=F32) for kr in k_refs], axis=1)

    @pl.when((phase == 1) & (j == 0))
    def _():
        sn = jnp.sum(qr_ref[...].astype(F32) * kn_ref[...].astype(F32), axis=1, keepdims=True)
        s = s_scr[...]
        m = jnp.maximum(jnp.max(jnp.max(s, axis=2, keepdims=True), axis=0, keepdims=True), sn[None])
        e = jnp.exp(s - m)
        en = jnp.exp(sn[None] - m)
        l = jnp.sum(jnp.sum(e, axis=2, keepdims=True), axis=0, keepdims=True) + en
        p = e / l
        pn = (en / l)[0]
        lam = _lambda_value(lq1, lk1, lq2, lk2)
        w = p[:, :ATT_HEADS, :] - lam * p[:, ATT_HEADS:, :]
        w_scr[...] = jnp.concatenate([w, jnp.zeros_like(w)], axis=1).astype(BF16)
        wn_scr[...] = pn[:ATT_HEADS] - lam * pn[ATT_HEADS:]
        acc_ref[...] = jnp.zeros(acc_ref.shape, F32)

    @pl.when(phase == 1)
    def _():
        wb = w_scr[j]
        page = wb.shape[1] // n_pg
        heads = []
        for h in range(ATT_HEADS):
            vh = jnp.concatenate([vr[pl.ds(h, page, stride=ATT_HEADS), :].astype(BF16) for vr in v_refs], axis=0)
            heads.append(jnp.dot(wb, vh, preferred_element_type=F32))
        acc_ref[...] = acc_ref[...] + jnp.concatenate(heads, axis=1)

    @pl.when((phase == 1) & (j == last))
    def _():
        wn = wn_scr[...].astype(BF16).astype(F32)
        od = acc_ref[...][:ATT_HEADS] + wn * vn_ref[...].astype(F32)
        own = (lax.broadcasted_iota(I32, od.shape, 1) // LANES) == lax.broadcasted_iota(I32, od.shape, 0)
        od = jnp.where(own, od, 0.0)
        ms = jnp.sum(od * od, axis=1, keepdims=True) / (2 * ATT_HEAD_DIM)
        y = od * lax.rsqrt(ms + RMS_EPS) * subln_ref[...] * (1.0 - LAMBDA_INIT)
        o_ref[...] = jnp.sum(y, axis=0, keepdims=True).astype(BF16)


def _attention_decode(q, k_new_bf, v_new_bf, cache_kt, cache_v, page_table, lq1, lk1, lq2, lk2, subln):
    Bd = q.shape[0]
    page = cache_kt.shape[2]
    n_pages = page_table.shape[1]
    n_pg = math.gcd(n_pages, PAGES_PER_STEP)
    assert n_pages % n_pg == 0
    subln_w = jnp.tile(subln, (1, ATT_HEADS))

    n_groups = n_pages // n_pg
    nrow = 2 * ATT_HEADS

    def k_spec(g):
        return pl.BlockSpec((None, ATT_WIDTH, page), lambda b, ph, j, pt: (
            pt[b * n_pages + jnp.where(ph == 0, j, n_groups - 1) * n_pg + g], 0, 0))

    def v_spec(g):
        return pl.BlockSpec((None, page * ATT_HEADS, 2 * ATT_HEAD_DIM), lambda b, ph, j, pt: (
            pt[b * n_pages + jnp.where(ph == 0, 0, j) * n_pg + g], 0, 0))

    row_spec = pl.BlockSpec((None, 1, ATT_WIDTH), lambda b, ph, j, pt: (b, 0, 0))
    small = lambda shape: pl.BlockSpec(shape, lambda b, ph, j, pt: (0,) * len(shape))
    grid_spec = pltpu.PrefetchScalarGridSpec(
        num_scalar_prefetch=1, grid=(Bd, 2, n_groups),
        in_specs=[row_spec, row_spec, row_spec] + [small((1, ATT_HEAD_DIM))] * 4 + [small((1, ATT_WIDTH))]
                 + [k_spec(g) for g in range(n_pg)] + [v_spec(g) for g in range(n_pg)],
        out_specs=row_spec,
        scratch_shapes=[pltpu.VMEM((nrow, ATT_WIDTH), BF16),
                        pltpu.VMEM((n_groups, nrow, n_pg * page), F32),
                        pltpu.VMEM((n_groups, nrow, n_pg * page), BF16),
                        pltpu.VMEM((ATT_HEADS, 1), F32),
                        pltpu.VMEM((nrow, ATT_WIDTH), F32)])
    out = pl.pallas_call(
        functools.partial(_decode_attn_kernel, n_pg=n_pg), grid_spec=grid_spec,
        out_shape=jax.ShapeDtypeStruct((Bd, 1, ATT_WIDTH), BF16),
        compiler_params=_params(("parallel", "arbitrary", "arbitrary")), name="attn_decode",
    )(page_table.reshape(-1), q.reshape(Bd, 1, -1), k_new_bf.reshape(Bd, 1, -1), v_new_bf.reshape(Bd, 1, -1),
      lq1, lk1, lq2, lk2, subln_w, *([cache_kt] * n_pg), *([cache_v] * n_pg))
    return out.reshape(Bd, ATT_WIDTH)


def _hgrn_masks():
    C = HG_CHUNK
    t = np.arange(C)[:, None]
    r = np.arange(C)[None, :]
    blocks = [(r <= t), (r > t)]
    for B in HG_LEVELS:
        mid = (t // B) * B + B // 2 - 1
        second = (t % B) >= B // 2
        blocks.append(np.where(second, (r > mid) & (r <= t), (r > t) & (r <= mid)))
    return jnp.asarray(np.concatenate(blocks, axis=0).astype(np.float32), dtype=BF16)


def _lower_bound(lb_ref):
    a = lb_ref[...].astype(F32)
    e = jnp.exp(a - jnp.max(a, axis=0, keepdims=True))
    return e[0:1] / jnp.sum(e, axis=0, keepdims=True)


def _group_rows(x, j):
    return jnp.broadcast_to(x[:, j:j + 1, :], x.shape)


def _split3(x):
    hi = x.astype(BF16)
    r1 = x - hi.astype(F32)
    mid = r1.astype(BF16)
    lo = (r1 - mid.astype(F32)).astype(BF16)
    return hi, mid, lo


def _hgrn_tile(q, z, v, lb, msk, s_t):
    C = HG_CHUNK
    n = q.shape[0] // C
    dot = lambda a, b: jnp.dot(a, b, preferred_element_type=F32)
    rows = lambda x, c: x[c * C:(c + 1) * C]
    chunks = range(n)

    logf = jnp.log(lb + (1.0 - lb) * jax.nn.sigmoid(z))
    kk = (1.0 - lb) * jax.nn.sigmoid(-z)
    v_bf = v.astype(BF16)
    parts = _split3(logf)
    e = [sum(dot(msk, rows(p, c)) for p in parts) for c in chunks]
    b = [ec[0:C] for ec in e]
    qi = [(rows(q, c) * jnp.exp(b[c])).astype(BF16) for c in chunks]
    kl = [(rows(kk, c) * jnp.exp(e[c][C:2 * C])).astype(BF16) for c in chunks]
    kv = [lax.dot_general(rows(v_bf, c), kl[c], _TN, preferred_element_type=F32) for c in chunks]

    t_idx = lax.broadcasted_iota(I32, (C, 1), 0)
    row = lax.broadcasted_iota(I32, (C, C), 0)
    col = lax.broadcasted_iota(I32, (C, C), 1)
    a = [jnp.zeros((C, C), F32) for _ in chunks]
    for li, B in enumerate(HG_LEVELS):
        second = (t_idx % B) >= B // 2
        same_block = (row // B) == (col // B)
        for c in chunks:
            x = jnp.exp(e[c][(2 + li) * C:(3 + li) * C])
            qt = jnp.where(second, rows(q, c) * x, 0.0).astype(BF16)
            kt = jnp.where(second, 0.0, rows(kk, c) * x).astype(BF16)
            al = lax.dot_general(qt, kt, _NT, preferred_element_type=F32)
            a[c] = a[c] + (jnp.where(same_block, al, 0.0) if B < C else al)

    states = [s_t]
    for c in chunks:
        states.append(states[-1] * jnp.exp(b[c][C - 1:C, :]) + kv[c])
    o = [lax.dot_general(qi[c], states[c].astype(BF16), _NT, preferred_element_type=F32)
         + dot(a[c].astype(BF16), rows(v_bf, c)) for c in chunks]

    G = n * C // SUBLANES
    b_all = jnp.concatenate(b, axis=0) if n > 1 else b[0]
    q3, k3, v3, b3 = (x.reshape(G, SUBLANES, LANES) for x in (q, kk, v, b_all))
    p3 = lax.broadcasted_iota(I32, (G, SUBLANES, 1), 1)
    o3 = (jnp.concatenate(o, axis=0) if n > 1 else o[0]).reshape(G, SUBLANES, LANES)
    for j in range(SUBLANES):
        term = q3 * _group_rows(k3, j) * jnp.exp(jnp.minimum(b3 - _group_rows(b3, j), 0.0))
        aj = jnp.where(p3 >= j, jnp.sum(term, axis=-1, keepdims=True), 0.0)
        o3 = o3 + aj * _group_rows(v3, j)
    return o3.reshape(n * C, LANES), states[-1]


def _hgrn_kernel(q_ref, z_ref, v_ref, g_ref, lb_ref, hgn_ref, msk_ref, o_ref, st_ref, s_scr):
    i = pl.program_id(1)

    @pl.when(i == 0)
    def _():
        s_scr[...] = jnp.zeros(s_scr.shape, F32)

    o, s_t = _hgrn_tile(q_ref[...], z_ref[...], v_ref[...], _lower_bound(lb_ref), msk_ref[...], s_scr[...])
    s_scr[...] = s_t
    g = g_ref[...]
    o_ref[...] = (_rms(o, hgn_ref[...]) * (g * jax.nn.sigmoid(g))).astype(BF16)

    @pl.when(i == pl.num_programs(1) - 1)
    def _():
        st_ref[...] = s_scr[...].T


def _hgrn_prompt(hg, hgrn_lb, hgrn_norm, tt):
    S = hg.shape[0]
    msk = _hgrn_masks()
    col = lambda seg: pl.BlockSpec((tt, LANES), lambda h, i, seg=seg: (i, seg * HG_HEADS + h))
    o, st = pl.pallas_call(
        _hgrn_kernel, grid=(HG_HEADS, S // tt),
        in_specs=[col(0), col(1), col(2), col(3),
                  pl.BlockSpec((hgrn_lb.shape[0], LANES), lambda h, i: (0, h)),
                  _small_spec((1, HG_VAL)), _small_spec(tuple(msk.shape))],
        out_specs=[pl.BlockSpec((tt, LANES), lambda h, i: (i, h)),
                   pl.BlockSpec((None, HG_KEY, HG_VAL), lambda h, i: (h, 0, 0))],
        out_shape=[jax.ShapeDtypeStruct((S, HG_WIDTH), BF16),
                   jax.ShapeDtypeStruct((HG_HEADS, HG_KEY, HG_VAL), F32)],
        scratch_shapes=[pltpu.VMEM((HG_VAL, HG_KEY), F32)],
        compiler_params=_params(("parallel", "arbitrary")), name="hgrn_prompt",
    )(hg, hg, hg, hg, hgrn_lb, hgrn_norm, msk)
    return o, st


def _hgrn_decode_kernel(hg_ref, s_ref, lb_ref, hgn_ref, o_ref, sn_ref):
    W = HG_WIDTH
    row = hg_ref[...]
    q, z, v, g = (row[:, s * W:(s + 1) * W] for s in range(4))
    lb = _lower_bound(lb_ref)
    f = lb + (1.0 - lb) * jax.nn.sigmoid(z)
    kk = (1.0 - lb) * jax.nn.sigmoid(-z)
    rnd = lambda x: x.astype(BF16).astype(F32)
    qf = rnd(q * f)
    qk = rnd(q) * rnd(kk)
    pad = jnp.zeros((SUBLANES - 3, LANES), F32)
    outs = []
    for h in range(HG_HEADS):
        hs = slice(h * LANES, (h + 1) * LANES)
        cols = jnp.concatenate([f[:, hs], kk[:, hs], qf[:, hs], pad], axis=0).T
        f_c, k_c, qf_c = cols[:, 0:1], cols[:, 1:2], cols[:, 2:3]
        s0 = s_ref[h]
        vh = v[:, hs]
        sn_ref[h] = f_c * s0 + k_c * vh
        o = (jnp.sum(qf_c * rnd(s0), axis=0, keepdims=True)
             + rnd(jnp.sum(qk[:, hs], axis=1, keepdims=True)) * rnd(vh))
        gh = g[:, hs]
        outs.append(_rms(o, hgn_ref[...]) * (gh * jax.nn.sigmoid(gh)))
    o_ref[...] = jnp.concatenate(outs, axis=1).astype(BF16)


def _hgrn_decode(hg, state, hgrn_lb, hgrn_norm):
    Bd = hg.shape[0]
    o, sn = pl.pallas_call(
        _hgrn_decode_kernel, grid=(Bd,),
        in_specs=[pl.BlockSpec((None, 1, 4 * HG_WIDTH), lambda b: (b, 0, 0)),
                  pl.BlockSpec((None, HG_HEADS, HG_KEY, HG_VAL), lambda b: (b, 0, 0, 0)),
                  _small_spec(tuple(hgrn_lb.shape)), _small_spec((1, HG_VAL))],
        out_specs=[pl.BlockSpec((None, 1, HG_WIDTH), lambda b: (b, 0, 0)),
                   pl.BlockSpec((None, HG_HEADS, HG_KEY, HG_VAL), lambda b: (b, 0, 0, 0))],
        out_shape=[jax.ShapeDtypeStruct((Bd, 1, HG_WIDTH), BF16),
                   jax.ShapeDtypeStruct(state.shape, F32)],
        compiler_params=_params(("parallel",)), name="hgrn_decode",
    )(hg.reshape(Bd, 1, -1), state, hgrn_lb, hgrn_norm)
    return o.reshape(Bd, HG_WIDTH), sn


def _merge_kernel(oa_ref, oh_ref, wa_ref, wb_ref, ga_ref, gb_ref, o_ref):
    a = jnp.dot(oa_ref[...], wa_ref[...].astype(BF16), preferred_element_type=F32)
    b = jnp.dot(oh_ref[...], wb_ref[...].astype(BF16), preferred_element_type=F32)
    ga = jax.nn.sigmoid(ga_ref[...].astype(F32))
    gb = jax.nn.sigmoid(gb_ref[...].astype(F32))
    o_ref[...] = (ga * a + gb * b).astype(BF16)


def _merge(oa, oh, w_pa, w_pb, gates, tm, tn, tag):
    M = oa.shape[0]
    nj = D_MODEL // tn
    return pl.pallas_call(
        _merge_kernel, grid=(M // tm, nj),
        in_specs=[pl.BlockSpec((tm, ATT_WIDTH), lambda i, j: (i, 0)),
                  pl.BlockSpec((tm, HG_WIDTH), lambda i, j: (i, 0)),
                  pl.BlockSpec((None, ATT_WIDTH, tn), lambda i, j: (0, 0, j)),
                  pl.BlockSpec((None, HG_WIDTH, tn), lambda i, j: (0, 0, j)),
                  pl.BlockSpec((tm, tn), lambda i, j: (i, j)),
                  pl.BlockSpec((tm, tn), lambda i, j: (i, j + nj))],
        out_specs=pl.BlockSpec((tm, tn), lambda i, j: (i, j)),
        out_shape=jax.ShapeDtypeStruct((M, D_MODEL), BF16),
        compiler_params=_params(("parallel", "arbitrary")), name=f"merge_{tag}",
    )(oa, oh, w_pa, w_pb, gates, gates)


def _cross_prompt_kernel(x_ref, g_ref, wq_ref, mk_ref, mv_ref, wo_ref, o_ref):
    x = x_ref[...]
    u = _rms(x, g_ref[...]).astype(BF16)
    q = jnp.dot(u, wq_ref[...], preferred_element_type=F32).astype(BF16)
    heads = []
    for h in range(X_HEADS):
        hs = slice(h * X_HEAD_DIM, (h + 1) * X_HEAD_DIM)
        s = lax.dot_general(q[:, hs], mk_ref[:, hs], _NT, preferred_element_type=F32) * (X_HEAD_DIM ** -0.5)
        p = jnp.exp(s - jnp.max(s, axis=1, keepdims=True))
        p = p / jnp.sum(p, axis=1, keepdims=True)
        heads.append(jnp.dot(p.astype(BF16), mv_ref[:, hs], preferred_element_type=F32))
    o = jnp.concatenate(heads, axis=1).astype(BF16)
    o_ref[...] = x + jnp.dot(o, wo_ref[...], preferred_element_type=F32)


def _cross_prompt(h1, g_cross, w_cq_bf, mk_bf, mv_bf, w_co_bf, tm):
    M = h1.shape[0]
    return pl.pallas_call(
        _cross_prompt_kernel, grid=(M // tm,),
        in_specs=[pl.BlockSpec((tm, D_MODEL), lambda i: (i, 0)), _small_spec((1, D_MODEL)),
                  _small_spec((D_MODEL, X_WIDTH)), _small_spec(tuple(mk_bf.shape)),
                  _small_spec(tuple(mv_bf.shape)), _small_spec((X_WIDTH, D_MODEL))],
        out_specs=pl.BlockSpec((tm, D_MODEL), lambda i: (i, 0)),
        out_shape=jax.ShapeDtypeStruct((M, D_MODEL), F32),
        compiler_params=_params(("parallel",)), name="cross_prompt",
    )(h1, g_cross.reshape(1, -1), w_cq_bf, mk_bf, mv_bf, w_co_bf)


def _cross_decode_kernel(q_ref, mk_ref, mv_ref, o_ref):
    rnd = lambda x: x.astype(BF16).astype(F32)
    q = rnd(q_ref[...])
    outs = []
    for h in range(X_HEADS):
        hs = slice(h * X_HEAD_DIM, (h + 1) * X_HEAD_DIM)
        s = jnp.sum(rnd(mk_ref[:, hs]) * q[:, hs], axis=1, keepdims=True) * (X_HEAD_DIM ** -0.5)
        p = jnp.exp(s - jnp.max(s, axis=0, keepdims=True))
        p = p / jnp.sum(p, axis=0, keepdims=True)
        outs.append(jnp.sum(rnd(p) * rnd(mv_ref[:, hs]), axis=0, keepdims=True))
    o_ref[...] = jnp.concatenate(outs, axis=1).astype(BF16)


def _cross_decode(q, mem_k, mem_v):
    Bd, mem = mem_k.shape[0], mem_k.shape[1]
    row = pl.BlockSpec((None, 1, X_WIDTH), lambda b: (b, 0, 0))
    mem_spec = pl.BlockSpec((None, mem, X_WIDTH), lambda b: (b, 0, 0))
    out = pl.pallas_call(
        _cross_decode_kernel, grid=(Bd,), in_specs=[row, mem_spec, mem_spec], out_specs=row,
        out_shape=jax.ShapeDtypeStruct((Bd, 1, X_WIDTH), BF16),
        compiler_params=_params(("parallel",)), name="cross_decode",
    )(q.reshape(Bd, 1, X_WIDTH), mem_k.reshape(Bd, mem, X_WIDTH), mem_v.reshape(Bd, mem, X_WIDTH))
    return out.reshape(Bd, X_WIDTH)


def _route_kernel(x_ref, g_ref, w_ref, b_ref, cin_ref, uin_ref, u_ref, eid_ref, wt_ref, rank_ref, cnt_ref,
                  c_scr, *, n_rows):
    i = pl.program_id(0)
    tm = x_ref.shape[0]

    @pl.when(i == 0)
    def _():
        c_scr[...] = cin_ref[...]

    u = _rms(x_ref[...], g_ref[...])
    u_ref[...] = u
    dot = lambda a, b: jnp.dot(a, b, preferred_element_type=F32)
    logits = dot(u.astype(BF16), w_ref[...]) + b_ref[...]

    lane = lax.broadcasted_iota(I32, (tm, LANES), 1)
    big = jnp.int32(1 << 20)
    first_max = lambda vals, vmax: jnp.min(jnp.where(vals == vmax, lane, big), axis=1, keepdims=True)
    glv = jnp.where(lane < N_GROUPS, logits, -jnp.inf)
    gmax = jnp.max(glv, axis=1, keepdims=True)
    gsel = first_max(glv, gmax)
    p_group = 1.0 / jnp.sum(jnp.exp(glv - gmax), axis=1, keepdims=True)
    e_lane = lane - N_GROUPS
    in_group = (e_lane >= 0) & (e_lane < N_EXPERTS) & (e_lane // EXPERTS_PER_GROUP == gsel)
    ev = jnp.where(in_group, logits, -jnp.inf)
    v1 = jnp.max(ev, axis=1, keepdims=True)
    i1 = first_max(ev, v1)
    ev2 = jnp.where(lane == i1, -jnp.inf, ev)
    v2 = jnp.max(ev2, axis=1, keepdims=True)
    i2 = first_max(ev2, v2)
    t2 = jnp.exp(v2 - v1)
    w1 = p_group / (1.0 + t2)
    w2 = w1 * t2
    e1 = i1 - N_GROUPS
    e2 = i2 - N_GROUPS
    eid_ref[...] = jnp.where(lane == 0, e1, jnp.where(lane == 1, e2, 0))
    wt_ref[...] = jnp.where(lane == 0, w1, jnp.where(lane == 1, w2, 0.0))

    valid = (lax.broadcasted_iota(I32, (tm, 1), 0) + i * tm) < n_rows
    oh1 = jnp.where((lane == e1) & valid, 1.0, 0.0)
    oh2 = jnp.where((lane == e2) & valid, 1.0, 0.0)
    cnt = (oh1 + oh2).astype(BF16)
    r_i = lax.broadcasted_iota(I32, (tm, tm), 0)
    c_i = lax.broadcasted_iota(I32, (tm, tm), 1)
    strict = jnp.where(c_i < r_i, 1.0, 0.0).astype(BF16)
    before = dot(strict, cnt) + c_scr[...]
    r1 = jnp.sum(oh1 * before, axis=1, keepdims=True)
    r2 = jnp.sum(oh2 * before, axis=1, keepdims=True)
    rank_ref[...] = jnp.where(lane == 0, r1, jnp.where(lane == 1, r2, 0.0)).astype(I32)
    c_scr[...] = c_scr[...] + jnp.sum(oh1 + oh2, axis=0, keepdims=True)
    cnt_ref[...] = c_scr[...]


def _route(x, g_ffn, w_router, b_router, counts_in, u_all, row_off, tm, tag):
    M = x.shape[0]
    T = u_all.shape[0]
    boff = row_off // tm
    kern = functools.partial(_route_kernel, n_rows=M)
    lane_out = lambda dt: jax.ShapeDtypeStruct((M, LANES), dt)
    outs = pl.pallas_call(
        kern, grid=(M // tm,),
        in_specs=[pl.BlockSpec((tm, D_MODEL), lambda i: (i, 0)), _small_spec((1, D_MODEL)),
                  _small_spec((D_MODEL, LANES)), _small_spec((1, LANES)), _small_spec((1, LANES)),
                  pl.BlockSpec(memory_space=pl.ANY)],
        out_specs=[pl.BlockSpec((tm, D_MODEL), lambda i: (i + boff, 0)),
                   pl.BlockSpec((tm, LANES), lambda i: (i, 0)), pl.BlockSpec((tm, LANES), lambda i: (i, 0)),
                   pl.BlockSpec((tm, LANES), lambda i: (i, 0)), _small_spec((1, LANES))],
        out_shape=[jax.ShapeDtypeStruct((T, D_MODEL), F32), lane_out(I32), lane_out(F32), lane_out(I32),
                   jax.ShapeDtypeStruct((1, LANES), F32)],
        scratch_shapes=[pltpu.VMEM((1, LANES), F32)],
        input_output_aliases={5: 0},
        compiler_params=_params(("arbitrary",)), name=f"moe_route_{tag}",
    )(x, g_ffn.reshape(1, -1), w_router, b_router, counts_in, u_all)
    return outs


def _plan_kernel(cnt_ref, pstart_ref, blkexp_ref, blkslot_ref, nxtexp_ref, nblk_ref, *, n_blocks):
    def per_expert(e, carry):
        acc, k = carry
        pstart_ref[e] = acc
        nb = (cnt_ref[e] + MOE_ROWS - 1) // MOE_ROWS
        first = acc // MOE_ROWS

        def fill(b, c):
            blkexp_ref[b] = e
            blkslot_ref[b] = k % 2
            return c

        lax.fori_loop(first, first + nb, fill, 0)
        return acc + nb * MOE_ROWS, k + jnp.where(nb > 0, 1, 0)

    total, _ = lax.fori_loop(0, N_EXPERTS, per_expert, (jnp.int32(0), jnp.int32(0)))
    used = total // MOE_ROWS
    nblk_ref[0] = used
    last = jnp.maximum(used - 1, 0)

    def backwards(t, carry):
        cur, nxt = carry
        b = last - t
        e = blkexp_ref[b]
        nxt = jnp.where(e != cur, cur, nxt)
        nxtexp_ref[b] = nxt
        return e, nxt

    lax.fori_loop(0, used, backwards, (blkexp_ref[last], jnp.int32(-1)))

    def tail(b, c):
        blkexp_ref[b] = blkexp_ref[last]
        blkslot_ref[b] = blkslot_ref[last]
        nxtexp_ref[b] = -1
        return c

    lax.fori_loop(used, n_blocks, tail, 0)


def _plan(counts, n_blocks):
    smem = pl.BlockSpec(memory_space=pltpu.SMEM)
    per_block = jax.ShapeDtypeStruct((n_blocks,), I32)
    return pl.pallas_call(
        functools.partial(_plan_kernel, n_blocks=n_blocks),
        in_specs=[smem], out_specs=[smem] * 5,
        out_shape=[jax.ShapeDtypeStruct((N_EXPERTS,), I32), per_block, per_block, per_block,
                   jax.ShapeDtypeStruct((1,), I32)],
        name="moe_plan",
    )(counts)


def _dispatch_kernel(eid_ref, rank_ref, pstart_ref, cnt_ref, nblk_ref, u_ref, xs_hbm, zbuf, sem_z, sem,
                     *, n_tokens, n_blocks):
    R = MOE_ROWS
    i = pl.program_id(0)
    tm = u_ref.shape[0]

    def zero_copy(b):
        return pltpu.make_async_copy(zbuf, xs_hbm.at[pl.ds(pl.multiple_of(b * R, R), R), :], sem_z)

    def partial_block(e):
        c = cnt_ref[e]
        return (c % R) != 0, (pstart_ref[e] + c) // R

    def zero_partial(start):
        def body(e, n):
            has, b = partial_block(e)

            @pl.when(has)
            def _():
                zero_copy(b).start() if start else zero_copy(b).wait()

            return n
        return body

    def zero_unused(start):
        def body(b, n):
            zero_copy(b).start() if start else zero_copy(b).wait()
            return n
        return body

    @pl.when(i == 0)
    def _():
        zbuf[...] = jnp.zeros(zbuf.shape, F32)
        for start in (True, False):
            lax.fori_loop(0, N_EXPERTS, zero_partial(start), 0)
            lax.fori_loop(nblk_ref[0], n_blocks, zero_unused(start), 0)

    def row_copy(r, k):
        a = 2 * (i * tm + r) + k
        d = pstart_ref[eid_ref[a]] + rank_ref[a]
        return pltpu.make_async_copy(u_ref.at[pl.ds(r, 1), :], xs_hbm.at[pl.ds(d, 1), :], sem)

    def start_row(r, n):
        row_copy(r, 0).start()
        row_copy(r, 1).start()
        return n

    def wait_row(r, n):
        row_copy(r, 0).wait()
        row_copy(r, 1).wait()
        return n

    rem = n_tokens % tm

    @pl.when((i + 1) * tm <= n_tokens)
    def _():
        lax.fori_loop(0, tm, start_row, 0, unroll=8)
        lax.fori_loop(0, tm, wait_row, 0, unroll=8)

    if rem:
        @pl.when((i + 1) * tm > n_tokens)
        def _():
            lax.fori_loop(0, rem, start_row, 0, unroll=8)
            lax.fori_loop(0, rem, wait_row, 0, unroll=8)


def _dispatch(eid, rank, pstart, counts, n_used, u_all, n_blocks, tm):
    T = u_all.shape[0]
    grid_spec = pltpu.PrefetchScalarGridSpec(
        num_scalar_prefetch=5, grid=(pl.cdiv(T, tm),),
        in_specs=[pl.BlockSpec((tm, D_MODEL), lambda i, *_: (i, 0))],
        out_specs=pl.BlockSpec(memory_space=pl.ANY),
        scratch_shapes=[pltpu.VMEM((MOE_ROWS, D_MODEL), F32), pltpu.SemaphoreType.DMA(()),
                        pltpu.SemaphoreType.DMA(())])
    return pl.pallas_call(
        functools.partial(_dispatch_kernel, n_tokens=T, n_blocks=n_blocks), grid_spec=grid_spec,
        out_shape=jax.ShapeDtypeStruct((n_blocks * MOE_ROWS, D_MODEL), F32),
        compiler_params=_params(("arbitrary",)), name="moe_dispatch",
    )(eid, rank, pstart, counts, n_used, u_all)


def _expert_kernel(blkexp_ref, blkslot_ref, nxtexp_ref, nblk_ref, x_ref, wg_hbm, wu_hbm, wd_hbm, o_ref,
                   wg_f, wu_f, wd_f, wg_s, wu_s, wd_s, sem):
    i = pl.program_id(0)

    def fetch(e, slot):
        return [pltpu.make_async_copy(src.at[0, e], dst.at[slot], sem.at[slot])
                for src, dst in ((wg_hbm, wg_f), (wu_hbm, wu_f), (wd_hbm, wd_f))]

    @pl.when(i < nblk_ref[0])
    def _():
        e = blkexp_ref[i]
        slot = blkslot_ref[i]
        nxt = nxtexp_ref[i]

        @pl.when(i == 0)
        def _():
            for cp in fetch(e, slot):
                cp.start()

        @pl.when((i == 0) | (e != blkexp_ref[jnp.maximum(i - 1, 0)]))
        def _():
            for cp in fetch(e, slot):
                cp.wait()

            @pl.when(nxt >= 0)
            def _():
                for cp in fetch(nxt, 1 - slot):
                    cp.start()

            wg_s[...] = wg_f[slot].astype(BF16)
            wu_s[...] = wu_f[slot].astype(BF16)
            wd_s[...] = wd_f[slot].astype(BF16)

        x = x_ref[...].astype(BF16)
        hg = jnp.dot(x, wg_s[...], preferred_element_type=F32)
        hu = jnp.dot(x, wu_s[...], preferred_element_type=F32)
        h = (hg * jax.nn.sigmoid(hg) * hu).astype(BF16)
        o_ref[...] = jnp.dot(h, wd_s[...], preferred_element_type=F32)

    @pl.when(i >= nblk_ref[0])
    def _():
        o_ref[...] = jnp.zeros(o_ref.shape, F32)


def _experts(xs, blk_exp, blk_slot, nxt_exp, n_used, we_g, we_u, we_d, n_blocks):
    hbm = pl.BlockSpec(memory_space=pl.ANY)
    up, down = (D_MODEL, EXPERT_FF), (EXPERT_FF, D_MODEL)
    grid_spec = pltpu.PrefetchScalarGridSpec(
        num_scalar_prefetch=4, grid=(n_blocks,),
        in_specs=[pl.BlockSpec((MOE_ROWS, D_MODEL),
                               lambda i, be, bs, nx, nb: (jnp.maximum(jnp.minimum(i, nb[0] - 1), 0), 0)),
                  hbm, hbm, hbm],
        out_specs=pl.BlockSpec((MOE_ROWS, D_MODEL), lambda i, *_: (i, 0)),
        scratch_shapes=[pltpu.VMEM((2,) + up, F32), pltpu.VMEM((2,) + up, F32), pltpu.VMEM((2,) + down, F32),
                        pltpu.VMEM(up, BF16), pltpu.VMEM(up, BF16), pltpu.VMEM(down, BF16),
                        pltpu.SemaphoreType.DMA((2,))])
    return pl.pallas_call(
        _expert_kernel, grid_spec=grid_spec,
        out_shape=jax.ShapeDtypeStruct((n_blocks * MOE_ROWS, D_MODEL), F32),
        compiler_params=_params(("arbitrary",)), name="moe_experts",
    )(blk_exp, blk_slot, nxt_exp, n_used, xs, we_g, we_u, we_d)


def _combine_kernel(eid_ref, rank_ref, pstart_ref, h_ref, wt_ref, gf_ref, yb_hbm, o_ref, ybuf0, ybuf1, sem,
                    *, dest_off):
    i = pl.program_id(0)
    tm = h_ref.shape[0]
    base = dest_off + i * (2 * tm)

    def row_copy(r, k, buf):
        a = base + 2 * r + k
        d = pstart_ref[eid_ref[a]] + rank_ref[a]
        return pltpu.make_async_copy(yb_hbm.at[pl.ds(d, 1), :], buf.at[pl.ds(r, 1), :], sem)

    def start(r, c):
        row_copy(r, 0, ybuf0).start()
        row_copy(r, 1, ybuf1).start()
        return c

    def wait(r, c):
        row_copy(r, 0, ybuf0).wait()
        row_copy(r, 1, ybuf1).wait()
        return c

    lax.fori_loop(0, tm, start, 0, unroll=4)
    lax.fori_loop(0, tm, wait, 0, unroll=4)
    wt = wt_ref[...]
    h = h_ref[...] + (wt[:, 0:1] * ybuf0[...] + wt[:, 1:2] * ybuf1[...])
    o_ref[...] = _rms(h, gf_ref[...])


def _combine(h2, wts, eid, rank, pstart, yb, norm_final, dest_off, tm, tag):
    M = h2.shape[0]
    grid_spec = pltpu.PrefetchScalarGridSpec(
        num_scalar_prefetch=3, grid=(M // tm,),
        in_specs=[pl.BlockSpec((tm, D_MODEL), lambda i, *_: (i, 0)), pl.BlockSpec((tm, LANES), lambda i, *_: (i, 0)),
                  pl.BlockSpec((1, D_MODEL), lambda i, *_: (0, 0)), pl.BlockSpec(memory_space=pl.ANY)],
        out_specs=pl.BlockSpec((tm, D_MODEL), lambda i, *_: (i, 0)),
        scratch_shapes=[pltpu.VMEM((tm, D_MODEL), F32), pltpu.VMEM((tm, D_MODEL), F32),
                        pltpu.SemaphoreType.DMA(())])
    return pl.pallas_call(
        functools.partial(_combine_kernel, dest_off=dest_off), grid_spec=grid_spec,
        out_shape=jax.ShapeDtypeStruct((M, D_MODEL), F32),
        compiler_params=_params(("arbitrary",)), name=f"moe_combine_{tag}",
    )(eid, rank, pstart, h2, wts, norm_final.reshape(1, -1), yb)


def _moe_and_final_norm(h2_p, h2_s, g_ffn, wr_g, br_g, wr_e, br_e, we_g, we_u, we_d, norm_final, tm_p):
    Tp, Ts = h2_p.shape[0], h2_s.shape[0]
    T = Tp + Ts
    pad = LANES - N_GROUPS - N_EXPERTS
    w_router = jnp.concatenate([wr_g, wr_e, jnp.zeros((D_MODEL, pad), F32)], axis=1).astype(BF16)
    b_router = jnp.concatenate([br_g, br_e, jnp.zeros((pad,), F32)]).reshape(1, LANES)
    u_all = jnp.zeros((T, D_MODEL), F32)
    zero_counts = jnp.zeros((1, LANES), F32)
    u_all, eid_p, wt_p, rank_p, counts = _route(h2_p, g_ffn, w_router, b_router, zero_counts, u_all, 0, tm_p, "p")
    u_all, eid_s, wt_s, rank_s, counts = _route(h2_s, g_ffn, w_router, b_router, counts, u_all, Tp, Ts, "s")
    eid = jnp.concatenate([eid_p[:, :2].reshape(-1), eid_s[:, :2].reshape(-1)])
    rank = jnp.concatenate([rank_p[:, :2].reshape(-1), rank_s[:, :2].reshape(-1)])
    A = 2 * T
    n_blocks = (A + N_EXPERTS * (MOE_ROWS - 1)) // MOE_ROWS + 1
    cnt = counts[0, :N_EXPERTS].astype(I32)
    pstart, blk_exp, blk_slot, nxt_exp, n_used = _plan(cnt, n_blocks)
    xs = _dispatch(eid, rank, pstart, cnt, n_used, u_all, n_blocks, 256)
    yb = _experts(xs, blk_exp, blk_slot, nxt_exp, n_used, we_g, we_u, we_d, n_blocks)
    y_p = _combine(h2_p, wt_p, eid, rank, pstart, yb, norm_final, 0, 256, "p")
    y_s = _combine(h2_s, wt_s, eid, rank, pstart, yb, norm_final, 2 * Tp, Ts, "s")
    return y_p, y_s


def kernel(x_prompt, x_sample, mem_prompt, cache_k, cache_v, cache_mem_k, cache_mem_v, state_hgrn, page_table,
           norm_mix, w_in, lambda_q1, lambda_k1, lambda_q2, lambda_k2, subln, hgrn_lb, hgrn_norm, w_pa, w_pb,
           w_out, norm_cross, w_cq, w_ck, w_cv, w_co, norm_ffn, w_router_group, b_router_group,
           w_router_expert, b_router_expert, w_e_gate, w_e_up, w_e_down, norm_final):
    assert w_in.shape[0] == 1, "single-layer step"
    Bp, S, D = x_prompt.shape
    Bd, Ld, _ = x_sample.shape
    assert Bp == 1 and Ld == 1
    n_pages = page_table.shape[1]
    page = cache_k.shape[2]
    past_len = n_pages * page
    xp = x_prompt.reshape(S, D)
    xs = x_sample.reshape(Bd, D)
    lam = (lambda_q1, lambda_k1, lambda_q2, lambda_k2)
    TM = 1024
    w_in, w_pa, w_pb, w_out = (w.astype(BF16) for w in (w_in, w_pa, w_pb, w_out))

    q, k, k_bf, v, v_bf, hg, gates = _in_projection(xp, norm_mix[0], w_in, jnp.arange(S, dtype=I32), TM, 512, "p",
                                                    BF16)
    oa = _attention_prompt(q, k_bf, v_bf, *lam, subln, ATTN_TQ, ATTN_TK)
    oh, st_p = _hgrn_prompt(hg, hgrn_lb, hgrn_norm, 512)
    merged = _merge(oa, oh, w_pa, w_pb, gates, TM, 512, "p")
    (h1_p,) = _mm(merged, w_out, col_off=0, n_cols=D, tm=TM, tn=512, epilogue=_epi_residual, out_dtypes=[F32],
                  extras=[(xp, pl.BlockSpec((TM, 512), lambda i, j: (i, j)))], name="outproj_p")

    pos_s = jnp.full((Bd,), past_len, I32)
    qs, ks, ks_bf, vs, vs_bf, hgs, gates_s = _in_projection(xs, norm_mix[0], w_in, pos_s, Bd, 512, "s", F32)
    ck = jnp.transpose(cache_k[0], (0, 2, 3, 4, 1)).reshape(cache_k.shape[1], ATT_WIDTH, page)
    cv = cache_v[0].reshape(cache_v.shape[1], page * ATT_HEADS, 2 * ATT_HEAD_DIM)
    oa_s = _attention_decode(qs, ks_bf, vs_bf, ck, cv, page_table, *lam, subln)
    oh_s, st_s = _hgrn_decode(hgs, state_hgrn[0], hgrn_lb, hgrn_norm)
    merged_s = _merge(oa_s, oh_s, w_pa, w_pb, gates_s, Bd, 512, "s")
    (h1_s,) = _mm(merged_s, w_out, col_off=0, n_cols=D, tm=Bd, tn=512, epilogue=_epi_residual, out_dtypes=[F32],
                  extras=[(xs, pl.BlockSpec((Bd, 512), lambda i, j: (i, j)))], name="outproj_s")

    mem = mem_prompt.reshape(-1, D)
    mk, mk_bf = _mm(mem, w_ck, col_off=0, n_cols=X_WIDTH, tm=mem.shape[0], tn=X_WIDTH, epilogue=_epi_v,
                    out_dtypes=[F32, BF16], name="mem_k")
    mv, mv_bf = _mm(mem, w_cv, col_off=0, n_cols=X_WIDTH, tm=mem.shape[0], tn=X_WIDTH, epilogue=_epi_v,
                    out_dtypes=[F32, BF16], name="mem_v")
    w_cq_bf = w_cq[0].astype(BF16)
    w_co_bf = w_co[0].astype(BF16)
    h2_p = _cross_prompt(h1_p, norm_cross[0], w_cq_bf, mk_bf, mv_bf, w_co_bf, 256)
    (qc_s,) = _mm(h1_s, w_cq, col_off=0, n_cols=X_WIDTH, tm=Bd, tn=X_WIDTH, epilogue=_epi_plain,
                  out_dtypes=[F32], norm_g=norm_cross[0], name="cross_q_s")
    oc_s = _cross_decode(qc_s, cache_mem_k[0], cache_mem_v[0])
    (h2_s,) = _mm(oc_s, w_co, col_off=0, n_cols=D, tm=Bd, tn=512, epilogue=_epi_residual, out_dtypes=[F32],
                  extras=[(h1_s, pl.BlockSpec((Bd, 512), lambda i, j: (i, j)))], name="cross_o_s")

    y_p, y_s = _moe_and_final_norm(h2_p, h2_s, norm_ffn[0], w_router_group[0], b_router_group[0],
                                   w_router_expert[0], b_router_expert[0], w_e_gate, w_e_up, w_e_down,
                                   norm_final, 256)

    return (y_p.reshape(Bp, S, D), y_s.reshape(Bd, Ld, D),
            k.reshape(1, Bp, S, ATT_HEADS, 2, ATT_HEAD_DIM), v.reshape(1, Bp, S, ATT_HEADS, 2 * ATT_HEAD_DIM),
            ks.reshape(1, Bd, Ld, ATT_HEADS, 2, ATT_HEAD_DIM), vs.reshape(1, Bd, Ld, ATT_HEADS, 2 * ATT_HEAD_DIM),
            st_p.reshape(1, Bp, HG_HEADS, HG_KEY, HG_VAL), st_s.reshape(1, Bd, HG_HEADS, HG_KEY, HG_VAL),
            mk.reshape(1, Bp, -1, X_HEADS, X_HEAD_DIM), mv.reshape(1, Bp, -1, X_HEADS, X_HEAD_DIM))
```

```python
import functools
import math

import numpy as np
import jax
import jax.numpy as jnp
from jax import lax
from jax.experimental import pallas as pl
from jax.experimental.pallas import tpu as pltpu

F32 = jnp.float32
BF16 = jnp.bfloat16
I32 = jnp.int32

D_MODEL = 2048
ATT_HEADS = 8
ATT_HEAD_DIM = 64
ATT_WIDTH = ATT_HEADS * 2 * ATT_HEAD_DIM
ROPE_THETA = 10000.0
HG_HEADS = 8
HG_KEY = 128
HG_VAL = 128
HG_WIDTH = HG_HEADS * HG_VAL
X_HEADS = 4
X_HEAD_DIM = 128
X_WIDTH = X_HEADS * X_HEAD_DIM
N_GROUPS = 4
EXPERTS_PER_GROUP = 8
N_EXPERTS = N_GROUPS * EXPERTS_PER_GROUP
EXPERT_FF = 512
RMS_EPS = 1e-6
NEG_INF = -1e30
LAMBDA_INIT = 0.8 - 0.6 * math.exp(-0.3 * 0)

LANES = 128
SUBLANES = 8
VMEM_LIMIT = 52 * 1024 * 1024

HG_CHUNK = 64
HG_LEVELS = (16, 32, 64)
MOE_ROWS = 256
PAGES_PER_STEP = 16
ATTN_TQ = 1024
ATTN_TK = 1024

_NT = (((1,), (1,)), ((), ()))
_TN = (((0,), (0,)), ((), ()))


def _params(sem):
    return pltpu.CompilerParams(dimension_semantics=sem, vmem_limit_bytes=VMEM_LIMIT)


def _rms(x, g):
    return x * lax.rsqrt(jnp.mean(x * x, axis=-1, keepdims=True) + RMS_EPS) * g


def _mm_kernel(*refs, n_extra, n_out, norm, epilogue):
    x_ref = refs[0]
    pos = 1
    if norm:
        g_ref = refs[1]
        pos = 2
    w_ref = refs[pos]
    extras = refs[pos + 1:pos + 1 + n_extra]
    outs = refs[pos + 1 + n_extra:pos + 1 + n_extra + n_out]
    if norm:
        u_ref = refs[-1]

        @pl.when(pl.program_id(1) == 0)
        def _():
            u_ref[...] = _rms(x_ref[...].astype(F32), g_ref[...]).astype(BF16)

        u = u_ref[...]
    else:
        u = x_ref[...].astype(BF16)
    acc = jnp.dot(u, w_ref[...].astype(BF16), preferred_element_type=F32)
    epilogue(acc, extras, outs)


def _mm(x, w, *, col_off, n_cols, tm, tn, epilogue, out_dtypes, norm_g=None, extras=(), name):
    M, K = x.shape
    assert M % tm == 0 and n_cols % tn == 0 and col_off % tn == 0
    joff = col_off // tn
    if w.ndim == 3:
        w_spec = pl.BlockSpec((None, K, tn), lambda i, j: (0, 0, j + joff))
    else:
        w_spec = pl.BlockSpec((K, tn), lambda i, j: (0, j + joff))
    in_specs = [pl.BlockSpec((tm, K), lambda i, j: (i, 0))]
    args = [x]
    if norm_g is not None:
        in_specs.append(pl.BlockSpec((1, K), lambda i, j: (0, 0)))
        args.append(norm_g.reshape(1, K))
    in_specs.append(w_spec)
    args.append(w)
    for arr, spec in extras:
        in_specs.append(spec)
        args.append(arr)
    out_specs = [pl.BlockSpec((tm, tn), lambda i, j: (i, j)) for _ in out_dtypes]
    out_shape = [jax.ShapeDtypeStruct((M, n_cols), dt) for dt in out_dtypes]
    scratch = [pltpu.VMEM((tm, K), BF16)] if norm_g is not None else []
    kern = functools.partial(_mm_kernel, n_extra=len(extras), n_out=len(out_dtypes),
                             norm=norm_g is not None, epilogue=epilogue)
    res = pl.pallas_call(
        kern, grid=(M // tm, n_cols // tn), in_specs=in_specs, out_specs=out_specs,
        out_shape=out_shape, scratch_shapes=scratch,
        compiler_params=_params(("parallel", "arbitrary")), name=name)(*args)
    return res


def _rope_tile(x, cos, sin_signed):
    first = (lax.broadcasted_iota(I32, (x.shape[0], LANES), 1) % ATT_HEAD_DIM) < ATT_HEAD_DIM // 2
    outs = []
    for c in range(x.shape[1] // LANES):
        xc = x[:, c * LANES:(c + 1) * LANES]
        rot = jnp.where(first, pltpu.roll(xc, LANES - ATT_HEAD_DIM // 2, 1),
                        pltpu.roll(xc, ATT_HEAD_DIM // 2, 1))
        outs.append(xc * cos + rot * sin_signed)
    return outs[0] if len(outs) == 1 else jnp.concatenate(outs, axis=1)


def _epi_q(acc, extras, outs):
    cos_ref, sin_ref = extras
    outs[0][...] = (_rope_tile(acc, cos_ref[...], sin_ref[...]) * (ATT_HEAD_DIM ** -0.5)).astype(BF16)


def _epi_k(acc, extras, outs):
    cos_ref, sin_ref = extras
    r = _rope_tile(acc, cos_ref[...], sin_ref[...])
    outs[0][...] = r
    outs[1][...] = r.astype(BF16)


def _epi_v(acc, extras, outs):
    outs[0][...] = acc
    outs[1][...] = acc.astype(BF16)


def _epi_plain(acc, extras, outs):
    outs[0][...] = acc.astype(outs[0].dtype)


def _epi_residual(acc, extras, outs):
    outs[0][...] = extras[0][...] + acc


def _rope_tables(pos):
    half = ATT_HEAD_DIM // 2
    freqs = ROPE_THETA ** (-jnp.arange(half, dtype=F32) / half)
    ang = pos.astype(F32)[:, None] * freqs[None, :]
    c, s = jnp.cos(ang), jnp.sin(ang)
    return jnp.tile(c, (1, 4)), jnp.concatenate([-s, s, -s, s], axis=1)


def _inproj_kernel(x_ref, g_ref, w_ref, cos_ref, sin_ref, q_ref, k_ref, kb_ref, v_ref, vb_ref, hg_ref, gt_ref,
                   u_ref, *, bounds):
    j = pl.program_id(1)

    @pl.when(j == 0)
    def _():
        u_ref[...] = _rms(x_ref[...].astype(F32), g_ref[...]).astype(BF16)

    acc = jnp.dot(u_ref[...], w_ref[...], preferred_element_type=F32)
    in_seg = lambda s: (j >= bounds[s]) & (j < bounds[s + 1])

    @pl.when(in_seg(0))
    def _():
        _epi_q(acc, (cos_ref, sin_ref), (q_ref,))

    @pl.when(in_seg(1))
    def _():
        _epi_k(acc, (cos_ref, sin_ref), (k_ref, kb_ref))

    @pl.when(in_seg(2))
    def _():
        _epi_v(acc, (), (v_ref, vb_ref))

    @pl.when(in_seg(3))
    def _():
        hg_ref[...] = acc

    @pl.when(in_seg(4))
    def _():
        gt_ref[...] = acc.astype(gt_ref.dtype)


def _in_projection(x, g, w_in_bf, pos, tm, tn, tag, gate_dtype):
    M, K = x.shape
    cos, sin = _rope_tables(pos)
    widths = (ATT_WIDTH, ATT_WIDTH, ATT_WIDTH, 4 * HG_WIDTH, 2 * D_MODEL)
    bounds = tuple(int(b) for b in np.cumsum((0,) + widths) // tn)

    def seg_spec(s):
        lo, n = bounds[s], bounds[s + 1] - bounds[s]
        return pl.BlockSpec((tm, tn), lambda i, j: (i, jnp.clip(j - lo, 0, n - 1)))

    seg_of_out = (0, 1, 1, 2, 2, 3, 4)
    out_dtypes = (BF16, F32, BF16, F32, BF16, F32, gate_dtype)
    row_tbl = pl.BlockSpec((tm, LANES), lambda i, j: (i, 0))
    return pl.pallas_call(
        functools.partial(_inproj_kernel, bounds=bounds),
        grid=(M // tm, bounds[-1]),
        in_specs=[pl.BlockSpec((tm, K), lambda i, j: (i, 0), pipeline_mode=pl.Buffered(1)),
                  pl.BlockSpec((1, K), lambda i, j: (0, 0)),
                  pl.BlockSpec((None, K, tn), lambda i, j: (0, 0, j)), row_tbl, row_tbl],
        out_specs=[seg_spec(s) for s in seg_of_out],
        out_shape=[jax.ShapeDtypeStruct((M, widths[s]), dt) for s, dt in zip(seg_of_out, out_dtypes)],
        scratch_shapes=[pltpu.VMEM((tm, K), BF16)],
        compiler_params=_params(("parallel", "arbitrary")), name=f"inproj_{tag}",
    )(x, g.reshape(1, K), w_in_bf, cos, sin)


def _lambda_value(lq1, lk1, lq2, lk2):
    return (jnp.exp(jnp.sum(lq1[...] * lk1[...], axis=-1, keepdims=True))
            - jnp.exp(jnp.sum(lq2[...] * lk2[...], axis=-1, keepdims=True)) + LAMBDA_INIT)


def _attn_kernel(q_ref, k_ref, v_ref, lq1, lk1, lq2, lk2, subln_ref, o_ref, *, tq, tk):
    i = pl.program_id(1)
    q = q_ref[...]
    lane = lax.broadcasted_iota(I32, q.shape, 1)
    zero = jnp.zeros_like(q)
    qm = (jnp.where(lane < ATT_HEAD_DIM, q, zero), jnp.where(lane >= ATT_HEAD_DIM, q, zero))

    def update(off, carry, mask):
        off = pl.multiple_of(off, tk)
        kt = k_ref[pl.ds(off, tk), :]
        vt = v_ref[pl.ds(off, tk), :]
        new = []
        for m in range(2):
            mx, l, acc = carry[3 * m:3 * m + 3]
            s = lax.dot_general(qm[m], kt, _NT, preferred_element_type=F32)
            if mask is not None:
                s = jnp.where(mask, s, NEG_INF)
            mn = jnp.maximum(mx, jnp.max(s, axis=1, keepdims=True))
            alpha = jnp.exp(mx - mn)
            p = jnp.exp(s - mn)
            l = alpha * l + jnp.sum(p, axis=1, keepdims=True)
            acc = alpha * acc + jnp.dot(p.astype(BF16), vt, preferred_element_type=F32)
            new += [mn, l, acc]
        return tuple(new)

    def body(j, carry):
        return update(j * tk, carry, None)

    init = (jnp.full((tq, 1), NEG_INF, F32), jnp.zeros((tq, 1), F32), jnp.zeros((tq, LANES), F32)) * 2
    carry = lax.fori_loop(0, i * (tq // tk), body, init)
    row = lax.broadcasted_iota(I32, (tq, tk), 0)
    col = lax.broadcasted_iota(I32, (tq, tk), 1)
    for d in range(tq // tk):
        carry = update(i * tq + d * tk, carry, col + d * tk <= row)
    _, l0, a0, _, l1, a1 = carry

    lam = _lambda_value(lq1, lk1, lq2, lk2)
    o = a0 / l0 - lam * (a1 / l1)
    o_ref[...] = (_rms(o, subln_ref[...]) * (1.0 - LAMBDA_INIT)).astype(BF16)


def _small_spec(shape):
    nd = len(shape)
    return pl.BlockSpec(shape, lambda *_: (0,) * nd)


def _attention_prompt(q, k_bf, v_bf, lq1, lk1, lq2, lk2, subln, tq, tk):
    S = q.shape[0]
    assert tq % tk == 0
    lam_specs = [_small_spec((1, ATT_HEAD_DIM))] * 4
    return pl.pallas_call(
        functools.partial(_attn_kernel, tq=tq, tk=tk),
        grid=(ATT_HEADS, S // tq),
        in_specs=[pl.BlockSpec((tq, LANES), lambda h, i: (i, h)),
                  pl.BlockSpec((S, LANES), lambda h, i: (0, h)),
                  pl.BlockSpec((S, LANES), lambda h, i: (0, h))] + lam_specs
                 + [_small_spec((1, LANES))],
        out_specs=pl.BlockSpec((tq, LANES), lambda h, i: (i, h)),
        out_shape=jax.ShapeDtypeStruct((S, ATT_WIDTH), BF16),
        compiler_params=_params(("parallel", "arbitrary")), name="attn_prompt",
    )(q, k_bf, v_bf, lq1, lk1, lq2, lk2, subln)


def _decode_attn_kernel(pt_ref, q_ref, kn_ref, vn_ref, lq1, lk1, lq2, lk2, subln_ref, *rest, n_pg):
    k_refs = rest[:n_pg]
    v_refs = rest[n_pg:2 * n_pg]
    o_ref = rest[2 * n_pg]
    qr_ref, s_scr, w_scr, wn_scr, acc_ref = rest[2 * n_pg + 1:]
    phase = pl.program_id(1)
    j = pl.program_id(2)
    last = pl.num_programs(2) - 1
    nrow = 2 * ATT_HEADS

    @pl.when((phase == 0) & (j == 0))
    def _():
        row = lax.broadcasted_iota(I32, (nrow, ATT_WIDTH), 0)
        lane = lax.broadcasted_iota(I32, (nrow, ATT_WIDTH), 1)
        sel = (lane // LANES == row % ATT_HEADS) & ((lane // ATT_HEAD_DIM) % 2 == row // ATT_HEADS)
        qb = jnp.broadcast_to(q_ref[...].astype(F32), (nrow, ATT_WIDTH))
        qr_ref[...] = jnp.where(sel, qb, 0.0).astype(BF16)

    @pl.when(phase == 0)
    def _():
        qr = qr_ref[...]
        s_scr[j] = jnp.concatenate(
            [jnp.dot(qr, kr[...].astype(BF16), preferred_element_type=F32) for kr in k_refs], axis=1)

    @pl.when((phase == 1) & (j == 0))
    def _():
        sn = jnp.sum(qr_ref[...].astype(F32) * kn_ref[...].astype(F32), axis=1, keepdims=True)
        s = s_scr[...]
        m = jnp.maximum(jnp.max(jnp.max(s, axis=2, keepdims=True), axis=0, keepdims=True), sn[None])
        e = jnp.exp(s - m)
        en = jnp.exp(sn[None] - m)
        l = jnp.sum(jnp.sum(e, axis=2, keepdims=True), axis=0, keepdims=True) + en
        p = e / l
        pn = (en / l)[0]
        lam = _lambda_value(lq1, lk1, lq2, lk2)
        w = p[:, :ATT_HEADS, :] - lam * p[:, ATT_HEADS:, :]
        w_scr[...] = jnp.concatenate([w, jnp.zeros_like(w)], axis=1).astype(BF16)
        wn_scr[...] = pn[:ATT_HEADS] - lam * pn[ATT_HEADS:]
        acc_ref[...] = jnp.zeros(acc_ref.shape, F32)

    @pl.when(phase == 1)
    def _():
        wb = w_scr[j]
        page = wb.shape[1] // n_pg
        heads = []
        for h in range(ATT_HEADS):
            vh = jnp.concatenate([vr[pl.ds(h, page, stride=ATT_HEADS), :].astype(BF16) for vr in v_refs], axis=0)
            heads.append(jnp.dot(wb, vh, preferred_element_type=F32))
        acc_ref[...] = acc_ref[...] + jnp.concatenate(heads, axis=1)

    @pl.when((phase == 1) & (j == last))
    def _():
        wn = wn_scr[...].astype(BF16).astype(F32)
        od = acc_ref[...][:ATT_HEADS] + wn * vn_ref[...].astype(F32)
        own = (lax.broadcasted_iota(I32, od.shape, 1) // LANES) == lax.broadcasted_iota(I32, od.shape, 0)
        od = jnp.where(own, od, 0.0)
        ms = jnp.sum(od * od, axis=1, keepdims=True) / (2 * ATT_HEAD_DIM)
        y = od * lax.rsqrt(ms + RMS_EPS) * subln_ref[...] * (1.0 - LAMBDA_INIT)
        o_ref[...] = jnp.sum(y, axis=0, keepdims=True).astype(BF16)


def _attention_decode(q, k_new_bf, v_new_bf, cache_kt, cache_v, page_table, lq1, lk1, lq2, lk2, subln):
    Bd = q.shape[0]
    page = cache_kt.shape[2]
    n_pages = page_table.shape[1]
    n_pg = math.gcd(n_pages, PAGES_PER_STEP)
    assert n_pages % n_pg == 0
    subln_w = jnp.tile(subln, (1, ATT_HEADS))

    n_groups = n_pages // n_pg
    nrow = 2 * ATT_HEADS

    def k_spec(g):
        return pl.BlockSpec((None, ATT_WIDTH, page), lambda b, ph, j, pt: (
            pt[b * n_pages + jnp.where(ph == 0, j, n_groups - 1) * n_pg + g], 0, 0))

    def v_spec(g):
        return pl.BlockSpec((None, page * ATT_HEADS, 2 * ATT_HEAD_DIM), lambda b, ph, j, pt: (
            pt[b * n_pages + jnp.where(ph == 0, 0, j) * n_pg + g], 0, 0))

    row_spec = pl.BlockSpec((None, 1, ATT_WIDTH), lambda b, ph, j, pt: (b, 0, 0))
    small = lambda shape: pl.BlockSpec(shape, lambda b, ph, j, pt: (0,) * len(shape))
    grid_spec = pltpu.PrefetchScalarGridSpec(
        num_scalar_prefetch=1, grid=(Bd, 2, n_groups),
        in_specs=[row_spec, row_spec, row_spec] + [small((1, ATT_HEAD_DIM))] * 4 + [small((1, ATT_WIDTH))]
                 + [k_spec(g) for g in range(n_pg)] + [v_spec(g) for g in range(n_pg)],
        out_specs=row_spec,
        scratch_shapes=[pltpu.VMEM((nrow, ATT_WIDTH), BF16),
                        pltpu.VMEM((n_groups, nrow, n_pg * page), F32),
                        pltpu.VMEM((n_groups, nrow, n_pg * page), BF16),
                        pltpu.VMEM((ATT_HEADS, 1), F32),
                        pltpu.VMEM((nrow, ATT_WIDTH), F32)])
    out = pl.pallas_call(
        functools.partial(_decode_attn_kernel, n_pg=n_pg), grid_spec=grid_spec,
        out_shape=jax.ShapeDtypeStruct((Bd, 1, ATT_WIDTH), BF16),
        compiler_params=_params(("parallel", "arbitrary", "arbitrary")), name="attn_decode",
    )(page_table.reshape(-1), q.reshape(Bd, 1, -1), k_new_bf.reshape(Bd, 1, -1), v_new_bf.reshape(Bd, 1, -1),
      lq1, lk1, lq2, lk2, subln_w, *([cache_kt] * n_pg), *([cache_v] * n_pg))
    return out.reshape(Bd, ATT_WIDTH)


def _hgrn_masks():
    C = HG_CHUNK
    t = np.arange(C)[:, None]
    r = np.arange(C)[None, :]
    blocks = [(r <= t), (r > t)]
    for B in HG_LEVELS:
        mid = (t // B) * B + B // 2 - 1
        second = (t % B) >= B // 2
        blocks.append(np.where(second, (r > mid) & (r <= t), (r > t) & (r <= mid)))
    return jnp.asarray(np.concatenate(blocks, axis=0).astype(np.float32), dtype=BF16)


def _lower_bound(lb_ref):
    a = lb_ref[...].astype(F32)
    e = jnp.exp(a - jnp.max(a, axis=0, keepdims=True))
    return e[0:1] / jnp.sum(e, axis=0, keepdims=True)


def _group_rows(x, j):
    return jnp.broadcast_to(x[:, j:j + 1, :], x.shape)


def _split3(x):
    hi = x.astype(BF16)
    r1 = x - hi.astype(F32)
    mid = r1.astype(BF16)
    lo = (r1 - mid.astype(F32)).astype(BF16)
    return hi, mid, lo


def _hgrn_tile(q, z, v, lb, msk, s_t):
    C = HG_CHUNK
    n = q.shape[0] // C
    dot = lambda a, b: jnp.dot(a, b, preferred_element_type=F32)
    rows = lambda x, c: x[c * C:(c + 1) * C]
    chunks = range(n)

    logf = jnp.log(lb + (1.0 - lb) * jax.nn.sigmoid(z))
    kk = (1.0 - lb) * jax.nn.sigmoid(-z)
    v_bf = v.astype(BF16)
    parts = _split3(logf)
    e = [sum(dot(msk, rows(p, c)) for p in parts) for c in chunks]
    b = [ec[0:C] for ec in e]
    qi = [(rows(q, c) * jnp.exp(b[c])).astype(BF16) for c in chunks]
    kl = [(rows(kk, c) * jnp.exp(e[c][C:2 * C])).astype(BF16) for c in chunks]
    kv = [lax.dot_general(rows(v_bf, c), kl[c], _TN, preferred_element_type=F32) for c in chunks]

    t_idx = lax.broadcasted_iota(I32, (C, 1), 0)
    row = lax.broadcasted_iota(I32, (C, C), 0)
    col = lax.broadcasted_iota(I32, (C, C), 1)
    a = [jnp.zeros((C, C), F32) for _ in chunks]
    for li, B in enumerate(HG_LEVELS):
        second = (t_idx % B) >= B // 2
        same_block = (row // B) == (col // B)
        for c in chunks:
            x = jnp.exp(e[c][(2 + li) * C:(3 + li) * C])
            qt = jnp.where(second, rows(q, c) * x, 0.0).astype(BF16)
            kt = jnp.where(second, 0.0, rows(kk, c) * x).astype(BF16)
            al = lax.dot_general(qt, kt, _NT, preferred_element_type=F32)
            a[c] = a[c] + (jnp.where(same_block, al, 0.0) if B < C else al)

    states = [s_t]
    for c in chunks:
        states.append(states[-1] * jnp.exp(b[c][C - 1:C, :]) + kv[c])
    o = [lax.dot_general(qi[c], states[c].astype(BF16), _NT, preferred_element_type=F32)
         + dot(a[c].astype(BF16), rows(v_bf, c)) for c in chunks]

    G = n * C // SUBLANES
    b_all = jnp.concatenate(b, axis=0) if n > 1 else b[0]
    q3, k3, v3, b3 = (x.reshape(G, SUBLANES, LANES) for x in (q, kk, v, b_all))
    p3 = lax.broadcasted_iota(I32, (G, SUBLANES, 1), 1)
    o3 = (jnp.concatenate(o, axis=0) if n > 1 else o[0]).reshape(G, SUBLANES, LANES)
    for j in range(SUBLANES):
        term = q3 * _group_rows(k3, j) * jnp.exp(jnp.minimum(b3 - _group_rows(b3, j), 0.0))
        aj = jnp.where(p3 >= j, jnp.sum(term, axis=-1, keepdims=True), 0.0)
        o3 = o3 + aj * _group_rows(v3, j)
    return o3.reshape(n * C, LANES), states[-1]


def _hgrn_kernel(q_ref, z_ref, v_ref, g_ref, lb_ref, hgn_ref, msk_ref, o_ref, st_ref, s_scr):
    i = pl.program_id(1)

    @pl.when(i == 0)
    def _():
        s_scr[...] = jnp.zeros(s_scr.shape, F32)

    o, s_t = _hgrn_tile(q_ref[...], z_ref[...], v_ref[...], _lower_bound(lb_ref), msk_ref[...], s_scr[...])
    s_scr[...] = s_t
    g = g_ref[...]
    o_ref[...] = (_rms(o, hgn_ref[...]) * (g * jax.nn.sigmoid(g))).astype(BF16)

    @pl.when(i == pl.num_programs(1) - 1)
    def _():
        st_ref[...] = s_scr[...].T


def _hgrn_prompt(hg, hgrn_lb, hgrn_norm, tt):
    S = hg.shape[0]
    msk = _hgrn_masks()
    col = lambda seg: pl.BlockSpec((tt, LANES), lambda h, i, seg=seg: (i, seg * HG_HEADS + h))
    o, st = pl.pallas_call(
        _hgrn_kernel, grid=(HG_HEADS, S // tt),
        in_specs=[col(0), col(1), col(2), col(3),
                  pl.BlockSpec((hgrn_lb.shape[0], LANES), lambda h, i: (0, h)),
                  _small_spec((1, HG_VAL)), _small_spec(tuple(msk.shape))],
        out_specs=[pl.BlockSpec((tt, LANES), lambda h, i: (i, h)),
                   pl.BlockSpec((None, HG_KEY, HG_VAL), lambda h, i: (h, 0, 0))],
        out_shape=[jax.ShapeDtypeStruct((S, HG_WIDTH), BF16),
                   jax.ShapeDtypeStruct((HG_HEADS, HG_KEY, HG_VAL), F32)],
        scratch_shapes=[pltpu.VMEM((HG_VAL, HG_KEY), F32)],
        compiler_params=_params(("parallel", "arbitrary")), name="hgrn_prompt",
    )(hg, hg, hg, hg, hgrn_lb, hgrn_norm, msk)
    return o, st


def _hgrn_decode_kernel(hg_ref, s_ref, lb_ref, hgn_ref, o_ref, sn_ref):
    W = HG_WIDTH
    row = hg_ref[...]
    q, z, v, g = (row[:, s * W:(s + 1) * W] for s in range(4))
    lb = _lower_bound(lb_ref)
    f = lb + (1.0 - lb) * jax.nn.sigmoid(z)
    kk = (1.0 - lb) * jax.nn.sigmoid(-z)
    rnd = lambda x: x.astype(BF16).astype(F32)
    qf = rnd(q * f)
    qk = rnd(q) * rnd(kk)
    pad = jnp.zeros((SUBLANES - 3, LANES), F32)
    outs = []
    for h in range(HG_HEADS):
        hs = slice(h * LANES, (h + 1) * LANES)
        cols = jnp.concatenate([f[:, hs], kk[:, hs], qf[:, hs], pad], axis=0).T
        f_c, k_c, qf_c = cols[:, 0:1], cols[:, 1:2], cols[:, 2:3]
        s0 = s_ref[h]
        vh = v[:, hs]
        sn_ref[h] = f_c * s0 + k_c * vh
        o = (jnp.sum(qf_c * rnd(s0), axis=0, keepdims=True)
             + rnd(jnp.sum(qk[:, hs], axis=1, keepdims=True)) * rnd(vh))
        gh = g[:, hs]
        outs.append(_rms(o, hgn_ref[...]) * (gh * jax.nn.sigmoid(gh)))
    o_ref[...] = jnp.concatenate(outs, axis=1).astype(BF16)


def _hgrn_decode(hg, state, hgrn_lb, hgrn_norm):
    Bd = hg.shape[0]
    o, sn = pl.pallas_call(
        _hgrn_decode_kernel, grid=(Bd,),
        in_specs=[pl.BlockSpec((None, 1, 4 * HG_WIDTH), lambda b: (b, 0, 0)),
                  pl.BlockSpec((None, HG_HEADS, HG_KEY, HG_VAL), lambda b: (b, 0, 0, 0)),
                  _small_spec(tuple(hgrn_lb.shape)), _small_spec((1, HG_VAL))],
        out_specs=[pl.BlockSpec((None, 1, HG_WIDTH), lambda b: (b, 0, 0)),
                   pl.BlockSpec((None, HG_HEADS, HG_KEY, HG_VAL), lambda b: (b, 0, 0, 0))],
        out_shape=[jax.ShapeDtypeStruct((Bd, 1, HG_WIDTH), BF16),
                   jax.ShapeDtypeStruct(state.shape, F32)],
        compiler_params=_params(("parallel",)), name="hgrn_decode",
    )(hg.reshape(Bd, 1, -1), state, hgrn_lb, hgrn_norm)
    return o.reshape(Bd, HG_WIDTH), sn


def _merge_kernel(oa_ref, oh_ref, wa_ref, wb_ref, ga_ref, gb_ref, o_ref):
    a = jnp.dot(oa_ref[...], wa_ref[...].astype(BF16), preferred_element_type=F32)
    b = jnp.dot(oh_ref[...], wb_ref[...].astype(BF16), preferred_element_type=F32)
    ga = jax.nn.sigmoid(ga_ref[...].astype(F32))
    gb = jax.nn.sigmoid(gb_ref[...].astype(F32))
    o_ref[...] = (ga * a + gb * b).astype(BF16)


def _merge(oa, oh, w_pa, w_pb, gates, tm, tn, tag):
    M = oa.shape[0]
    nj = D_MODEL // tn
    return pl.pallas_call(
        _merge_kernel, grid=(M // tm, nj),
        in_specs=[pl.BlockSpec((tm, ATT_WIDTH), lambda i, j: (i, 0)),
                  pl.BlockSpec((tm, HG_WIDTH), lambda i, j: (i, 0)),
                  pl.BlockSpec((None, ATT_WIDTH, tn), lambda i, j: (0, 0, j)),
                  pl.BlockSpec((None, HG_WIDTH, tn), lambda i, j: (0, 0, j)),
                  pl.BlockSpec((tm, tn), lambda i, j: (i, j)),
                  pl.BlockSpec((tm, tn), lambda i, j: (i, j + nj))],
        out_specs=pl.BlockSpec((tm, tn), lambda i, j: (i, j)),
        out_shape=jax.ShapeDtypeStruct((M, D_MODEL), BF16),
        compiler_params=_params(("parallel", "arbitrary")), name=f"merge_{tag}",
    )(oa, oh, w_pa, w_pb, gates, gates)


def _cross_prompt_kernel(x_ref, g_ref, wq_ref, mk_ref, mv_ref, wo_ref, o_ref):
    x = x_ref[...]
    u = _rms(x, g_ref[...]).astype(BF16)
    q = jnp.dot(u, wq_ref[...], preferred_element_type=F32).astype(BF16)
    heads = []
    for h in range(X_HEADS):
        hs = slice(h * X_HEAD_DIM, (h + 1) * X_HEAD_DIM)
        s = lax.dot_general(q[:, hs], mk_ref[:, hs], _NT, preferred_element_type=F32) * (X_HEAD_DIM ** -0.5)
        p = jnp.exp(s - jnp.max(s, axis=1, keepdims=True))
        p = p / jnp.sum(p, axis=1, keepdims=True)
        heads.append(jnp.dot(p.astype(BF16), mv_ref[:, hs], preferred_element_type=F32))
    o = jnp.concatenate(heads, axis=1).astype(BF16)
    o_ref[...] = x + jnp.dot(o, wo_ref[...], preferred_element_type=F32)


def _cross_prompt(h1, g_cross, w_cq_bf, mk_bf, mv_bf, w_co_bf, tm):
    M = h1.shape[0]
    return pl.pallas_call(
        _cross_prompt_kernel, grid=(M // tm,),
        in_specs=[pl.BlockSpec((tm, D_MODEL), lambda i: (i, 0)), _small_spec((1, D_MODEL)),
                  _small_spec((D_MODEL, X_WIDTH)), _small_spec(tuple(mk_bf.shape)),
                  _small_spec(tuple(mv_bf.shape)), _small_spec((X_WIDTH, D_MODEL))],
        out_specs=pl.BlockSpec((tm, D_MODEL), lambda i: (i, 0)),
        out_shape=jax.ShapeDtypeStruct((M, D_MODEL), F32),
        compiler_params=_params(("parallel",)), name="cross_prompt",
    )(h1, g_cross.reshape(1, -1), w_cq_bf, mk_bf, mv_bf, w_co_bf)


def _cross_decode_kernel(q_ref, mk_ref, mv_ref, o_ref):
    rnd = lambda x: x.astype(BF16).astype(F32)
    q = rnd(q_ref[...])
    mem = mk_ref.shape[0] // X_HEADS
    outs = []
    for h in range(X_HEADS):
        hs = slice(h * X_HEAD_DIM, (h + 1) * X_HEAD_DIM)
        rows = pl.ds(h, mem, stride=X_HEADS)
        s = jnp.sum(rnd(mk_ref[rows, :]) * q[:, hs], axis=1, keepdims=True) * (X_HEAD_DIM ** -0.5)
        p = jnp.exp(s - jnp.max(s, axis=0, keepdims=True))
        p = p / jnp.sum(p, axis=0, keepdims=True)
        outs.append(jnp.sum(rnd(p) * rnd(mv_ref[rows, :]), axis=0, keepdims=True))
    o_ref[...] = jnp.concatenate(outs, axis=1).astype(BF16)


def _cross_decode(q, mem_k, mem_v):
    Bd, mem = mem_k.shape[0], mem_k.shape[1]
    row = pl.BlockSpec((None, 1, X_WIDTH), lambda b: (b, 0, 0))
    mem_spec = pl.BlockSpec((None, mem * X_HEADS, X_HEAD_DIM), lambda b: (b, 0, 0))
    flat = lambda m: m.reshape(Bd, mem * X_HEADS, X_HEAD_DIM)
    out = pl.pallas_call(
        _cross_decode_kernel, grid=(Bd,), in_specs=[row, mem_spec, mem_spec], out_specs=row,
        out_shape=jax.ShapeDtypeStruct((Bd, 1, X_WIDTH), BF16),
        compiler_params=_params(("parallel",)), name="cross_decode",
    )(q.reshape(Bd, 1, X_WIDTH), flat(mem_k), flat(mem_v))
    return out.reshape(Bd, X_WIDTH)


def _route_kernel(x_ref, g_ref, w_ref, b_ref, cin_ref, uin_ref, u_ref, eid_ref, wt_ref, rank_ref, cnt_ref,
                  c_scr, *, n_rows):
    i = pl.program_id(0)
    tm = x_ref.shape[0]

    @pl.when(i == 0)
    def _():
        c_scr[...] = cin_ref[...]

    u = _rms(x_ref[...], g_ref[...])
    u_ref[...] = u
    dot = lambda a, b: jnp.dot(a, b, preferred_element_type=F32)
    logits = dot(u.astype(BF16), w_ref[...]) + b_ref[...]

    lane = lax.broadcasted_iota(I32, (tm, LANES), 1)
    big = jnp.int32(1 << 20)
    first_max = lambda vals, vmax: jnp.min(jnp.where(vals == vmax, lane, big), axis=1, keepdims=True)
    glv = jnp.where(lane < N_GROUPS, logits, -jnp.inf)
    gmax = jnp.max(glv, axis=1, keepdims=True)
    gsel = first_max(glv, gmax)
    p_group = 1.0 / jnp.sum(jnp.exp(glv - gmax), axis=1, keepdims=True)
    e_lane = lane - N_GROUPS
    in_group = (e_lane >= 0) & (e_lane < N_EXPERTS) & (e_lane // EXPERTS_PER_GROUP == gsel)
    ev = jnp.where(in_group, logits, -jnp.inf)
    v1 = jnp.max(ev, axis=1, keepdims=True)
    i1 = first_max(ev, v1)
    ev2 = jnp.where(lane == i1, -jnp.inf, ev)
    v2 = jnp.max(ev2, axis=1, keepdims=True)
    i2 = first_max(ev2, v2)
    t2 = jnp.exp(v2 - v1)
    w1 = p_group / (1.0 + t2)
    w2 = w1 * t2
    e1 = i1 - N_GROUPS
    e2 = i2 - N_GROUPS
    eid_ref[...] = jnp.where(lane == 0, e1, jnp.where(lane == 1, e2, 0))
    wt_ref[...] = jnp.where(lane == 0, w1, jnp.where(lane == 1, w2, 0.0))

    valid = (lax.broadcasted_iota(I32, (tm, 1), 0) + i * tm) < n_rows
    oh1 = jnp.where((lane == e1) & valid, 1.0, 0.0)
    oh2 = jnp.where((lane == e2) & valid, 1.0, 0.0)
    cnt = (oh1 + oh2).astype(BF16)
    r_i = lax.broadcasted_iota(I32, (tm, tm), 0)
    c_i = lax.broadcasted_iota(I32, (tm, tm), 1)
    strict = jnp.where(c_i < r_i, 1.0, 0.0).astype(BF16)
    before = dot(strict, cnt) + c_scr[...]
    r1 = jnp.sum(oh1 * before, axis=1, keepdims=True)
    r2 = jnp.sum(oh2 * before, axis=1, keepdims=True)
    rank_ref[...] = jnp.where(lane == 0, r1, jnp.where(lane == 1, r2, 0.0)).astype(I32)
    c_scr[...] = c_scr[...] + jnp.sum(oh1 + oh2, axis=0, keepdims=True)
    cnt_ref[...] = c_scr[...]


def _route(x, g_ffn, w_router, b_router, counts_in, u_all, row_off, tm, tag):
    M = x.shape[0]
    T = u_all.shape[0]
    boff = row_off // tm
    kern = functools.partial(_route_kernel, n_rows=M)
    lane_out = lambda dt: jax.ShapeDtypeStruct((M, LANES), dt)
    outs = pl.pallas_call(
        kern, grid=(M // tm,),
        in_specs=[pl.BlockSpec((tm, D_MODEL), lambda i: (i, 0)), _small_spec((1, D_MODEL)),
                  _small_spec((D_MODEL, LANES)), _small_spec((1, LANES)), _small_spec((1, LANES)),
                  pl.BlockSpec(memory_space=pl.ANY)],
        out_specs=[pl.BlockSpec((tm, D_MODEL), lambda i: (i + boff, 0)),
                   pl.BlockSpec((tm, LANES), lambda i: (i, 0)), pl.BlockSpec((tm, LANES), lambda i: (i, 0)),
                   pl.BlockSpec((tm, LANES), lambda i: (i, 0)), _small_spec((1, LANES))],
        out_shape=[jax.ShapeDtypeStruct((T, D_MODEL), F32), lane_out(I32), lane_out(F32), lane_out(I32),
                   jax.ShapeDtypeStruct((1, LANES), F32)],
        scratch_shapes=[pltpu.VMEM((1, LANES), F32)],
        input_output_aliases={5: 0},
        compiler_params=_params(("arbitrary",)), name=f"moe_route_{tag}",
    )(x, g_ffn.reshape(1, -1), w_router, b_router, counts_in, u_all)
    return outs


def _plan_kernel(cnt_ref, pstart_ref, blkexp_ref, blkslot_ref, nxtexp_ref, nblk_ref, *, n_blocks):
    def per_expert(e, carry):
        acc, k = carry
        pstart_ref[e] = acc
        nb = (cnt_ref[e] + MOE_ROWS - 1) // MOE_ROWS
        first = acc // MOE_ROWS

        def fill(b, c):
            blkexp_ref[b] = e
            blkslot_ref[b] = k % 2
            return c

        lax.fori_loop(first, first + nb, fill, 0)
        return acc + nb * MOE_ROWS, k + jnp.where(nb > 0, 1, 0)

    total, _ = lax.fori_loop(0, N_EXPERTS, per_expert, (jnp.int32(0), jnp.int32(0)))
    used = total // MOE_ROWS
    nblk_ref[0] = used
    last = jnp.maximum(used - 1, 0)

    def backwards(t, carry):
        cur, nxt = carry
        b = last - t
        e = blkexp_ref[b]
        nxt = jnp.where(e != cur, cur, nxt)
        nxtexp_ref[b] = nxt
        return e, nxt

    lax.fori_loop(0, used, backwards, (blkexp_ref[last], jnp.int32(-1)))

    def tail(b, c):
        blkexp_ref[b] = blkexp_ref[last]
        blkslot_ref[b] = blkslot_ref[last]
        nxtexp_ref[b] = -1
        return c

    lax.fori_loop(used, n_blocks, tail, 0)


def _plan(counts, n_blocks):
    smem = pl.BlockSpec(memory_space=pltpu.SMEM)
    per_block = jax.ShapeDtypeStruct((n_blocks,), I32)
    return pl.pallas_call(
        functools.partial(_plan_kernel, n_blocks=n_blocks),
        in_specs=[smem], out_specs=[smem] * 5,
        out_shape=[jax.ShapeDtypeStruct((N_EXPERTS,), I32), per_block, per_block, per_block,
                   jax.ShapeDtypeStruct((1,), I32)],
        name="moe_plan",
    )(counts)


def _dispatch_kernel(eid_ref, rank_ref, pstart_ref, cnt_ref, nblk_ref, u_ref, xs_hbm, zbuf, sem_z, sem,
                     *, n_tokens, n_blocks):
    R = MOE_ROWS
    i = pl.program_id(0)
    tm = u_ref.shape[0]

    def zero_copy(b):
        return pltpu.make_async_copy(zbuf, xs_hbm.at[pl.ds(pl.multiple_of(b * R, R), R), :], sem_z)

    def partial_block(e):
        c = cnt_ref[e]
        return (c % R) != 0, (pstart_ref[e] + c) // R

    def zero_partial(start):
        def body(e, n):
            has, b = partial_block(e)

            @pl.when(has)
            def _():
                zero_copy(b).start() if start else zero_copy(b).wait()

            return n
        return body

    def zero_unused(start):
        def body(b, n):
            zero_copy(b).start() if start else zero_copy(b).wait()
            return n
        return body

    @pl.when(i == 0)
    def _():
        zbuf[...] = jnp.zeros(zbuf.shape, F32)
        for start in (True, False):
            lax.fori_loop(0, N_EXPERTS, zero_partial(start), 0)
            lax.fori_loop(nblk_ref[0], n_blocks, zero_unused(start), 0)

    def row_copy(r, k):
        a = 2 * (i * tm + r) + k
        d = pstart_ref[eid_ref[a]] + rank_ref[a]
        return pltpu.make_async_copy(u_ref.at[pl.ds(r, 1), :], xs_hbm.at[pl.ds(d, 1), :], sem)

    def start_row(r, n):
        row_copy(r, 0).start()
        row_copy(r, 1).start()
        return n

    def wait_row(r, n):
        row_copy(r, 0).wait()
        row_copy(r, 1).wait()
        return n

    rem = n_tokens % tm

    @pl.when((i + 1) * tm <= n_tokens)
    def _():
        lax.fori_loop(0, tm, start_row, 0, unroll=8)
        lax.fori_loop(0, tm, wait_row, 0, unroll=8)

    if rem:
        @pl.when((i + 1) * tm > n_tokens)
        def _():
            lax.fori_loop(0, rem, start_row, 0, unroll=8)
            lax.fori_loop(0, rem, wait_row, 0, unroll=8)


def _dispatch(eid, rank, pstart, counts, n_used, u_all, n_blocks, tm):
    T = u_all.shape[0]
    grid_spec = pltpu.PrefetchScalarGridSpec(
        num_scalar_prefetch=5, grid=(pl.cdiv(T, tm),),
        in_specs=[pl.BlockSpec((tm, D_MODEL), lambda i, *_: (i, 0))],
        out_specs=pl.BlockSpec(memory_space=pl.ANY),
        scratch_shapes=[pltpu.VMEM((MOE_ROWS, D_MODEL), F32), pltpu.SemaphoreType.DMA(()),
                        pltpu.SemaphoreType.DMA(())])
    return pl.pallas_call(
        functools.partial(_dispatch_kernel, n_tokens=T, n_blocks=n_blocks), grid_spec=grid_spec,
        out_shape=jax.ShapeDtypeStruct((n_blocks * MOE_ROWS, D_MODEL), F32),
        compiler_params=_params(("arbitrary",)), name="moe_dispatch",
    )(eid, rank, pstart, counts, n_used, u_all)


def _expert_kernel(blkexp_ref, blkslot_ref, nxtexp_ref, nblk_ref, x_ref, wg_hbm, wu_hbm, wd_hbm, o_ref,
                   wg_f, wu_f, wd_f, wg_s, wu_s, wd_s, sem):
    i = pl.program_id(0)

    def fetch(e, slot):
        return [pltpu.make_async_copy(src.at[0, e], dst.at[slot], sem.at[slot])
                for src, dst in ((wg_hbm, wg_f), (wu_hbm, wu_f), (wd_hbm, wd_f))]

    @pl.when(i < nblk_ref[0])
    def _():
        e = blkexp_ref[i]
        slot = blkslot_ref[i]
        nxt = nxtexp_ref[i]

        @pl.when(i == 0)
        def _():
            for cp in fetch(e, slot):
                cp.start()

        @pl.when((i == 0) | (e != blkexp_ref[jnp.maximum(i - 1, 0)]))
        def _():
            for cp in fetch(e, slot):
                cp.wait()

            @pl.when(nxt >= 0)
            def _():
                for cp in fetch(nxt, 1 - slot):
                    cp.start()

            wg_s[...] = wg_f[slot].astype(BF16)
            wu_s[...] = wu_f[slot].astype(BF16)
            wd_s[...] = wd_f[slot].astype(BF16)

        x = x_ref[...].astype(BF16)
        hg = jnp.dot(x, wg_s[...], preferred_element_type=F32)
        hu = jnp.dot(x, wu_s[...], preferred_element_type=F32)
        h = (hg * jax.nn.sigmoid(hg) * hu).astype(BF16)
        o_ref[...] = jnp.dot(h, wd_s[...], preferred_element_type=F32)

    @pl.when(i >= nblk_ref[0])
    def _():
        o_ref[...] = jnp.zeros(o_ref.shape, F32)


def _experts(xs, blk_exp, blk_slot, nxt_exp, n_used, we_g, we_u, we_d, n_blocks):
    hbm = pl.BlockSpec(memory_space=pl.ANY)
    up, down = (D_MODEL, EXPERT_FF), (EXPERT_FF, D_MODEL)
    grid_spec = pltpu.PrefetchScalarGridSpec(
        num_scalar_prefetch=4, grid=(n_blocks,),
        in_specs=[pl.BlockSpec((MOE_ROWS, D_MODEL),
                               lambda i, be, bs, nx, nb: (jnp.maximum(jnp.minimum(i, nb[0] - 1), 0), 0)),
                  hbm, hbm, hbm],
        out_specs=pl.BlockSpec((MOE_ROWS, D_MODEL), lambda i, *_: (i, 0)),
        scratch_shapes=[pltpu.VMEM((2,) + up, F32), pltpu.VMEM((2,) + up, F32), pltpu.VMEM((2,) + down, F32),
                        pltpu.VMEM(up, BF16), pltpu.VMEM(up, BF16), pltpu.VMEM(down, BF16),
                        pltpu.SemaphoreType.DMA((2,))])
    return pl.pallas_call(
        _expert_kernel, grid_spec=grid_spec,
        out_shape=jax.ShapeDtypeStruct((n_blocks * MOE_ROWS, D_MODEL), F32),
        compiler_params=_params(("arbitrary",)), name="moe_experts",
    )(blk_exp, blk_slot, nxt_exp, n_used, xs, we_g, we_u, we_d)


def _combine_kernel(eid_ref, rank_ref, pstart_ref, h_ref, wt_ref, gf_ref, yb_hbm, o_ref, ybuf0, ybuf1, sem,
                    *, dest_off):
    i = pl.program_id(0)
    tm = h_ref.shape[0]
    base = dest_off + i * (2 * tm)

    def row_copy(r, k, buf):
        a = base + 2 * r + k
        d = pstart_ref[eid_ref[a]] + rank_ref[a]
        return pltpu.make_async_copy(yb_hbm.at[pl.ds(d, 1), :], buf.at[pl.ds(r, 1), :], sem)

    def start(r, c):
        row_copy(r, 0, ybuf0).start()
        row_copy(r, 1, ybuf1).start()
        return c

    def wait(r, c):
        row_copy(r, 0, ybuf0).wait()
        row_copy(r, 1, ybuf1).wait()
        return c

    lax.fori_loop(0, tm, start, 0, unroll=4)
    lax.fori_loop(0, tm, wait, 0, unroll=4)
    wt = wt_ref[...]
    h = h_ref[...] + (wt[:, 0:1] * ybuf0[...] + wt[:, 1:2] * ybuf1[...])
    o_ref[...] = _rms(h, gf_ref[...])


def _combine(h2, wts, eid, rank, pstart, yb, norm_final, dest_off, tm, tag):
    M = h2.shape[0]
    grid_spec = pltpu.PrefetchScalarGridSpec(
        num_scalar_prefetch=3, grid=(M // tm,),
        in_specs=[pl.BlockSpec((tm, D_MODEL), lambda i, *_: (i, 0)), pl.BlockSpec((tm, LANES), lambda i, *_: (i, 0)),
                  pl.BlockSpec((1, D_MODEL), lambda i, *_: (0, 0)), pl.BlockSpec(memory_space=pl.ANY)],
        out_specs=pl.BlockSpec((tm, D_MODEL), lambda i, *_: (i, 0)),
        scratch_shapes=[pltpu.VMEM((tm, D_MODEL), F32), pltpu.VMEM((tm, D_MODEL), F32),
                        pltpu.SemaphoreType.DMA(())])
    return pl.pallas_call(
        functools.partial(_combine_kernel, dest_off=dest_off), grid_spec=grid_spec,
        out_shape=jax.ShapeDtypeStruct((M, D_MODEL), F32),
        compiler_params=_params(("arbitrary",)), name=f"moe_combine_{tag}",
    )(eid, rank, pstart, h2, wts, norm_final.reshape(1, -1), yb)


def _moe_and_final_norm(h2_p, h2_s, g_ffn, wr_g, br_g, wr_e, br_e, we_g, we_u, we_d, norm_final, tm_p):
    Tp, Ts = h2_p.shape[0], h2_s.shape[0]
    T = Tp + Ts
    pad = LANES - N_GROUPS - N_EXPERTS
    w_router = jnp.concatenate([wr_g, wr_e, jnp.zeros((D_MODEL, pad), F32)], axis=1).astype(BF16)
    b_router = jnp.concatenate([br_g, br_e, jnp.zeros((pad,), F32)]).reshape(1, LANES)
    u_all = jnp.zeros((T, D_MODEL), F32)
    zero_counts = jnp.zeros((1, LANES), F32)
    u_all, eid_p, wt_p, rank_p, counts = _route(h2_p, g_ffn, w_router, b_router, zero_counts, u_all, 0, tm_p, "p")
    u_all, eid_s, wt_s, rank_s, counts = _route(h2_s, g_ffn, w_router, b_router, counts, u_all, Tp, Ts, "s")
    eid = jnp.concatenate([eid_p[:, :2].reshape(-1), eid_s[:, :2].reshape(-1)])
    rank = jnp.concatenate([rank_p[:, :2].reshape(-1), rank_s[:, :2].reshape(-1)])
    A = 2 * T
    n_blocks = (A + N_EXPERTS * (MOE_ROWS - 1)) // MOE_ROWS + 1
    cnt = counts[0, :N_EXPERTS].astype(I32)
    pstart, blk_exp, blk_slot, nxt_exp, n_used = _plan(cnt, n_blocks)
    xs = _dispatch(eid, rank, pstart, cnt, n_used, u_all, n_blocks, 256)
    yb = _experts(xs, blk_exp, blk_slot, nxt_exp, n_used, we_g, we_u, we_d, n_blocks)
    y_p = _combine(h2_p, wt_p, eid, rank, pstart, yb, norm_final, 0, 256, "p")
    y_s = _combine(h2_s, wt_s, eid, rank, pstart, yb, norm_final, 2 * Tp, Ts, "s")
    return y_p, y_s


def kernel(x_prompt, x_sample, mem_prompt, cache_k, cache_v, cache_mem_k, cache_mem_v, state_hgrn, page_table,
           norm_mix, w_in, lambda_q1, lambda_k1, lambda_q2, lambda_k2, subln, hgrn_lb, hgrn_norm, w_pa, w_pb,
           w_out, norm_cross, w_cq, w_ck, w_cv, w_co, norm_ffn, w_router_group, b_router_group,
           w_router_expert, b_router_expert, w_e_gate, w_e_up, w_e_down, norm_final):
    assert w_in.shape[0] == 1, "single-layer step"
    Bp, S, D = x_prompt.shape
    Bd, Ld, _ = x_sample.shape
    assert Bp == 1 and Ld == 1
    n_pages = page_table.shape[1]
    page = cache_k.shape[2]
    past_len = n_pages * page
    xp = x_prompt.reshape(S, D)
    xs = x_sample.reshape(Bd, D)
    lam = (lambda_q1, lambda_k1, lambda_q2, lambda_k2)
    TM = 1024
    w_in, w_pa, w_pb, w_out = (w.astype(BF16) for w in (w_in, w_pa, w_pb, w_out))

    q, k, k_bf, v, v_bf, hg, gates = _in_projection(xp, norm_mix[0], w_in, jnp.arange(S, dtype=I32), TM, 512, "p",
                                                    BF16)
    oa = _attention_prompt(q, k_bf, v_bf, *lam, subln, ATTN_TQ, ATTN_TK)
    oh, st_p = _hgrn_prompt(hg, hgrn_lb, hgrn_norm, 512)
    merged = _merge(oa, oh, w_pa, w_pb, gates, TM, 512, "p")
    (h1_p,) = _mm(merged, w_out, col_off=0, n_cols=D, tm=TM, tn=512, epilogue=_epi_residual, out_dtypes=[F32],
                  extras=[(xp, pl.BlockSpec((TM, 512), lambda i, j: (i, j)))], name="outproj_p")

    pos_s = jnp.full((Bd,), past_len, I32)
    qs, ks, ks_bf, vs, vs_bf, hgs, gates_s = _in_projection(xs, norm_mix[0], w_in, pos_s, Bd, 512, "s", F32)
    ck = jnp.transpose(cache_k[0], (0, 2, 3, 4, 1)).reshape(cache_k.shape[1], ATT_WIDTH, page)
    cv = cache_v[0].reshape(cache_v.shape[1], page * ATT_HEADS, 2 * ATT_HEAD_DIM)
    oa_s = _attention_decode(qs, ks_bf, vs_bf, ck, cv, page_table, *lam, subln)
    oh_s, st_s = _hgrn_decode(hgs, state_hgrn[0], hgrn_lb, hgrn_norm)
    merged_s = _merge(oa_s, oh_s, w_pa, w_pb, gates_s, Bd, 512, "s")
    (h1_s,) = _mm(merged_s, w_out, col_off=0, n_cols=D, tm=Bd, tn=512, epilogue=_epi_residual, out_dtypes=[F32],
                  extras=[(xs, pl.BlockSpec((Bd, 512), lambda i, j: (i, j)))], name="outproj_s")

    mem = mem_prompt.reshape(-1, D)
    mk, mk_bf = _mm(mem, w_ck, col_off=0, n_cols=X_WIDTH, tm=mem.shape[0], tn=X_WIDTH, epilogue=_epi_v,
                    out_dtypes=[F32, BF16], name="mem_k")
    mv, mv_bf = _mm(mem, w_cv, col_off=0, n_cols=X_WIDTH, tm=mem.shape[0], tn=X_WIDTH, epilogue=_epi_v,
                    out_dtypes=[F32, BF16], name="mem_v")
    w_cq_bf = w_cq[0].astype(BF16)
    w_co_bf = w_co[0].astype(BF16)
    h2_p = _cross_prompt(h1_p, norm_cross[0], w_cq_bf, mk_bf, mv_bf, w_co_bf, 256)
    (qc_s,) = _mm(h1_s, w_cq, col_off=0, n_cols=X_WIDTH, tm=Bd, tn=X_WIDTH, epilogue=_epi_plain,
                  out_dtypes=[F32], norm_g=norm_cross[0], name="cross_q_s")
    oc_s = _cross_decode(qc_s, cache_mem_k[0], cache_mem_v[0])
    (h2_s,) = _mm(oc_s, w_co, col_off=0, n_cols=D, tm=Bd, tn=512, epilogue=_epi_residual, out_dtypes=[F32],
                  extras=[(h1_s, pl.BlockSpec((Bd, 512), lambda i, j: (i, j)))], name="cross_o_s")

    y_p, y_s = _moe_and_final_norm(h2_p, h2_s, norm_ffn[0], w_router_group[0], b_router_group[0],
                                   w_router_expert[0], b_router_expert[0], w_e_gate, w_e_up, w_e_down,
                                   norm_final, 256)

    return (y_p.reshape(Bp, S, D), y_s.reshape(Bd, Ld, D),
            k.reshape(1, Bp, S, ATT_HEADS, 2, ATT_HEAD_DIM), v.reshape(1, Bp, S, ATT_HEADS, 2 * ATT_HEAD_DIM),
            ks.reshape(1, Bd, Ld, ATT_HEADS, 2, ATT_HEAD_DIM), vs.reshape(1, Bd, Ld, ATT_HEADS, 2 * ATT_HEAD_DIM),
            st_p.reshape(1, Bp, HG_HEADS, HG_KEY, HG_VAL), st_s.reshape(1, Bd, HG_HEADS, HG_KEY, HG_VAL),
            mk.reshape(1, Bp, -1, X_HEADS, X_HEAD_DIM), mv.reshape(1, Bp, -1, X_HEADS, X_HEAD_DIM))
```

```python
import functools
import math

import numpy as np
import jax
import jax.numpy as jnp
from jax import lax
from jax.experimental import pallas as pl
from jax.experimental.pallas import tpu as pltpu

F32 = jnp.float32
BF16 = jnp.bfloat16
I32 = jnp.int32

D_MODEL = 2048
ATT_HEADS = 8
ATT_HEAD_DIM = 64
ATT_WIDTH = ATT_HEADS * 2 * ATT_HEAD_DIM
ROPE_THETA = 10000.0
HG_HEADS = 8
HG_KEY = 128
HG_VAL = 128
HG_WIDTH = HG_HEADS * HG_VAL
X_HEADS = 4
X_HEAD_DIM = 128
X_WIDTH = X_HEADS * X_HEAD_DIM
N_GROUPS = 4
EXPERTS_PER_GROUP = 8
N_EXPERTS = N_GROUPS * EXPERTS_PER_GROUP
EXPERT_FF = 512
RMS_EPS = 1e-6
NEG_INF = -1e30
LAMBDA_INIT = 0.8 - 0.6 * math.exp(-0.3 * 0)

LANES = 128
SUBLANES = 8
VMEM_LIMIT = 52 * 1024 * 1024

HG_CHUNK = 64
HG_LEVELS = (16, 32, 64)
MOE_ROWS = 256
PAGES_PER_STEP = 16
ATTN_TQ = 1024
ATTN_TK = 1024

_NT = (((1,), (1,)), ((), ()))
_TN = (((0,), (0,)), ((), ()))


def _params(sem):
    return pltpu.CompilerParams(dimension_semantics=sem, vmem_limit_bytes=VMEM_LIMIT)


def _rms(x, g):
    return x * lax.rsqrt(jnp.mean(x * x, axis=-1, keepdims=True) + RMS_EPS) * g


def _mm_kernel(*refs, n_extra, n_out, norm, epilogue):
    x_ref = refs[0]
    pos = 1
    if norm:
        g_ref = refs[1]
        pos = 2
    w_ref = refs[pos]
    extras = refs[pos + 1:pos + 1 + n_extra]
    outs = refs[pos + 1 + n_extra:pos + 1 + n_extra + n_out]
    if norm:
        u_ref = refs[-1]

        @pl.when(pl.program_id(1) == 0)
        def _():
            u_ref[...] = _rms(x_ref[...].astype(F32), g_ref[...]).astype(BF16)

        u = u_ref[...]
    else:
        u = x_ref[...].astype(BF16)
    acc = jnp.dot(u, w_ref[...].astype(BF16), preferred_element_type=F32)
    epilogue(acc, extras, outs)


def _mm(x, w, *, col_off, n_cols, tm, tn, epilogue, out_dtypes, norm_g=None, extras=(), name):
    M, K = x.shape
    assert M % tm == 0 and n_cols % tn == 0 and col_off % tn == 0
    joff = col_off // tn
    if w.ndim == 3:
        w_spec = pl.BlockSpec((None, K, tn), lambda i, j: (0, 0, j + joff))
    else:
        w_spec = pl.BlockSpec((K, tn), lambda i, j: (0, j + joff))
    in_specs = [pl.BlockSpec((tm, K), lambda i, j: (i, 0))]
    args = [x]
    if norm_g is not None:
        in_specs.append(pl.BlockSpec((1, K), lambda i, j: (0, 0)))
        args.append(norm_g.reshape(1, K))
    in_specs.append(w_spec)
    args.append(w)
    for arr, spec in extras:
        in_specs.append(spec)
        args.append(arr)
    out_specs = [pl.BlockSpec((tm, tn), lambda i, j: (i, j)) for _ in out_dtypes]
    out_shape = [jax.ShapeDtypeStruct((M, n_cols), dt) for dt in out_dtypes]
    scratch = [pltpu.VMEM((tm, K), BF16)] if norm_g is not None else []
    kern = functools.partial(_mm_kernel, n_extra=len(extras), n_out=len(out_dtypes),
                             norm=norm_g is not None, epilogue=epilogue)
    res = pl.pallas_call(
        kern, grid=(M // tm, n_cols // tn), in_specs=in_specs, out_specs=out_specs,
        out_shape=out_shape, scratch_shapes=scratch,
        compiler_params=_params(("parallel", "arbitrary")), name=name)(*args)
    return res


def _rope_tile(x, cos, sin_signed):
    first = (lax.broadcasted_iota(I32, (x.shape[0], LANES), 1) % ATT_HEAD_DIM) < ATT_HEAD_DIM // 2
    outs = []
    for c in range(x.shape[1] // LANES):
        xc = x[:, c * LANES:(c + 1) * LANES]
        rot = jnp.where(first, pltpu.roll(xc, LANES - ATT_HEAD_DIM // 2, 1),
                        pltpu.roll(xc, ATT_HEAD_DIM // 2, 1))
        outs.append(xc * cos + rot * sin_signed)
    return outs[0] if len(outs) == 1 else jnp.concatenate(outs, axis=1)


def _epi_q(acc, extras, outs):
    cos_ref, sin_ref = extras
    outs[0][...] = (_rope_tile(acc, cos_ref[...], sin_ref[...]) * (ATT_HEAD_DIM ** -0.5)).astype(BF16)


def _epi_k(acc, extras, outs):
    cos_ref, sin_ref = extras
    r = _rope_tile(acc, cos_ref[...], sin_ref[...])
    outs[0][...] = r
    outs[1][...] = r.astype(BF16)


def _epi_v(acc, extras, outs):
    outs[0][...] = acc
    outs[1][...] = acc.astype(BF16)


def _epi_plain(acc, extras, outs):
    outs[0][...] = acc.astype(outs[0].dtype)


def _epi_residual(acc, extras, outs):
    outs[0][...] = extras[0][...] + acc


def _rope_tables(pos):
    half = ATT_HEAD_DIM // 2
    freqs = ROPE_THETA ** (-jnp.arange(half, dtype=F32) / half)
    ang = pos.astype(F32)[:, None] * freqs[None, :]
    c, s = jnp.cos(ang), jnp.sin(ang)
    return jnp.tile(c, (1, 4)), jnp.concatenate([-s, s, -s, s], axis=1)


def _inproj_kernel(x_ref, g_ref, w_ref, cos_ref, sin_ref, q_ref, k_ref, kb_ref, v_ref, vb_ref, hg_ref, gt_ref,
                   u_ref, *, bounds):
    j = pl.program_id(1)

    @pl.when(j == 0)
    def _():
        u_ref[...] = _rms(x_ref[...].astype(F32), g_ref[...]).astype(BF16)

    acc = jnp.dot(u_ref[...], w_ref[...], preferred_element_type=F32)
    in_seg = lambda s: (j >= bounds[s]) & (j < bounds[s + 1])

    @pl.when(in_seg(0))
    def _():
        _epi_q(acc, (cos_ref, sin_ref), (q_ref,))

    @pl.when(in_seg(1))
    def _():
        _epi_k(acc, (cos_ref, sin_ref), (k_ref, kb_ref))

    @pl.when(in_seg(2))
    def _():
        _epi_v(acc, (), (v_ref, vb_ref))

    @pl.when(in_seg(3))
    def _():
        hg_ref[...] = acc

    @pl.when(in_seg(4))
    def _():
        gt_ref[...] = acc.astype(gt_ref.dtype)


def _in_projection(x, g, w_in_bf, pos, tm, tn, tag, gate_dtype):
    M, K = x.shape
    cos, sin = _rope_tables(pos)
    widths = (ATT_WIDTH, ATT_WIDTH, ATT_WIDTH, 4 * HG_WIDTH, 2 * D_MODEL)
    bounds = tuple(int(b) for b in np.cumsum((0,) + widths) // tn)

    def seg_spec(s):
        lo, n = bounds[s], bounds[s + 1] - bounds[s]
        return pl.BlockSpec((tm, tn), lambda i, j: (i, jnp.clip(j - lo, 0, n - 1)))

    seg_of_out = (0, 1, 1, 2, 2, 3, 4)
    out_dtypes = (BF16, F32, BF16, F32, BF16, F32, gate_dtype)
    row_tbl = pl.BlockSpec((tm, LANES), lambda i, j: (i, 0))
    return pl.pallas_call(
        functools.partial(_inproj_kernel, bounds=bounds),
        grid=(M // tm, bounds[-1]),
        in_specs=[pl.BlockSpec((tm, K), lambda i, j: (i, 0), pipeline_mode=pl.Buffered(1)),
                  pl.BlockSpec((1, K), lambda i, j: (0, 0)),
                  pl.BlockSpec((None, K, tn), lambda i, j: (0, 0, j)), row_tbl, row_tbl],
        out_specs=[seg_spec(s) for s in seg_of_out],
        out_shape=[jax.ShapeDtypeStruct((M, widths[s]), dt) for s, dt in zip(seg_of_out, out_dtypes)],
        scratch_shapes=[pltpu.VMEM((tm, K), BF16)],
        compiler_params=_params(("parallel", "arbitrary")), name=f"inproj_{tag}",
    )(x, g.reshape(1, K), w_in_bf, cos, sin)


def _lambda_value(lq1, lk1, lq2, lk2):
    return (jnp.exp(jnp.sum(lq1[...] * lk1[...], axis=-1, keepdims=True))
            - jnp.exp(jnp.sum(lq2[...] * lk2[...], axis=-1, keepdims=True)) + LAMBDA_INIT)


def _attn_kernel(q_ref, k_ref, v_ref, lq1, lk1, lq2, lk2, subln_ref, o_ref, vt_scr, *, tq, tk):
    i = pl.program_id(1)
    S = v_ref.shape[0]

    @pl.when(i == 0)
    def _():
        for c in range(S // tk):
            vt_scr[:, c * tk:(c + 1) * tk] = v_ref[c * tk:(c + 1) * tk, :].astype(F32).T.astype(BF16)

    qt = q_ref[...].astype(F32).T
    sub = lax.broadcasted_iota(I32, qt.shape, 0)
    qm = (jnp.where(sub < ATT_HEAD_DIM, qt, 0.0).astype(BF16), jnp.where(sub >= ATT_HEAD_DIM, qt, 0.0).astype(BF16))

    def update(off, carry, mask):
        off = pl.multiple_of(off, tk)
        kt = k_ref[pl.ds(off, tk), :]
        vt = vt_scr[:, pl.ds(off, tk)]
        new = []
        for m in range(2):
            mx, l, acc = carry[3 * m:3 * m + 3]
            s = jnp.dot(kt, qm[m], preferred_element_type=F32)
            if mask is not None:
                s = jnp.where(mask, s, NEG_INF)
            mn = jnp.maximum(mx, jnp.max(s, axis=0, keepdims=True))
            alpha = jnp.exp(mx - mn)
            p = jnp.exp(s - mn)
            l = alpha * l + jnp.sum(p, axis=0, keepdims=True)
            acc = alpha * acc + jnp.dot(vt, p.astype(BF16), preferred_element_type=F32)
            new += [mn, l, acc]
        return tuple(new)

    def body(j, carry):
        return update(j * tk, carry, None)

    init = (jnp.full((1, tq), NEG_INF, F32), jnp.zeros((1, tq), F32), jnp.zeros((LANES, tq), F32)) * 2
    carry = lax.fori_loop(0, i * (tq // tk), body, init)
    key = lax.broadcasted_iota(I32, (tk, tq), 0)
    qry = lax.broadcasted_iota(I32, (tk, tq), 1)
    for d in range(tq // tk):
        carry = update(i * tq + d * tk, carry, key + d * tk <= qry)
    _, l0, a0, _, l1, a1 = carry

    lam = _lambda_value(lq1, lk1, lq2, lk2)
    o = (a0 / l0 - lam * (a1 / l1)).T
    o_ref[...] = (_rms(o, subln_ref[...]) * (1.0 - LAMBDA_INIT)).astype(BF16)


def _small_spec(shape):
    nd = len(shape)
    return pl.BlockSpec(shape, lambda *_: (0,) * nd)


def _attention_prompt(q, k_bf, v_bf, lq1, lk1, lq2, lk2, subln, tq, tk):
    S = q.shape[0]
    assert tq % tk == 0
    lam_specs = [_small_spec((1, ATT_HEAD_DIM))] * 4
    return pl.pallas_call(
        functools.partial(_attn_kernel, tq=tq, tk=tk),
        grid=(ATT_HEADS, S // tq),
        in_specs=[pl.BlockSpec((tq, LANES), lambda h, i: (i, h)),
                  pl.BlockSpec((S, LANES), lambda h, i: (0, h)),
                  pl.BlockSpec((S, LANES), lambda h, i: (0, h))] + lam_specs
                 + [_small_spec((1, LANES))],
        out_specs=pl.BlockSpec((tq, LANES), lambda h, i: (i, h)),
        out_shape=jax.ShapeDtypeStruct((S, ATT_WIDTH), BF16),
        scratch_shapes=[pltpu.VMEM((LANES, S), BF16)],
        compiler_params=_params(("parallel", "arbitrary")), name="attn_prompt",
    )(q, k_bf, v_bf, lq1, lk1, lq2, lk2, subln)


def _decode_attn_kernel(pt_ref, q_ref, kn_ref, vn_ref, lq1, lk1, lq2, lk2, subln_ref, *rest, n_pg):
    k_refs = rest[:n_pg]
    v_refs = rest[n_pg:2 * n_pg]
    o_ref = rest[2 * n_pg]
    qr_ref, s_scr, w_scr, wn_scr, acc_ref = rest[2 * n_pg + 1:]
    phase = pl.program_id(1)
    j = pl.program_id(2)
    last = pl.num_programs(2) - 1
    nrow = 2 * ATT_HEADS

    @pl.when((phase == 0) & (j == 0))
    def _():
        row = lax.broadcasted_iota(I32, (nrow, ATT_WIDTH), 0)
        lane = lax.broadcasted_iota(I32, (nrow, ATT_WIDTH), 1)
        sel = (lane // LANES == row % ATT_HEADS) & ((lane // ATT_HEAD_DIM) % 2 == row // ATT_HEADS)
        qb = jnp.broadcast_to(q_ref[...].astype(F32), (nrow, ATT_WIDTH))
        qr_ref[...] = jnp.where(sel, qb, 0.0).astype(BF16)

    @pl.when(phase == 0)
    def _():
        qr = qr_ref[...]
        s_scr[j] = jnp.concatenate(
            [jnp.dot(qr, kr[...].astype(BF16), preferred_element_type=F32) for kr in k_refs], axis=1)

    @pl.when((phase == 1) & (j == 0))
    def _():
        sn = jnp.sum(qr_ref[...].astype(F32) * kn_ref[...].astype(F32), axis=1, keepdims=True)
        s = s_scr[...]
        m = jnp.maximum(jnp.max(jnp.max(s, axis=2, keepdims=True), axis=0, keepdims=True), sn[None])
        e = jnp.exp(s - m)
        en = jnp.exp(sn[None] - m)
        l = jnp.sum(jnp.sum(e, axis=2, keepdims=True), axis=0, keepdims=True) + en
        p = e / l
        pn = (en / l)[0]
        lam = _lambda_value(lq1, lk1, lq2, lk2)
        w = p[:, :ATT_HEADS, :] - lam * p[:, ATT_HEADS:, :]
        w_scr[...] = jnp.concatenate([w, jnp.zeros_like(w)], axis=1).astype(BF16)
        wn_scr[...] = pn[:ATT_HEADS] - lam * pn[ATT_HEADS:]
        acc_ref[...] = jnp.zeros(acc_ref.shape, F32)

    @pl.when(phase == 1)
    def _():
        wb = w_scr[j]
        page = wb.shape[1] // n_pg
        heads = []
        for h in range(ATT_HEADS):
            vh = jnp.concatenate([vr[pl.ds(h, page, stride=ATT_HEADS), :].astype(BF16) for vr in v_refs], axis=0)
            heads.append(jnp.dot(wb, vh, preferred_element_type=F32))
        acc_ref[...] = acc_ref[...] + jnp.concatenate(heads, axis=1)

    @pl.when((phase == 1) & (j == last))
    def _():
        wn = wn_scr[...].astype(BF16).astype(F32)
        od = acc_ref[...][:ATT_HEADS] + wn * vn_ref[...].astype(F32)
        own = (lax.broadcasted_iota(I32, od.shape, 1) // LANES) == lax.broadcasted_iota(I32, od.shape, 0)
        od = jnp.where(own, od, 0.0)
        ms = jnp.sum(od * od, axis=1, keepdims=True) / (2 * ATT_HEAD_DIM)
        y = od * lax.rsqrt(ms + RMS_EPS) * subln_ref[...] * (1.0 - LAMBDA_INIT)
        o_ref[...] = jnp.sum(y, axis=0, keepdims=True).astype(BF16)


def _attention_decode(q, k_new_bf, v_new_bf, cache_kt, cache_v, page_table, lq1, lk1, lq2, lk2, subln):
    Bd = q.shape[0]
    page = cache_kt.shape[2]
    n_pages = page_table.shape[1]
    n_pg = math.gcd(n_pages, PAGES_PER_STEP)
    assert n_pages % n_pg == 0
    subln_w = jnp.tile(subln, (1, ATT_HEADS))

    n_groups = n_pages // n_pg
    nrow = 2 * ATT_HEADS

    def k_spec(g):
        return pl.BlockSpec((None, ATT_WIDTH, page), lambda b, ph, j, pt: (
            pt[b * n_pages + jnp.where(ph == 0, j, n_groups - 1) * n_pg + g], 0, 0))

    def v_spec(g):
        def index(b, ph, j, pt):
            held = jnp.maximum(b - 1, 0) * n_pages + (n_groups - 1) * n_pg
            return pt[jnp.where(ph == 0, held, b * n_pages + j * n_pg) + g], 0, 0
        return pl.BlockSpec((None, page * ATT_HEADS, 2 * ATT_HEAD_DIM), index)

    row_spec = pl.BlockSpec((None, 1, ATT_WIDTH), lambda b, ph, j, pt: (b, 0, 0))
    small = lambda shape: pl.BlockSpec(shape, lambda b, ph, j, pt: (0,) * len(shape))
    grid_spec = pltpu.PrefetchScalarGridSpec(
        num_scalar_prefetch=1, grid=(Bd, 2, n_groups),
        in_specs=[row_spec, row_spec, row_spec] + [small((1, ATT_HEAD_DIM))] * 4 + [small((1, ATT_WIDTH))]
                 + [k_spec(g) for g in range(n_pg)] + [v_spec(g) for g in range(n_pg)],
        out_specs=row_spec,
        scratch_shapes=[pltpu.VMEM((nrow, ATT_WIDTH), BF16),
                        pltpu.VMEM((n_groups, nrow, n_pg * page), F32),
                        pltpu.VMEM((n_groups, nrow, n_pg * page), BF16),
                        pltpu.VMEM((ATT_HEADS, 1), F32),
                        pltpu.VMEM((nrow, ATT_WIDTH), F32)])
    out = pl.pallas_call(
        functools.partial(_decode_attn_kernel, n_pg=n_pg), grid_spec=grid_spec,
        out_shape=jax.ShapeDtypeStruct((Bd, 1, ATT_WIDTH), BF16),
        compiler_params=_params(("parallel", "arbitrary", "arbitrary")), name="attn_decode",
    )(page_table.reshape(-1), q.reshape(Bd, 1, -1), k_new_bf.reshape(Bd, 1, -1), v_new_bf.reshape(Bd, 1, -1),
      lq1, lk1, lq2, lk2, subln_w, *([cache_kt] * n_pg), *([cache_v] * n_pg))
    return out.reshape(Bd, ATT_WIDTH)


def _hgrn_masks():
    C = HG_CHUNK
    t = np.arange(C)[:, None]
    r = np.arange(C)[None, :]
    blocks = [(r <= t), (r > t)]
    for B in HG_LEVELS:
        mid = (t // B) * B + B // 2 - 1
        second = (t % B) >= B // 2
        blocks.append(np.where(second, (r > mid) & (r <= t), (r > t) & (r <= mid)))
    return jnp.asarray(np.concatenate(blocks, axis=0).astype(np.float32), dtype=BF16)


def _lower_bound(lb_ref):
    a = lb_ref[...].astype(F32)
    e = jnp.exp(a - jnp.max(a, axis=0, keepdims=True))
    return e[0:1] / jnp.sum(e, axis=0, keepdims=True)


def _group_rows(x, j):
    return jnp.broadcast_to(x[:, j:j + 1, :], x.shape)


def _split3(x):
    hi = x.astype(BF16)
    r1 = x - hi.astype(F32)
    mid = r1.astype(BF16)
    lo = (r1 - mid.astype(F32)).astype(BF16)
    return hi, mid, lo


def _hgrn_tile(q, z, v, lb, msk, s_t):
    C = HG_CHUNK
    n = q.shape[0] // C
    dot = lambda a, b: jnp.dot(a, b, preferred_element_type=F32)
    rows = lambda x, c: x[c * C:(c + 1) * C]
    chunks = range(n)

    logf = jnp.log(lb + (1.0 - lb) * jax.nn.sigmoid(z))
    kk = (1.0 - lb) * jax.nn.sigmoid(-z)
    v_bf = v.astype(BF16)
    parts = _split3(logf)
    e = [sum(dot(msk, rows(p, c)) for p in parts) for c in chunks]
    b = [ec[0:C] for ec in e]
    qi = [(rows(q, c) * jnp.exp(b[c])).astype(BF16) for c in chunks]
    kl = [(rows(kk, c) * jnp.exp(e[c][C:2 * C])).astype(BF16) for c in chunks]
    kv = [lax.dot_general(rows(v_bf, c), kl[c], _TN, preferred_element_type=F32) for c in chunks]

    t_idx = lax.broadcasted_iota(I32, (C, 1), 0)
    row = lax.broadcasted_iota(I32, (C, C), 0)
    col = lax.broadcasted_iota(I32, (C, C), 1)
    a = [jnp.zeros((C, C), F32) for _ in chunks]
    for li, B in enumerate(HG_LEVELS):
        second = (t_idx % B) >= B // 2
        same_block = (row // B) == (col // B)
        for c in chunks:
            x = jnp.exp(e[c][(2 + li) * C:(3 + li) * C])
            qt = jnp.where(second, rows(q, c) * x, 0.0).astype(BF16)
            kt = jnp.where(second, 0.0, rows(kk, c) * x).astype(BF16)
            al = lax.dot_general(qt, kt, _NT, preferred_element_type=F32)
            a[c] = a[c] + (jnp.where(same_block, al, 0.0) if B < C else al)

    states = [s_t]
    for c in chunks:
        states.append(states[-1] * jnp.exp(b[c][C - 1:C, :]) + kv[c])
    o = [lax.dot_general(qi[c], states[c].astype(BF16), _NT, preferred_element_type=F32)
         + dot(a[c].astype(BF16), rows(v_bf, c)) for c in chunks]

    G = n * C // SUBLANES
    b_all = jnp.concatenate(b, axis=0) if n > 1 else b[0]
    q3, k3, v3, b3 = (x.reshape(G, SUBLANES, LANES) for x in (q, kk, v, b_all))
    p3 = lax.broadcasted_iota(I32, (G, SUBLANES, 1), 1)
    o3 = (jnp.concatenate(o, axis=0) if n > 1 else o[0]).reshape(G, SUBLANES, LANES)
    for j in range(SUBLANES):
        term = q3 * _group_rows(k3, j) * jnp.exp(jnp.minimum(b3 - _group_rows(b3, j), 0.0))
        aj = jnp.where(p3 >= j, jnp.sum(term, axis=-1, keepdims=True), 0.0)
        o3 = o3 + aj * _group_rows(v3, j)
    return o3.reshape(n * C, LANES), states[-1]


def _hgrn_kernel(q_ref, z_ref, v_ref, g_ref, lb_ref, hgn_ref, msk_ref, o_ref, st_ref, s_scr):
    i = pl.program_id(1)

    @pl.when(i == 0)
    def _():
        s_scr[...] = jnp.zeros(s_scr.shape, F32)

    o, s_t = _hgrn_tile(q_ref[...], z_ref[...], v_ref[...], _lower_bound(lb_ref), msk_ref[...], s_scr[...])
    s_scr[...] = s_t
    g = g_ref[...]
    o_ref[...] = (_rms(o, hgn_ref[...]) * (g * jax.nn.sigmoid(g))).astype(BF16)

    @pl.when(i == pl.num_programs(1) - 1)
    def _():
        st_ref[...] = s_scr[...].T


def _hgrn_prompt(hg, hgrn_lb, hgrn_norm, tt):
    S = hg.shape[0]
    msk = _hgrn_masks()
    col = lambda seg: pl.BlockSpec((tt, LANES), lambda h, i, seg=seg: (i, seg * HG_HEADS + h))
    o, st = pl.pallas_call(
        _hgrn_kernel, grid=(HG_HEADS, S // tt),
        in_specs=[col(0), col(1), col(2), col(3),
                  pl.BlockSpec((hgrn_lb.shape[0], LANES), lambda h, i: (0, h)),
                  _small_spec((1, HG_VAL)), _small_spec(tuple(msk.shape))],
        out_specs=[pl.BlockSpec((tt, LANES), lambda h, i: (i, h)),
                   pl.BlockSpec((None, HG_KEY, HG_VAL), lambda h, i: (h, 0, 0))],
        out_shape=[jax.ShapeDtypeStruct((S, HG_WIDTH), BF16),
                   jax.ShapeDtypeStruct((HG_HEADS, HG_KEY, HG_VAL), F32)],
        scratch_shapes=[pltpu.VMEM((HG_VAL, HG_KEY), F32)],
        compiler_params=_params(("parallel", "arbitrary")), name="hgrn_prompt",
    )(hg, hg, hg, hg, hgrn_lb, hgrn_norm, msk)
    return o, st


def _hgrn_decode_kernel(hg_ref, s_ref, lb_ref, hgn_ref, o_ref, sn_ref):
    W = HG_WIDTH
    row = hg_ref[...]
    q, z, v, g = (row[:, s * W:(s + 1) * W] for s in range(4))
    lb = _lower_bound(lb_ref)
    f = lb + (1.0 - lb) * jax.nn.sigmoid(z)
    kk = (1.0 - lb) * jax.nn.sigmoid(-z)
    rnd = lambda x: x.astype(BF16).astype(F32)
    qf = rnd(q * f)
    qk = rnd(q) * rnd(kk)
    pad = jnp.zeros((SUBLANES - 3, LANES), F32)
    outs = []
    for h in range(HG_HEADS):
        hs = slice(h * LANES, (h + 1) * LANES)
        cols = jnp.concatenate([f[:, hs], kk[:, hs], qf[:, hs], pad], axis=0).T
        f_c, k_c, qf_c = cols[:, 0:1], cols[:, 1:2], cols[:, 2:3]
        s0 = s_ref[h]
        vh = v[:, hs]
        sn_ref[h] = f_c * s0 + k_c * vh
        o = (jnp.sum(qf_c * rnd(s0), axis=0, keepdims=True)
             + rnd(jnp.sum(qk[:, hs], axis=1, keepdims=True)) * rnd(vh))
        gh = g[:, hs]
        outs.append(_rms(o, hgn_ref[...]) * (gh * jax.nn.sigmoid(gh)))
    o_ref[...] = jnp.concatenate(outs, axis=1).astype(BF16)


def _hgrn_decode(hg, state, hgrn_lb, hgrn_norm):
    Bd = hg.shape[0]
    o, sn = pl.pallas_call(
        _hgrn_decode_kernel, grid=(Bd,),
        in_specs=[pl.BlockSpec((None, 1, 4 * HG_WIDTH), lambda b: (b, 0, 0)),
                  pl.BlockSpec((None, HG_HEADS, HG_KEY, HG_VAL), lambda b: (b, 0, 0, 0)),
                  _small_spec(tuple(hgrn_lb.shape)), _small_spec((1, HG_VAL))],
        out_specs=[pl.BlockSpec((None, 1, HG_WIDTH), lambda b: (b, 0, 0)),
                   pl.BlockSpec((None, HG_HEADS, HG_KEY, HG_VAL), lambda b: (b, 0, 0, 0))],
        out_shape=[jax.ShapeDtypeStruct((Bd, 1, HG_WIDTH), BF16),
                   jax.ShapeDtypeStruct(state.shape, F32)],
        compiler_params=_params(("parallel",)), name="hgrn_decode",
    )(hg.reshape(Bd, 1, -1), state, hgrn_lb, hgrn_norm)
    return o.reshape(Bd, HG_WIDTH), sn


def _merge_kernel(oa_ref, oh_ref, wa_ref, wb_ref, ga_ref, gb_ref, o_ref):
    a = jnp.dot(oa_ref[...], wa_ref[...].astype(BF16), preferred_element_type=F32)
    b = jnp.dot(oh_ref[...], wb_ref[...].astype(BF16), preferred_element_type=F32)
    ga = jax.nn.sigmoid(ga_ref[...].astype(F32))
    gb = jax.nn.sigmoid(gb_ref[...].astype(F32))
    o_ref[...] = (ga * a + gb * b).astype(BF16)


def _merge(oa, oh, w_pa, w_pb, gates, tm, tn, tag):
    M = oa.shape[0]
    nj = D_MODEL // tn
    return pl.pallas_call(
        _merge_kernel, grid=(M // tm, nj),
        in_specs=[pl.BlockSpec((tm, ATT_WIDTH), lambda i, j: (i, 0)),
                  pl.BlockSpec((tm, HG_WIDTH), lambda i, j: (i, 0)),
                  pl.BlockSpec((None, ATT_WIDTH, tn), lambda i, j: (0, 0, j)),
                  pl.BlockSpec((None, HG_WIDTH, tn), lambda i, j: (0, 0, j)),
                  pl.BlockSpec((tm, tn), lambda i, j: (i, j)),
                  pl.BlockSpec((tm, tn), lambda i, j: (i, j + nj))],
        out_specs=pl.BlockSpec((tm, tn), lambda i, j: (i, j)),
        out_shape=jax.ShapeDtypeStruct((M, D_MODEL), BF16),
        compiler_params=_params(("parallel", "arbitrary")), name=f"merge_{tag}",
    )(oa, oh, w_pa, w_pb, gates, gates)


def _cross_prompt_kernel(x_ref, g_ref, wq_ref, mk_ref, mv_ref, wo_ref, o_ref):
    x = x_ref[...]
    u = _rms(x, g_ref[...]).astype(BF16)
    q = jnp.dot(u, wq_ref[...], preferred_element_type=F32).astype(BF16)
    heads = []
    for h in range(X_HEADS):
        hs = slice(h * X_HEAD_DIM, (h + 1) * X_HEAD_DIM)
        s = lax.dot_general(q[:, hs], mk_ref[:, hs], _NT, preferred_element_type=F32) * (X_HEAD_DIM ** -0.5)
        p = jnp.exp(s - jnp.max(s, axis=1, keepdims=True))
        p = p / jnp.sum(p, axis=1, keepdims=True)
        heads.append(jnp.dot(p.astype(BF16), mv_ref[:, hs], preferred_element_type=F32))
    o = jnp.concatenate(heads, axis=1).astype(BF16)
    o_ref[...] = x + jnp.dot(o, wo_ref[...], preferred_element_type=F32)


def _cross_prompt(h1, g_cross, w_cq_bf, mk_bf, mv_bf, w_co_bf, tm):
    M = h1.shape[0]
    return pl.pallas_call(
        _cross_prompt_kernel, grid=(M // tm,),
        in_specs=[pl.BlockSpec((tm, D_MODEL), lambda i: (i, 0)), _small_spec((1, D_MODEL)),
                  _small_spec((D_MODEL, X_WIDTH)), _small_spec(tuple(mk_bf.shape)),
                  _small_spec(tuple(mv_bf.shape)), _small_spec((X_WIDTH, D_MODEL))],
        out_specs=pl.BlockSpec((tm, D_MODEL), lambda i: (i, 0)),
        out_shape=jax.ShapeDtypeStruct((M, D_MODEL), F32),
        compiler_params=_params(("parallel",)), name="cross_prompt",
    )(h1, g_cross.reshape(1, -1), w_cq_bf, mk_bf, mv_bf, w_co_bf)


def _cross_decode_kernel(q_ref, mk_ref, mv_ref, o_ref):
    rnd = lambda x: x.astype(BF16).astype(F32)
    q = rnd(q_ref[...])
    mem = mk_ref.shape[0] // X_HEADS
    outs = []
    for h in range(X_HEADS):
        hs = slice(h * X_HEAD_DIM, (h + 1) * X_HEAD_DIM)
        rows = pl.ds(h, mem, stride=X_HEADS)
        s = jnp.sum(rnd(mk_ref[rows, :]) * q[:, hs], axis=1, keepdims=True) * (X_HEAD_DIM ** -0.5)
        p = jnp.exp(s - jnp.max(s, axis=0, keepdims=True))
        p = p / jnp.sum(p, axis=0, keepdims=True)
        outs.append(jnp.sum(rnd(p) * rnd(mv_ref[rows, :]), axis=0, keepdims=True))
    o_ref[...] = jnp.concatenate(outs, axis=1).astype(BF16)


def _cross_decode(q, mem_k, mem_v):
    Bd, mem = mem_k.shape[0], mem_k.shape[1]
    row = pl.BlockSpec((None, 1, X_WIDTH), lambda b: (b, 0, 0))
    mem_spec = pl.BlockSpec((None, mem * X_HEADS, X_HEAD_DIM), lambda b: (b, 0, 0))
    flat = lambda m: m.reshape(Bd, mem * X_HEADS, X_HEAD_DIM)
    out = pl.pallas_call(
        _cross_decode_kernel, grid=(Bd,), in_specs=[row, mem_spec, mem_spec], out_specs=row,
        out_shape=jax.ShapeDtypeStruct((Bd, 1, X_WIDTH), BF16),
        compiler_params=_params(("parallel",)), name="cross_decode",
    )(q.reshape(Bd, 1, X_WIDTH), flat(mem_k), flat(mem_v))
    return out.reshape(Bd, X_WIDTH)


def _route_kernel(x_ref, g_ref, w_ref, b_ref, cin_ref, uin_ref, u_ref, eid_ref, wt_ref, rank_ref, cnt_ref,
                  c_scr, *, n_rows):
    i = pl.program_id(0)
    tm = x_ref.shape[0]

    @pl.when(i == 0)
    def _():
        c_scr[...] = cin_ref[...]

    u = _rms(x_ref[...], g_ref[...])
    u_ref[...] = u
    dot = lambda a, b: jnp.dot(a, b, preferred_element_type=F32)
    logits = dot(u.astype(BF16), w_ref[...]) + b_ref[...]

    lane = lax.broadcasted_iota(I32, (tm, LANES), 1)
    big = jnp.int32(1 << 20)
    first_max = lambda vals, vmax: jnp.min(jnp.where(vals == vmax, lane, big), axis=1, keepdims=True)
    glv = jnp.where(lane < N_GROUPS, logits, -jnp.inf)
    gmax = jnp.max(glv, axis=1, keepdims=True)
    gsel = first_max(glv, gmax)
    p_group = 1.0 / jnp.sum(jnp.exp(glv - gmax), axis=1, keepdims=True)
    e_lane = lane - N_GROUPS
    in_group = (e_lane >= 0) & (e_lane < N_EXPERTS) & (e_lane // EXPERTS_PER_GROUP == gsel)
    ev = jnp.where(in_group, logits, -jnp.inf)
    v1 = jnp.max(ev, axis=1, keepdims=True)
    i1 = first_max(ev, v1)
    ev2 = jnp.where(lane == i1, -jnp.inf, ev)
    v2 = jnp.max(ev2, axis=1, keepdims=True)
    i2 = first_max(ev2, v2)
    t2 = jnp.exp(v2 - v1)
    w1 = p_group / (1.0 + t2)
    w2 = w1 * t2
    e1 = i1 - N_GROUPS
    e2 = i2 - N_GROUPS
    eid_ref[...] = jnp.where(lane == 0, e1, jnp.where(lane == 1, e2, 0))
    wt_ref[...] = jnp.where(lane == 0, w1, jnp.where(lane == 1, w2, 0.0))

    valid = (lax.broadcasted_iota(I32, (tm, 1), 0) + i * tm) < n_rows
    oh1 = jnp.where((lane == e1) & valid, 1.0, 0.0)
    oh2 = jnp.where((lane == e2) & valid, 1.0, 0.0)
    cnt = (oh1 + oh2).astype(BF16)
    r_i = lax.broadcasted_iota(I32, (tm, tm), 0)
    c_i = lax.broadcasted_iota(I32, (tm, tm), 1)
    strict = jnp.where(c_i < r_i, 1.0, 0.0).astype(BF16)
    before = dot(strict, cnt) + c_scr[...]
    r1 = jnp.sum(oh1 * before, axis=1, keepdims=True)
    r2 = jnp.sum(oh2 * before, axis=1, keepdims=True)
    rank_ref[...] = jnp.where(lane == 0, r1, jnp.where(lane == 1, r2, 0.0)).astype(I32)
    c_scr[...] = c_scr[...] + jnp.sum(oh1 + oh2, axis=0, keepdims=True)
    cnt_ref[...] = c_scr[...]


def _route(x, g_ffn, w_router, b_router, counts_in, u_all, row_off, tm, tag):
    M = x.shape[0]
    T = u_all.shape[0]
    boff = row_off // tm
    kern = functools.partial(_route_kernel, n_rows=M)
    lane_out = lambda dt: jax.ShapeDtypeStruct((M, LANES), dt)
    outs = pl.pallas_call(
        kern, grid=(M // tm,),
        in_specs=[pl.BlockSpec((tm, D_MODEL), lambda i: (i, 0)), _small_spec((1, D_MODEL)),
                  _small_spec((D_MODEL, LANES)), _small_spec((1, LANES)), _small_spec((1, LANES)),
                  pl.BlockSpec(memory_space=pl.ANY)],
        out_specs=[pl.BlockSpec((tm, D_MODEL), lambda i: (i + boff, 0)),
                   pl.BlockSpec((tm, LANES), lambda i: (i, 0)), pl.BlockSpec((tm, LANES), lambda i: (i, 0)),
                   pl.BlockSpec((tm, LANES), lambda i: (i, 0)), _small_spec((1, LANES))],
        out_shape=[jax.ShapeDtypeStruct((T, D_MODEL), F32), lane_out(I32), lane_out(F32), lane_out(I32),
                   jax.ShapeDtypeStruct((1, LANES), F32)],
        scratch_shapes=[pltpu.VMEM((1, LANES), F32)],
        input_output_aliases={5: 0},
        compiler_params=_params(("arbitrary",)), name=f"moe_route_{tag}",
    )(x, g_ffn.reshape(1, -1), w_router, b_router, counts_in, u_all)
    return outs


def _plan_kernel(cnt_ref, pstart_ref, blkexp_ref, blkslot_ref, nxtexp_ref, nblk_ref, *, n_blocks):
    def per_expert(e, carry):
        acc, k = carry
        pstart_ref[e] = acc
        nb = (cnt_ref[e] + MOE_ROWS - 1) // MOE_ROWS
        first = acc // MOE_ROWS

        def fill(b, c):
            blkexp_ref[b] = e
            blkslot_ref[b] = k % 2
            return c

        lax.fori_loop(first, first + nb, fill, 0)
        return acc + nb * MOE_ROWS, k + jnp.where(nb > 0, 1, 0)

    total, _ = lax.fori_loop(0, N_EXPERTS, per_expert, (jnp.int32(0), jnp.int32(0)))
    used = total // MOE_ROWS
    nblk_ref[0] = used
    last = jnp.maximum(used - 1, 0)

    def backwards(t, carry):
        cur, nxt = carry
        b = last - t
        e = blkexp_ref[b]
        nxt = jnp.where(e != cur, cur, nxt)
        nxtexp_ref[b] = nxt
        return e, nxt

    lax.fori_loop(0, used, backwards, (blkexp_ref[last], jnp.int32(-1)))

    def tail(b, c):
        blkexp_ref[b] = blkexp_ref[last]
        blkslot_ref[b] = blkslot_ref[last]
        nxtexp_ref[b] = -1
        return c

    lax.fori_loop(used, n_blocks, tail, 0)


def _plan(counts, n_blocks):
    smem = pl.BlockSpec(memory_space=pltpu.SMEM)
    per_block = jax.ShapeDtypeStruct((n_blocks,), I32)
    return pl.pallas_call(
        functools.partial(_plan_kernel, n_blocks=n_blocks),
        in_specs=[smem], out_specs=[smem] * 5,
        out_shape=[jax.ShapeDtypeStruct((N_EXPERTS,), I32), per_block, per_block, per_block,
                   jax.ShapeDtypeStruct((1,), I32)],
        name="moe_plan",
    )(counts)


def _dispatch_kernel(eid_ref, rank_ref, pstart_ref, cnt_ref, nblk_ref, u_ref, xs_hbm, zbuf, sem_z, sem,
                     *, n_tokens, n_blocks):
    R = MOE_ROWS
    i = pl.program_id(0)
    tm = u_ref.shape[0]

    def zero_copy(b):
        return pltpu.make_async_copy(zbuf, xs_hbm.at[pl.ds(pl.multiple_of(b * R, R), R), :], sem_z)

    def partial_block(e):
        c = cnt_ref[e]
        return (c % R) != 0, (pstart_ref[e] + c) // R

    def zero_partial(start):
        def body(e, n):
            has, b = partial_block(e)

            @pl.when(has)
            def _():
                zero_copy(b).start() if start else zero_copy(b).wait()

            return n
        return body

    def zero_unused(start):
        def body(b, n):
            zero_copy(b).start() if start else zero_copy(b).wait()
            return n
        return body

    @pl.when(i == 0)
    def _():
        zbuf[...] = jnp.zeros(zbuf.shape, F32)
        for start in (True, False):
            lax.fori_loop(0, N_EXPERTS, zero_partial(start), 0)
            lax.fori_loop(nblk_ref[0], n_blocks, zero_unused(start), 0)

    def row_copy(r, k):
        a = 2 * (i * tm + r) + k
        d = pstart_ref[eid_ref[a]] + rank_ref[a]
        return pltpu.make_async_copy(u_ref.at[pl.ds(r, 1), :], xs_hbm.at[pl.ds(d, 1), :], sem)

    def start_row(r, n):
        row_copy(r, 0).start()
        row_copy(r, 1).start()
        return n

    def wait_row(r, n):
        row_copy(r, 0).wait()
        row_copy(r, 1).wait()
        return n

    rem = n_tokens % tm

    @pl.when((i + 1) * tm <= n_tokens)
    def _():
        lax.fori_loop(0, tm, start_row, 0, unroll=8)
        lax.fori_loop(0, tm, wait_row, 0, unroll=8)

    if rem:
        @pl.when((i + 1) * tm > n_tokens)
        def _():
            lax.fori_loop(0, rem, start_row, 0, unroll=8)
            lax.fori_loop(0, rem, wait_row, 0, unroll=8)


def _dispatch(eid, rank, pstart, counts, n_used, u_all, n_blocks, tm):
    T = u_all.shape[0]
    grid_spec = pltpu.PrefetchScalarGridSpec(
        num_scalar_prefetch=5, grid=(pl.cdiv(T, tm),),
        in_specs=[pl.BlockSpec((tm, D_MODEL), lambda i, *_: (i, 0))],
        out_specs=pl.BlockSpec(memory_space=pl.ANY),
        scratch_shapes=[pltpu.VMEM((MOE_ROWS, D_MODEL), F32), pltpu.SemaphoreType.DMA(()),
                        pltpu.SemaphoreType.DMA(())])
    return pl.pallas_call(
        functools.partial(_dispatch_kernel, n_tokens=T, n_blocks=n_blocks), grid_spec=grid_spec,
        out_shape=jax.ShapeDtypeStruct((n_blocks * MOE_ROWS, D_MODEL), F32),
        compiler_params=_params(("arbitrary",)), name="moe_dispatch",
    )(eid, rank, pstart, counts, n_used, u_all)


def _expert_kernel(blkexp_ref, blkslot_ref, nxtexp_ref, nblk_ref, x_ref, wg_hbm, wu_hbm, wd_hbm, o_ref,
                   wg_f, wu_f, wd_f, wg_s, wu_s, wd_s, sem):
    i = pl.program_id(0)

    def fetch(e, slot):
        return [pltpu.make_async_copy(src.at[0, e], dst.at[slot], sem.at[slot])
                for src, dst in ((wg_hbm, wg_f), (wu_hbm, wu_f), (wd_hbm, wd_f))]

    @pl.when(i < nblk_ref[0])
    def _():
        e = blkexp_ref[i]
        slot = blkslot_ref[i]
        nxt = nxtexp_ref[i]

        @pl.when(i == 0)
        def _():
            for cp in fetch(e, slot):
                cp.start()

        @pl.when((i == 0) | (e != blkexp_ref[jnp.maximum(i - 1, 0)]))
        def _():
            for cp in fetch(e, slot):
                cp.wait()

            @pl.when(nxt >= 0)
            def _():
                for cp in fetch(nxt, 1 - slot):
                    cp.start()

            wg_s[...] = wg_f[slot].astype(BF16)
            wu_s[...] = wu_f[slot].astype(BF16)
            wd_s[...] = wd_f[slot].astype(BF16)

        x = x_ref[...].astype(BF16)
        hg = jnp.dot(x, wg_s[...], preferred_element_type=F32)
        hu = jnp.dot(x, wu_s[...], preferred_element_type=F32)
        h = (hg * jax.nn.sigmoid(hg) * hu).astype(BF16)
        o_ref[...] = jnp.dot(h, wd_s[...], preferred_element_type=F32)

    @pl.when(i >= nblk_ref[0])
    def _():
        o_ref[...] = jnp.zeros(o_ref.shape, F32)


def _experts(xs, blk_exp, blk_slot, nxt_exp, n_used, we_g, we_u, we_d, n_blocks):
    hbm = pl.BlockSpec(memory_space=pl.ANY)
    up, down = (D_MODEL, EXPERT_FF), (EXPERT_FF, D_MODEL)
    grid_spec = pltpu.PrefetchScalarGridSpec(
        num_scalar_prefetch=4, grid=(n_blocks,),
        in_specs=[pl.BlockSpec((MOE_ROWS, D_MODEL),
                               lambda i, be, bs, nx, nb: (jnp.maximum(jnp.minimum(i, nb[0] - 1), 0), 0)),
                  hbm, hbm, hbm],
        out_specs=pl.BlockSpec((MOE_ROWS, D_MODEL), lambda i, *_: (i, 0)),
        scratch_shapes=[pltpu.VMEM((2,) + up, F32), pltpu.VMEM((2,) + up, F32), pltpu.VMEM((2,) + down, F32),
                        pltpu.VMEM(up, BF16), pltpu.VMEM(up, BF16), pltpu.VMEM(down, BF16),
                        pltpu.SemaphoreType.DMA((2,))])
    return pl.pallas_call(
        _expert_kernel, grid_spec=grid_spec,
        out_shape=jax.ShapeDtypeStruct((n_blocks * MOE_ROWS, D_MODEL), F32),
        compiler_params=_params(("arbitrary",)), name="moe_experts",
    )(blk_exp, blk_slot, nxt_exp, n_used, xs, we_g, we_u, we_d)


def _combine_kernel(eid_ref, rank_ref, pstart_ref, h_ref, wt_ref, gf_ref, yb_hbm, o_ref, ybuf0, ybuf1, sem,
                    *, dest_off):
    i = pl.program_id(0)
    tm = h_ref.shape[0]
    base = dest_off + i * (2 * tm)

    def row_copy(r, k, buf):
        a = base + 2 * r + k
        d = pstart_ref[eid_ref[a]] + rank_ref[a]
        return pltpu.make_async_copy(yb_hbm.at[pl.ds(d, 1), :], buf.at[pl.ds(r, 1), :], sem)

    def start(r, c):
        row_copy(r, 0, ybuf0).start()
        row_copy(r, 1, ybuf1).start()
        return c

    def wait(r, c):
        row_copy(r, 0, ybuf0).wait()
        row_copy(r, 1, ybuf1).wait()
        return c

    lax.fori_loop(0, tm, start, 0, unroll=4)
    lax.fori_loop(0, tm, wait, 0, unroll=4)
    wt = wt_ref[...]
    h = h_ref[...] + (wt[:, 0:1] * ybuf0[...] + wt[:, 1:2] * ybuf1[...])
    o_ref[...] = _rms(h, gf_ref[...])


def _combine(h2, wts, eid, rank, pstart, yb, norm_final, dest_off, tm, tag):
    M = h2.shape[0]
    grid_spec = pltpu.PrefetchScalarGridSpec(
        num_scalar_prefetch=3, grid=(M // tm,),
        in_specs=[pl.BlockSpec((tm, D_MODEL), lambda i, *_: (i, 0)), pl.BlockSpec((tm, LANES), lambda i, *_: (i, 0)),
                  pl.BlockSpec((1, D_MODEL), lambda i, *_: (0, 0)), pl.BlockSpec(memory_space=pl.ANY)],
        out_specs=pl.BlockSpec((tm, D_MODEL), lambda i, *_: (i, 0)),
        scratch_shapes=[pltpu.VMEM((tm, D_MODEL), F32), pltpu.VMEM((tm, D_MODEL), F32),
                        pltpu.SemaphoreType.DMA(())])
    return pl.pallas_call(
        functools.partial(_combine_kernel, dest_off=dest_off), grid_spec=grid_spec,
        out_shape=jax.ShapeDtypeStruct((M, D_MODEL), F32),
        compiler_params=_params(("arbitrary",)), name=f"moe_combine_{tag}",
    )(eid, rank, pstart, h2, wts, norm_final.reshape(1, -1), yb)


def _moe_and_final_norm(h2_p, h2_s, g_ffn, wr_g, br_g, wr_e, br_e, we_g, we_u, we_d, norm_final, tm_p):
    Tp, Ts = h2_p.shape[0], h2_s.shape[0]
    T = Tp + Ts
    pad = LANES - N_GROUPS - N_EXPERTS
    w_router = jnp.concatenate([wr_g, wr_e, jnp.zeros((D_MODEL, pad), F32)], axis=1).astype(BF16)
    b_router = jnp.concatenate([br_g, br_e, jnp.zeros((pad,), F32)]).reshape(1, LANES)
    u_all = jnp.zeros((T, D_MODEL), F32)
    zero_counts = jnp.zeros((1, LANES), F32)
    u_all, eid_p, wt_p, rank_p, counts = _route(h2_p, g_ffn, w_router, b_router, zero_counts, u_all, 0, tm_p, "p")
    u_all, eid_s, wt_s, rank_s, counts = _route(h2_s, g_ffn, w_router, b_router, counts, u_all, Tp, Ts, "s")
    eid = jnp.concatenate([eid_p[:, :2].reshape(-1), eid_s[:, :2].reshape(-1)])
    rank = jnp.concatenate([rank_p[:, :2].reshape(-1), rank_s[:, :2].reshape(-1)])
    A = 2 * T
    n_blocks = (A + N_EXPERTS * (MOE_ROWS - 1)) // MOE_ROWS + 1
    cnt = counts[0, :N_EXPERTS].astype(I32)
    pstart, blk_exp, blk_slot, nxt_exp, n_used = _plan(cnt, n_blocks)
    xs = _dispatch(eid, rank, pstart, cnt, n_used, u_all, n_blocks, 256)
    yb = _experts(xs, blk_exp, blk_slot, nxt_exp, n_used, we_g, we_u, we_d, n_blocks)
    y_p = _combine(h2_p, wt_p, eid, rank, pstart, yb, norm_final, 0, 256, "p")
    y_s = _combine(h2_s, wt_s, eid, rank, pstart, yb, norm_final, 2 * Tp, Ts, "s")
    return y_p, y_s


def kernel(x_prompt, x_sample, mem_prompt, cache_k, cache_v, cache_mem_k, cache_mem_v, state_hgrn, page_table,
           norm_mix, w_in, lambda_q1, lambda_k1, lambda_q2, lambda_k2, subln, hgrn_lb, hgrn_norm, w_pa, w_pb,
           w_out, norm_cross, w_cq, w_ck, w_cv, w_co, norm_ffn, w_router_group, b_router_group,
           w_router_expert, b_router_expert, w_e_gate, w_e_up, w_e_down, norm_final):
    assert w_in.shape[0] == 1, "single-layer step"
    Bp, S, D = x_prompt.shape
    Bd, Ld, _ = x_sample.shape
    assert Bp == 1 and Ld == 1
    n_pages = page_table.shape[1]
    page = cache_k.shape[2]
    past_len = n_pages * page
    xp = x_prompt.reshape(S, D)
    xs = x_sample.reshape(Bd, D)
    lam = (lambda_q1, lambda_k1, lambda_q2, lambda_k2)
    TM = 1024
    w_in, w_pa, w_pb, w_out = (w.astype(BF16) for w in (w_in, w_pa, w_pb, w_out))

    q, k, k_bf, v, v_bf, hg, gates = _in_projection(xp, norm_mix[0], w_in, jnp.arange(S, dtype=I32), TM, 512, "p",
                                                    BF16)
    oa = _attention_prompt(q, k_bf, v_bf, *lam, subln, ATTN_TQ, ATTN_TK)
    oh, st_p = _hgrn_prompt(hg, hgrn_lb, hgrn_norm, 512)
    merged = _merge(oa, oh, w_pa, w_pb, gates, TM, 512, "p")
    (h1_p,) = _mm(merged, w_out, col_off=0, n_cols=D, tm=TM, tn=512, epilogue=_epi_residual, out_dtypes=[F32],
                  extras=[(xp, pl.BlockSpec((TM, 512), lambda i, j: (i, j)))], name="outproj_p")

    pos_s = jnp.full((Bd,), past_len, I32)
    qs, ks, ks_bf, vs, vs_bf, hgs, gates_s = _in_projection(xs, norm_mix[0], w_in, pos_s, Bd, 512, "s", F32)
    ck = jnp.transpose(cache_k[0], (0, 2, 3, 4, 1)).reshape(cache_k.shape[1], ATT_WIDTH, page)
    cv = cache_v[0].reshape(cache_v.shape[1], page * ATT_HEADS, 2 * ATT_HEAD_DIM)
    oa_s = _attention_decode(qs, ks_bf, vs_bf, ck, cv, page_table, *lam, subln)
    oh_s, st_s = _hgrn_decode(hgs, state_hgrn[0], hgrn_lb, hgrn_norm)
    merged_s = _merge(oa_s, oh_s, w_pa, w_pb, gates_s, Bd, 512, "s")
    (h1_s,) = _mm(merged_s, w_out, col_off=0, n_cols=D, tm=Bd, tn=512, epilogue=_epi_residual, out_dtypes=[F32],
                  extras=[(xs, pl.BlockSpec((Bd, 512), lambda i, j: (i, j)))], name="outproj_s")

    mem = mem_prompt.reshape(-1, D)
    mk, mk_bf = _mm(mem, w_ck, col_off=0, n_cols=X_WIDTH, tm=mem.shape[0], tn=X_WIDTH, epilogue=_epi_v,
                    out_dtypes=[F32, BF16], name="mem_k")
    mv, mv_bf = _mm(mem, w_cv, col_off=0, n_cols=X_WIDTH, tm=mem.shape[0], tn=X_WIDTH, epilogue=_epi_v,
                    out_dtypes=[F32, BF16], name="mem_v")
    w_cq_bf = w_cq[0].astype(BF16)
    w_co_bf = w_co[0].astype(BF16)
    h2_p = _cross_prompt(h1_p, norm_cross[0], w_cq_bf, mk_bf, mv_bf, w_co_bf, 256)
    (qc_s,) = _mm(h1_s, w_cq, col_off=0, n_cols=X_WIDTH, tm=Bd, tn=X_WIDTH, epilogue=_epi_plain,
                  out_dtypes=[F32], norm_g=norm_cross[0], name="cross_q_s")
    oc_s = _cross_decode(qc_s, cache_mem_k[0], cache_mem_v[0])
    (h2_s,) = _mm(oc_s, w_co, col_off=0, n_cols=D, tm=Bd, tn=512, epilogue=_epi_residual, out_dtypes=[F32],
                  extras=[(h1_s, pl.BlockSpec((Bd, 512), lambda i, j: (i, j)))], name="cross_o_s")

    y_p, y_s = _moe_and_final_norm(h2_p, h2_s, norm_ffn[0], w_router_group[0], b_router_group[0],
                                   w_router_expert[0], b_router_expert[0], w_e_gate, w_e_up, w_e_down,
                                   norm_final, 256)

    return (y_p.reshape(Bp, S, D), y_s.reshape(Bd, Ld, D),
            k.reshape(1, Bp, S, ATT_HEADS, 2, ATT_HEAD_DIM), v.reshape(1, Bp, S, ATT_HEADS, 2 * ATT_HEAD_DIM),
            ks.reshape(1, Bd, Ld, ATT_HEADS, 2, ATT_HEAD_DIM), vs.reshape(1, Bd, Ld, ATT_HEADS, 2 * ATT_HEAD_DIM),
            st_p.reshape(1, Bp, HG_HEADS, HG_KEY, HG_VAL), st_s.reshape(1, Bd, HG_HEADS, HG_KEY, HG_VAL),
            mk.reshape(1, Bp, -1, X_HEADS, X_HEAD_DIM), mv.reshape(1, Bp, -1, X_HEADS, X_HEAD_DIM))
```

```python
import functools
import math

import numpy as np
import jax
import jax.numpy as jnp
from jax import lax
from jax.experimental import pallas as pl
from jax.experimental.pallas import tpu as pltpu

F32 = jnp.float32
BF16 = jnp.bfloat16
I32 = jnp.int32

D_MODEL = 2048
ATT_HEADS = 8
ATT_HEAD_DIM = 64
ATT_WIDTH = ATT_HEADS * 2 * ATT_HEAD_DIM
ROPE_THETA = 10000.0
HG_HEADS = 8
HG_KEY = 128
HG_VAL = 128
HG_WIDTH = HG_HEADS * HG_VAL
X_HEADS = 4
X_HEAD_DIM = 128
X_WIDTH = X_HEADS * X_HEAD_DIM
N_GROUPS = 4
EXPERTS_PER_GROUP = 8
N_EXPERTS = N_GROUPS * EXPERTS_PER_GROUP
EXPERT_FF = 512
RMS_EPS = 1e-6
NEG_INF = -1e30
LAMBDA_INIT = 0.8 - 0.6 * math.exp(-0.3 * 0)

LANES = 128
SUBLANES = 8
VMEM_LIMIT = 52 * 1024 * 1024

HG_CHUNK = 64
HG_LEVELS = (16, 32, 64)
MOE_ROWS = 256
PAGES_PER_STEP = 16
ATTN_TQ = 1024
ATTN_TK = 1024

_NT = (((1,), (1,)), ((), ()))
_TN = (((0,), (0,)), ((), ()))


def _params(sem):
    return pltpu.CompilerParams(dimension_semantics=sem, vmem_limit_bytes=VMEM_LIMIT)


def _rms(x, g):
    return x * lax.rsqrt(jnp.mean(x * x, axis=-1, keepdims=True) + RMS_EPS) * g


def _mm_kernel(*refs, n_extra, n_out, norm, epilogue):
    x_ref = refs[0]
    pos = 1
    if norm:
        g_ref = refs[1]
        pos = 2
    w_ref = refs[pos]
    extras = refs[pos + 1:pos + 1 + n_extra]
    outs = refs[pos + 1 + n_extra:pos + 1 + n_extra + n_out]
    if norm:
        u_ref = refs[-1]

        @pl.when(pl.program_id(1) == 0)
        def _():
            u_ref[...] = _rms(x_ref[...].astype(F32), g_ref[...]).astype(BF16)

        u = u_ref[...]
    else:
        u = x_ref[...].astype(BF16)
    acc = jnp.dot(u, w_ref[...].astype(BF16), preferred_element_type=F32)
    epilogue(acc, extras, outs)


def _mm(x, w, *, col_off, n_cols, tm, tn, epilogue, out_dtypes, norm_g=None, extras=(), name):
    M, K = x.shape
    assert M % tm == 0 and n_cols % tn == 0 and col_off % tn == 0
    joff = col_off // tn
    if w.ndim == 3:
        w_spec = pl.BlockSpec((None, K, tn), lambda i, j: (0, 0, j + joff))
    else:
        w_spec = pl.BlockSpec((K, tn), lambda i, j: (0, j + joff))
    in_specs = [pl.BlockSpec((tm, K), lambda i, j: (i, 0))]
    args = [x]
    if norm_g is not None:
        in_specs.append(pl.BlockSpec((1, K), lambda i, j: (0, 0)))
        args.append(norm_g.reshape(1, K))
    in_specs.append(w_spec)
    args.append(w)
    for arr, spec in extras:
        in_specs.append(spec)
        args.append(arr)
    out_specs = [pl.BlockSpec((tm, tn), lambda i, j: (i, j)) for _ in out_dtypes]
    out_shape = [jax.ShapeDtypeStruct((M, n_cols), dt) for dt in out_dtypes]
    scratch = [pltpu.VMEM((tm, K), BF16)] if norm_g is not None else []
    kern = functools.partial(_mm_kernel, n_extra=len(extras), n_out=len(out_dtypes),
                             norm=norm_g is not None, epilogue=epilogue)
    res = pl.pallas_call(
        kern, grid=(M // tm, n_cols // tn), in_specs=in_specs, out_specs=out_specs,
        out_shape=out_shape, scratch_shapes=scratch,
        compiler_params=_params(("parallel", "arbitrary")), name=name)(*args)
    return res


def _rope_tile(x, cos, sin_signed):
    first = (lax.broadcasted_iota(I32, (x.shape[0], LANES), 1) % ATT_HEAD_DIM) < ATT_HEAD_DIM // 2
    outs = []
    for c in range(x.shape[1] // LANES):
        xc = x[:, c * LANES:(c + 1) * LANES]
        rot = jnp.where(first, pltpu.roll(xc, LANES - ATT_HEAD_DIM // 2, 1),
                        pltpu.roll(xc, ATT_HEAD_DIM // 2, 1))
        outs.append(xc * cos + rot * sin_signed)
    return outs[0] if len(outs) == 1 else jnp.concatenate(outs, axis=1)


def _epi_q(acc, extras, outs):
    cos_ref, sin_ref = extras
    outs[0][...] = (_rope_tile(acc, cos_ref[...], sin_ref[...]) * (ATT_HEAD_DIM ** -0.5)).astype(BF16)


def _epi_k(acc, extras, outs):
    cos_ref, sin_ref = extras
    r = _rope_tile(acc, cos_ref[...], sin_ref[...])
    outs[0][...] = r
    outs[1][...] = r.astype(BF16)


def _epi_v(acc, extras, outs):
    outs[0][...] = acc
    outs[1][...] = acc.astype(BF16)


def _epi_plain(acc, extras, outs):
    outs[0][...] = acc.astype(outs[0].dtype)


def _epi_residual(acc, extras, outs):
    outs[0][...] = extras[0][...] + acc


def _rope_tables(pos):
    half = ATT_HEAD_DIM // 2
    freqs = ROPE_THETA ** (-jnp.arange(half, dtype=F32) / half)
    ang = pos.astype(F32)[:, None] * freqs[None, :]
    c, s = jnp.cos(ang), jnp.sin(ang)
    return jnp.tile(c, (1, 4)), jnp.concatenate([-s, s, -s, s], axis=1)


def _inproj_kernel(x_ref, g_ref, w_ref, cos_ref, sin_ref, q_ref, k_ref, kb_ref, v_ref, vb_ref, hg_ref, gt_ref,
                   u_ref, *, bounds):
    j = pl.program_id(1)

    @pl.when(j == 0)
    def _():
        u_ref[...] = _rms(x_ref[...].astype(F32), g_ref[...]).astype(BF16)

    acc = jnp.dot(u_ref[...], w_ref[...], preferred_element_type=F32)
    in_seg = lambda s: (j >= bounds[s]) & (j < bounds[s + 1])

    @pl.when(in_seg(0))
    def _():
        _epi_q(acc, (cos_ref, sin_ref), (q_ref,))

    @pl.when(in_seg(1))
    def _():
        _epi_k(acc, (cos_ref, sin_ref), (k_ref, kb_ref))

    @pl.when(in_seg(2))
    def _():
        _epi_v(acc, (), (v_ref, vb_ref))

    @pl.when(in_seg(3))
    def _():
        hg_ref[...] = acc

    @pl.when(in_seg(4))
    def _():
        gt_ref[...] = acc.astype(gt_ref.dtype)


def _in_projection(x, g, w_in_bf, pos, tm, tn, tag, gate_dtype):
    M, K = x.shape
    cos, sin = _rope_tables(pos)
    widths = (ATT_WIDTH, ATT_WIDTH, ATT_WIDTH, 4 * HG_WIDTH, 2 * D_MODEL)
    bounds = tuple(int(b) for b in np.cumsum((0,) + widths) // tn)

    def seg_spec(s):
        lo, n = bounds[s], bounds[s + 1] - bounds[s]
        return pl.BlockSpec((tm, tn), lambda i, j: (i, jnp.clip(j - lo, 0, n - 1)))

    seg_of_out = (0, 1, 1, 2, 2, 3, 4)
    out_dtypes = (BF16, F32, BF16, F32, BF16, F32, gate_dtype)
    row_tbl = pl.BlockSpec((tm, LANES), lambda i, j: (i, 0))
    return pl.pallas_call(
        functools.partial(_inproj_kernel, bounds=bounds),
        grid=(M // tm, bounds[-1]),
        in_specs=[pl.BlockSpec((tm, K), lambda i, j: (i, 0), pipeline_mode=pl.Buffered(1)),
                  pl.BlockSpec((1, K), lambda i, j: (0, 0)),
                  pl.BlockSpec((None, K, tn), lambda i, j: (0, 0, j)), row_tbl, row_tbl],
        out_specs=[seg_spec(s) for s in seg_of_out],
        out_shape=[jax.ShapeDtypeStruct((M, widths[s]), dt) for s, dt in zip(seg_of_out, out_dtypes)],
        scratch_shapes=[pltpu.VMEM((tm, K), BF16)],
        compiler_params=_params(("parallel", "arbitrary")), name=f"inproj_{tag}",
    )(x, g.reshape(1, K), w_in_bf, cos, sin)


def _lambda_value(lq1, lk1, lq2, lk2):
    return (jnp.exp(jnp.sum(lq1[...] * lk1[...], axis=-1, keepdims=True))
            - jnp.exp(jnp.sum(lq2[...] * lk2[...], axis=-1, keepdims=True)) + LAMBDA_INIT)


def _attn_kernel(q_ref, k_ref, v_ref, lq1, lk1, lq2, lk2, subln_ref, o_ref, *, tq, tk):
    i = pl.program_id(1)
    q = q_ref[...]
    lane = lax.broadcasted_iota(I32, q.shape, 1)
    zero = jnp.zeros_like(q)
    qm = (jnp.where(lane < ATT_HEAD_DIM, q, zero), jnp.where(lane >= ATT_HEAD_DIM, q, zero))

    def update(off, carry, mask):
        off = pl.multiple_of(off, tk)
        kt = k_ref[pl.ds(off, tk), :]
        vt = v_ref[pl.ds(off, tk), :]
        new = []
        for m in range(2):
            mx, l, acc = carry[3 * m:3 * m + 3]
            s = lax.dot_general(qm[m], kt, _NT, preferred_element_type=F32)
            if mask is not None:
                s = jnp.where(mask, s, NEG_INF)
            mn = jnp.maximum(mx, jnp.max(s, axis=1, keepdims=True))
            alpha = jnp.exp(mx - mn)
            p = jnp.exp(s - mn)
            l = alpha * l + jnp.sum(p, axis=1, keepdims=True)
            acc = alpha * acc + jnp.dot(p.astype(BF16), vt, preferred_element_type=F32)
            new += [mn, l, acc]
        return tuple(new)

    def body(j, carry):
        return update(j * tk, carry, None)

    init = (jnp.full((tq, 1), NEG_INF, F32), jnp.zeros((tq, 1), F32), jnp.zeros((tq, LANES), F32)) * 2
    carry = lax.fori_loop(0, i * (tq // tk), body, init)
    row = lax.broadcasted_iota(I32, (tq, tk), 0)
    col = lax.broadcasted_iota(I32, (tq, tk), 1)
    for d in range(tq // tk):
        carry = update(i * tq + d * tk, carry, col + d * tk <= row)
    _, l0, a0, _, l1, a1 = carry

    lam = _lambda_value(lq1, lk1, lq2, lk2)
    o = a0 / l0 - lam * (a1 / l1)
    o_ref[...] = (_rms(o, subln_ref[...]) * (1.0 - LAMBDA_INIT)).astype(BF16)


def _small_spec(shape):
    nd = len(shape)
    return pl.BlockSpec(shape, lambda *_: (0,) * nd)


def _attention_prompt(q, k_bf, v_bf, lq1, lk1, lq2, lk2, subln, tq, tk):
    S = q.shape[0]
    assert tq % tk == 0
    lam_specs = [_small_spec((1, ATT_HEAD_DIM))] * 4
    return pl.pallas_call(
        functools.partial(_attn_kernel, tq=tq, tk=tk),
        grid=(ATT_HEADS, S // tq),
        in_specs=[pl.BlockSpec((tq, LANES), lambda h, i: (i, h)),
                  pl.BlockSpec((S, LANES), lambda h, i: (0, h)),
                  pl.BlockSpec((S, LANES), lambda h, i: (0, h))] + lam_specs
                 + [_small_spec((1, LANES))],
        out_specs=pl.BlockSpec((tq, LANES), lambda h, i: (i, h)),
        out_shape=jax.ShapeDtypeStruct((S, ATT_WIDTH), BF16),
        compiler_params=_params(("parallel", "arbitrary")), name="attn_prompt",
    )(q, k_bf, v_bf, lq1, lk1, lq2, lk2, subln)


def _decode_attn_kernel(pt_ref, q_ref, kn_ref, vn_ref, lq1, lk1, lq2, lk2, subln_ref, *rest, n_pg):
    k_refs = rest[:n_pg]
    v_refs = rest[n_pg:2 * n_pg]
    o_ref = rest[2 * n_pg]
    qr_ref, s_scr, w_scr, wn_scr, acc_ref = rest[2 * n_pg + 1:]
    phase = pl.program_id(1)
    j = pl.program_id(2)
    last = pl.num_programs(2) - 1
    nrow = 2 * ATT_HEADS

    @pl.when((phase == 0) & (j == 0))
    def _():
        row = lax.broadcasted_iota(I32, (nrow, ATT_WIDTH), 0)
        lane = lax.broadcasted_iota(I32, (nrow, ATT_WIDTH), 1)
        sel = (lane // LANES == row % ATT_HEADS) & ((lane // ATT_HEAD_DIM) % 2 == row // ATT_HEADS)
        qb = jnp.broadcast_to(q_ref[...].astype(F32), (nrow, ATT_WIDTH))
        qr_ref[...] = jnp.where(sel, qb, 0.0).astype(BF16)

    @pl.when(phase == 0)
    def _():
        qr = qr_ref[...]
        s_scr[j] = jnp.concatenate(
            [jnp.dot(qr, kr[...].astype(BF16), preferred_element_type=F32) for kr in k_refs], axis=1)

    @pl.when((phase == 1) & (j == 0))
    def _():
        sn = jnp.sum(qr_ref[...].astype(F32) * kn_ref[...].astype(F32), axis=1, keepdims=True)
        s = s_scr[...]
        m = jnp.maximum(jnp.max(jnp.max(s, axis=2, keepdims=True), axis=0, keepdims=True), sn[None])
        e = jnp.exp(s - m)
        en = jnp.exp(sn[None] - m)
        l = jnp.sum(jnp.sum(e, axis=2, keepdims=True), axis=0, keepdims=True) + en
        p = e / l
        pn = (en / l)[0]
        lam = _lambda_value(lq1, lk1, lq2, lk2)
        w = p[:, :ATT_HEADS, :] - lam * p[:, ATT_HEADS:, :]
        w_scr[...] = jnp.concatenate([w, jnp.zeros_like(w)], axis=1).astype(BF16)
        wn_scr[...] = pn[:ATT_HEADS] - lam * pn[ATT_HEADS:]
        acc_ref[...] = jnp.zeros(acc_ref.shape, F32)

    @pl.when(phase == 1)
    def _():
        wb = w_scr[j]
        page = wb.shape[1] // n_pg
        heads = []
        for h in range(ATT_HEADS):
            vh = jnp.concatenate([vr[pl.ds(h, page, stride=ATT_HEADS), :].astype(BF16) for vr in v_refs], axis=0)
            heads.append(jnp.dot(wb, vh, preferred_element_type=F32))
        acc_ref[...] = acc_ref[...] + jnp.concatenate(heads, axis=1)

    @pl.when((phase == 1) & (j == last))
    def _():
        wn = wn_scr[...].astype(BF16).astype(F32)
        od = acc_ref[...][:ATT_HEADS] + wn * vn_ref[...].astype(F32)
        own = (lax.broadcasted_iota(I32, od.shape, 1) // LANES) == lax.broadcasted_iota(I32, od.shape, 0)
        od = jnp.where(own, od, 0.0)
        ms = jnp.sum(od * od, axis=1, keepdims=True) / (2 * ATT_HEAD_DIM)
        y = od * lax.rsqrt(ms + RMS_EPS) * subln_ref[...] * (1.0 - LAMBDA_INIT)
        o_ref[...] = jnp.sum(y, axis=0, keepdims=True).astype(BF16)


def _attention_decode(q, k_new_bf, v_new_bf, cache_kt, cache_v, page_table, lq1, lk1, lq2, lk2, subln):
    Bd = q.shape[0]
    page = cache_kt.shape[2]
    n_pages = page_table.shape[1]
    n_pg = math.gcd(n_pages, PAGES_PER_STEP)
    assert n_pages % n_pg == 0
    subln_w = jnp.tile(subln, (1, ATT_HEADS))

    n_groups = n_pages // n_pg
    nrow = 2 * ATT_HEADS

    def k_spec(g):
        return pl.BlockSpec((None, ATT_WIDTH, page), lambda b, ph, j, pt: (
            pt[b * n_pages + jnp.where(ph == 0, j, n_groups - 1) * n_pg + g], 0, 0))

    def v_spec(g):
        def index(b, ph, j, pt):
            held = jnp.maximum(b - 1, 0) * n_pages + (n_groups - 1) * n_pg
            return pt[jnp.where(ph == 0, held, b * n_pages + j * n_pg) + g], 0, 0
        return pl.BlockSpec((None, page * ATT_HEADS, 2 * ATT_HEAD_DIM), index)

    row_spec = pl.BlockSpec((None, 1, ATT_WIDTH), lambda b, ph, j, pt: (b, 0, 0))
    small = lambda shape: pl.BlockSpec(shape, lambda b, ph, j, pt: (0,) * len(shape))
    grid_spec = pltpu.PrefetchScalarGridSpec(
        num_scalar_prefetch=1, grid=(Bd, 2, n_groups),
        in_specs=[row_spec, row_spec, row_spec] + [small((1, ATT_HEAD_DIM))] * 4 + [small((1, ATT_WIDTH))]
                 + [k_spec(g) for g in range(n_pg)] + [v_spec(g) for g in range(n_pg)],
        out_specs=row_spec,
        scratch_shapes=[pltpu.VMEM((nrow, ATT_WIDTH), BF16),
                        pltpu.VMEM((n_groups, nrow, n_pg * page), F32),
                        pltpu.VMEM((n_groups, nrow, n_pg * page), BF16),
                        pltpu.VMEM((ATT_HEADS, 1), F32),
                        pltpu.VMEM((nrow, ATT_WIDTH), F32)])
    out = pl.pallas_call(
        functools.partial(_decode_attn_kernel, n_pg=n_pg), grid_spec=grid_spec,
        out_shape=jax.ShapeDtypeStruct((Bd, 1, ATT_WIDTH), BF16),
        compiler_params=_params(("parallel", "arbitrary", "arbitrary")), name="attn_decode",
    )(page_table.reshape(-1), q.reshape(Bd, 1, -1), k_new_bf.reshape(Bd, 1, -1), v_new_bf.reshape(Bd, 1, -1),
      lq1, lk1, lq2, lk2, subln_w, *([cache_kt] * n_pg), *([cache_v] * n_pg))
    return out.reshape(Bd, ATT_WIDTH)


def _hgrn_masks():
    C = HG_CHUNK
    t = np.arange(C)[:, None]
    r = np.arange(C)[None, :]
    blocks = [(r <= t), (r > t)]
    for B in HG_LEVELS:
        mid = (t // B) * B + B // 2 - 1
        second = (t % B) >= B // 2
        blocks.append(np.where(second, (r > mid) & (r <= t), (r > t) & (r <= mid)))
    return jnp.asarray(np.concatenate(blocks, axis=0).astype(np.float32), dtype=BF16)


def _lower_bound(lb_ref):
    a = lb_ref[...].astype(F32)
    e = jnp.exp(a - jnp.max(a, axis=0, keepdims=True))
    return e[0:1] / jnp.sum(e, axis=0, keepdims=True)


def _group_rows(x, j):
    return jnp.broadcast_to(x[:, j:j + 1, :], x.shape)


def _split3(x):
    hi = x.astype(BF16)
    r1 = x - hi.astype(F32)
    mid = r1.astype(BF16)
    lo = (r1 - mid.astype(F32)).astype(BF16)
    return hi, mid, lo


def _hgrn_tile(q, z, v, lb, msk, s_t):
    C = HG_CHUNK
    n = q.shape[0] // C
    dot = lambda a, b: jnp.dot(a, b, preferred_element_type=F32)
    rows = lambda x, c: x[c * C:(c + 1) * C]
    chunks = range(n)

    logf = jnp.log(lb + (1.0 - lb) * jax.nn.sigmoid(z))
    kk = (1.0 - lb) * jax.nn.sigmoid(-z)
    v_bf = v.astype(BF16)
    parts = _split3(logf)
    e = [sum(dot(msk, rows(p, c)) for p in parts) for c in chunks]
    b = [ec[0:C] for ec in e]
    qi = [(rows(q, c) * jnp.exp(b[c])).astype(BF16) for c in chunks]
    kl = [(rows(kk, c) * jnp.exp(e[c][C:2 * C])).astype(BF16) for c in chunks]
    kv = [lax.dot_general(rows(v_bf, c), kl[c], _TN, preferred_element_type=F32) for c in chunks]

    t_idx = lax.broadcasted_iota(I32, (C, 1), 0)
    row = lax.broadcasted_iota(I32, (C, C), 0)
    col = lax.broadcasted_iota(I32, (C, C), 1)
    a = [jnp.zeros((C, C), F32) for _ in chunks]
    for li, B in enumerate(HG_LEVELS):
        second = (t_idx % B) >= B // 2
        same_block = (row // B) == (col // B)
        for c in chunks:
            x = jnp.exp(e[c][(2 + li) * C:(3 + li) * C])
            qt = jnp.where(second, rows(q, c) * x, 0.0).astype(BF16)
            kt = jnp.where(second, 0.0, rows(kk, c) * x).astype(BF16)
            al = lax.dot_general(qt, kt, _NT, preferred_element_type=F32)
            a[c] = a[c] + (jnp.where(same_block, al, 0.0) if B < C else al)

    states = [s_t]
    for c in chunks:
        states.append(states[-1] * jnp.exp(b[c][C - 1:C, :]) + kv[c])
    o = [lax.dot_general(qi[c], states[c].astype(BF16), _NT, preferred_element_type=F32)
         + dot(a[c].astype(BF16), rows(v_bf, c)) for c in chunks]

    G = n * C // SUBLANES
    b_all = jnp.concatenate(b, axis=0) if n > 1 else b[0]
    q3, k3, v3, b3 = (x.reshape(G, SUBLANES, LANES) for x in (q, kk, v, b_all))
    p3 = lax.broadcasted_iota(I32, (G, SUBLANES, 1), 1)
    o3 = (jnp.concatenate(o, axis=0) if n > 1 else o[0]).reshape(G, SUBLANES, LANES)
    for j in range(SUBLANES):
        term = q3 * _group_rows(k3, j) * jnp.exp(jnp.minimum(b3 - _group_rows(b3, j), 0.0))
        aj = jnp.where(p3 >= j, jnp.sum(term, axis=-1, keepdims=True), 0.0)
        o3 = o3 + aj * _group_rows(v3, j)
    return o3.reshape(n * C, LANES), states[-1]


def _hgrn_kernel(q_ref, z_ref, v_ref, g_ref, lb_ref, hgn_ref, msk_ref, o_ref, st_ref, s_scr):
    i = pl.program_id(1)

    @pl.when(i == 0)
    def _():
        s_scr[...] = jnp.zeros(s_scr.shape, F32)

    o, s_t = _hgrn_tile(q_ref[...], z_ref[...], v_ref[...], _lower_bound(lb_ref), msk_ref[...], s_scr[...])
    s_scr[...] = s_t
    g = g_ref[...]
    o_ref[...] = (_rms(o, hgn_ref[...]) * (g * jax.nn.sigmoid(g))).astype(BF16)

    @pl.when(i == pl.num_programs(1) - 1)
    def _():
        st_ref[...] = s_scr[...].T


def _hgrn_prompt(hg, hgrn_lb, hgrn_norm, tt):
    S = hg.shape[0]
    msk = _hgrn_masks()
    col = lambda seg: pl.BlockSpec((tt, LANES), lambda h, i, seg=seg: (i, seg * HG_HEADS + h))
    o, st = pl.pallas_call(
        _hgrn_kernel, grid=(HG_HEADS, S // tt),
        in_specs=[col(0), col(1), col(2), col(3),
                  pl.BlockSpec((hgrn_lb.shape[0], LANES), lambda h, i: (0, h)),
                  _small_spec((1, HG_VAL)), _small_spec(tuple(msk.shape))],
        out_specs=[pl.BlockSpec((tt, LANES), lambda h, i: (i, h)),
                   pl.BlockSpec((None, HG_KEY, HG_VAL), lambda h, i: (h, 0, 0))],
        out_shape=[jax.ShapeDtypeStruct((S, HG_WIDTH), BF16),
                   jax.ShapeDtypeStruct((HG_HEADS, HG_KEY, HG_VAL), F32)],
        scratch_shapes=[pltpu.VMEM((HG_VAL, HG_KEY), F32)],
        compiler_params=_params(("parallel", "arbitrary")), name="hgrn_prompt",
    )(hg, hg, hg, hg, hgrn_lb, hgrn_norm, msk)
    return o, st


def _hgrn_decode_kernel(hg_ref, s_ref, lb_ref, hgn_ref, o_ref, sn_ref):
    W = HG_WIDTH
    row = hg_ref[...]
    q, z, v, g = (row[:, s * W:(s + 1) * W] for s in range(4))
    lb = _lower_bound(lb_ref)
    f = lb + (1.0 - lb) * jax.nn.sigmoid(z)
    kk = (1.0 - lb) * jax.nn.sigmoid(-z)
    rnd = lambda x: x.astype(BF16).astype(F32)
    qf = rnd(q * f)
    qk = rnd(q) * rnd(kk)
    pad = jnp.zeros((SUBLANES - 3, LANES), F32)
    outs = []
    for h in range(HG_HEADS):
        hs = slice(h * LANES, (h + 1) * LANES)
        cols = jnp.concatenate([f[:, hs], kk[:, hs], qf[:, hs], pad], axis=0).T
        f_c, k_c, qf_c = cols[:, 0:1], cols[:, 1:2], cols[:, 2:3]
        s0 = s_ref[h]
        vh = v[:, hs]
        sn_ref[h] = f_c * s0 + k_c * vh
        o = (jnp.sum(qf_c * rnd(s0), axis=0, keepdims=True)
             + rnd(jnp.sum(qk[:, hs], axis=1, keepdims=True)) * rnd(vh))
        gh = g[:, hs]
        outs.append(_rms(o, hgn_ref[...]) * (gh * jax.nn.sigmoid(gh)))
    o_ref[...] = jnp.concatenate(outs, axis=1).astype(BF16)


def _hgrn_decode(hg, state, hgrn_lb, hgrn_norm):
    Bd = hg.shape[0]
    o, sn = pl.pallas_call(
        _hgrn_decode_kernel, grid=(Bd,),
        in_specs=[pl.BlockSpec((None, 1, 4 * HG_WIDTH), lambda b: (b, 0, 0)),
                  pl.BlockSpec((None, HG_HEADS, HG_KEY, HG_VAL), lambda b: (b, 0, 0, 0)),
                  _small_spec(tuple(hgrn_lb.shape)), _small_spec((1, HG_VAL))],
        out_specs=[pl.BlockSpec((None, 1, HG_WIDTH), lambda b: (b, 0, 0)),
                   pl.BlockSpec((None, HG_HEADS, HG_KEY, HG_VAL), lambda b: (b, 0, 0, 0))],
        out_shape=[jax.ShapeDtypeStruct((Bd, 1, HG_WIDTH), BF16),
                   jax.ShapeDtypeStruct(state.shape, F32)],
        compiler_params=_params(("parallel",)), name="hgrn_decode",
    )(hg.reshape(Bd, 1, -1), state, hgrn_lb, hgrn_norm)
    return o.reshape(Bd, HG_WIDTH), sn


def _merge_kernel(oa_ref, oh_ref, wa_ref, wb_ref, ga_ref, gb_ref, o_ref):
    a = jnp.dot(oa_ref[...], wa_ref[...].astype(BF16), preferred_element_type=F32)
    b = jnp.dot(oh_ref[...], wb_ref[...].astype(BF16), preferred_element_type=F32)
    ga = jax.nn.sigmoid(ga_ref[...].astype(F32))
    gb = jax.nn.sigmoid(gb_ref[...].astype(F32))
    o_ref[...] = (ga * a + gb * b).astype(BF16)


def _merge(oa, oh, w_pa, w_pb, gates, tm, tn, tag):
    M = oa.shape[0]
    nj = D_MODEL // tn
    return pl.pallas_call(
        _merge_kernel, grid=(M // tm, nj),
        in_specs=[pl.BlockSpec((tm, ATT_WIDTH), lambda i, j: (i, 0)),
                  pl.BlockSpec((tm, HG_WIDTH), lambda i, j: (i, 0)),
                  pl.BlockSpec((None, ATT_WIDTH, tn), lambda i, j: (0, 0, j)),
                  pl.BlockSpec((None, HG_WIDTH, tn), lambda i, j: (0, 0, j)),
                  pl.BlockSpec((tm, tn), lambda i, j: (i, j)),
                  pl.BlockSpec((tm, tn), lambda i, j: (i, j + nj))],
        out_specs=pl.BlockSpec((tm, tn), lambda i, j: (i, j)),
        out_shape=jax.ShapeDtypeStruct((M, D_MODEL), BF16),
        compiler_params=_params(("parallel", "arbitrary")), name=f"merge_{tag}",
    )(oa, oh, w_pa, w_pb, gates, gates)


def _cross_prompt_kernel(x_ref, g_ref, wq_ref, mk_ref, mv_ref, wo_ref, o_ref):
    x = x_ref[...]
    u = _rms(x, g_ref[...]).astype(BF16)
    q = jnp.dot(u, wq_ref[...], preferred_element_type=F32).astype(BF16)
    heads = []
    for h in range(X_HEADS):
        hs = slice(h * X_HEAD_DIM, (h + 1) * X_HEAD_DIM)
        s = lax.dot_general(q[:, hs], mk_ref[:, hs], _NT, preferred_element_type=F32) * (X_HEAD_DIM ** -0.5)
        p = jnp.exp(s - jnp.max(s, axis=1, keepdims=True))
        p = p / jnp.sum(p, axis=1, keepdims=True)
        heads.append(jnp.dot(p.astype(BF16), mv_ref[:, hs], preferred_element_type=F32))
    o = jnp.concatenate(heads, axis=1).astype(BF16)
    o_ref[...] = x + jnp.dot(o, wo_ref[...], preferred_element_type=F32)


def _cross_prompt(h1, g_cross, w_cq_bf, mk_bf, mv_bf, w_co_bf, tm):
    M = h1.shape[0]
    return pl.pallas_call(
        _cross_prompt_kernel, grid=(M // tm,),
        in_specs=[pl.BlockSpec((tm, D_MODEL), lambda i: (i, 0)), _small_spec((1, D_MODEL)),
                  _small_spec((D_MODEL, X_WIDTH)), _small_spec(tuple(mk_bf.shape)),
                  _small_spec(tuple(mv_bf.shape)), _small_spec((X_WIDTH, D_MODEL))],
        out_specs=pl.BlockSpec((tm, D_MODEL), lambda i: (i, 0)),
        out_shape=jax.ShapeDtypeStruct((M, D_MODEL), F32),
        compiler_params=_params(("parallel",)), name="cross_prompt",
    )(h1, g_cross.reshape(1, -1), w_cq_bf, mk_bf, mv_bf, w_co_bf)


def _cross_decode_kernel(q_ref, mk_ref, mv_ref, o_ref):
    rnd = lambda x: x.astype(BF16).astype(F32)
    q = rnd(q_ref[...])
    mem = mk_ref.shape[0] // X_HEADS
    outs = []
    for h in range(X_HEADS):
        hs = slice(h * X_HEAD_DIM, (h + 1) * X_HEAD_DIM)
        rows = pl.ds(h, mem, stride=X_HEADS)
        s = jnp.sum(rnd(mk_ref[rows, :]) * q[:, hs], axis=1, keepdims=True) * (X_HEAD_DIM ** -0.5)
        p = jnp.exp(s - jnp.max(s, axis=0, keepdims=True))
        p = p / jnp.sum(p, axis=0, keepdims=True)
        outs.append(jnp.sum(rnd(p) * rnd(mv_ref[rows, :]), axis=0, keepdims=True))
    o_ref[...] = jnp.concatenate(outs, axis=1).astype(BF16)


def _cross_decode(q, mem_k, mem_v):
    Bd, mem = mem_k.shape[0], mem_k.shape[1]
    row = pl.BlockSpec((None, 1, X_WIDTH), lambda b: (b, 0, 0))
    mem_spec = pl.BlockSpec((None, mem * X_HEADS, X_HEAD_DIM), lambda b: (b, 0, 0))
    flat = lambda m: m.reshape(Bd, mem * X_HEADS, X_HEAD_DIM)
    out = pl.pallas_call(
        _cross_decode_kernel, grid=(Bd,), in_specs=[row, mem_spec, mem_spec], out_specs=row,
        out_shape=jax.ShapeDtypeStruct((Bd, 1, X_WIDTH), BF16),
        compiler_params=_params(("parallel",)), name="cross_decode",
    )(q.reshape(Bd, 1, X_WIDTH), flat(mem_k), flat(mem_v))
    return out.reshape(Bd, X_WIDTH)


def _route_kernel(x_ref, g_ref, w_ref, b_ref, cin_ref, uin_ref, u_ref, eid_ref, wt_ref, rank_ref, cnt_ref,
                  c_scr, *, n_rows):
    i = pl.program_id(0)
    tm = x_ref.shape[0]

    @pl.when(i == 0)
    def _():
        c_scr[...] = cin_ref[...]

    u = _rms(x_ref[...], g_ref[...])
    u_ref[...] = u
    dot = lambda a, b: jnp.dot(a, b, preferred_element_type=F32)
    logits = dot(u.astype(BF16), w_ref[...]) + b_ref[...]

    lane = lax.broadcasted_iota(I32, (tm, LANES), 1)
    big = jnp.int32(1 << 20)
    first_max = lambda vals, vmax: jnp.min(jnp.where(vals == vmax, lane, big), axis=1, keepdims=True)
    glv = jnp.where(lane < N_GROUPS, logits, -jnp.inf)
    gmax = jnp.max(glv, axis=1, keepdims=True)
    gsel = first_max(glv, gmax)
    p_group = 1.0 / jnp.sum(jnp.exp(glv - gmax), axis=1, keepdims=True)
    e_lane = lane - N_GROUPS
    in_group = (e_lane >= 0) & (e_lane < N_EXPERTS) & (e_lane // EXPERTS_PER_GROUP == gsel)
    ev = jnp.where(in_group, logits, -jnp.inf)
    v1 = jnp.max(ev, axis=1, keepdims=True)
    i1 = first_max(ev, v1)
    ev2 = jnp.where(lane == i1, -jnp.inf, ev)
    v2 = jnp.max(ev2, axis=1, keepdims=True)
    i2 = first_max(ev2, v2)
    t2 = jnp.exp(v2 - v1)
    w1 = p_group / (1.0 + t2)
    w2 = w1 * t2
    e1 = i1 - N_GROUPS
    e2 = i2 - N_GROUPS
    eid_ref[...] = jnp.where(lane == 0, e1, jnp.where(lane == 1, e2, 0))
    wt_ref[...] = jnp.where(lane == 0, w1, jnp.where(lane == 1, w2, 0.0))

    valid = (lax.broadcasted_iota(I32, (tm, 1), 0) + i * tm) < n_rows
    oh1 = jnp.where((lane == e1) & valid, 1.0, 0.0)
    oh2 = jnp.where((lane == e2) & valid, 1.0, 0.0)
    cnt = (oh1 + oh2).astype(BF16)
    r_i = lax.broadcasted_iota(I32, (tm, tm), 0)
    c_i = lax.broadcasted_iota(I32, (tm, tm), 1)
    strict = jnp.where(c_i < r_i, 1.0, 0.0).astype(BF16)
    before = dot(strict, cnt) + c_scr[...]
    r1 = jnp.sum(oh1 * before, axis=1, keepdims=True)
    r2 = jnp.sum(oh2 * before, axis=1, keepdims=True)
    rank_ref[...] = jnp.where(lane == 0, r1, jnp.where(lane == 1, r2, 0.0)).astype(I32)
    c_scr[...] = c_scr[...] + jnp.sum(oh1 + oh2, axis=0, keepdims=True)
    cnt_ref[...] = c_scr[...]


def _route(x, g_ffn, w_router, b_router, counts_in, u_all, row_off, tm, tag):
    M = x.shape[0]
    T = u_all.shape[0]
    boff = row_off // tm
    kern = functools.partial(_route_kernel, n_rows=M)
    lane_out = lambda dt: jax.ShapeDtypeStruct((M, LANES), dt)
    outs = pl.pallas_call(
        kern, grid=(M // tm,),
        in_specs=[pl.BlockSpec((tm, D_MODEL), lambda i: (i, 0)), _small_spec((1, D_MODEL)),
                  _small_spec((D_MODEL, LANES)), _small_spec((1, LANES)), _small_spec((1, LANES)),
                  pl.BlockSpec(memory_space=pl.ANY)],
        out_specs=[pl.BlockSpec((tm, D_MODEL), lambda i: (i + boff, 0)),
                   pl.BlockSpec((tm, LANES), lambda i: (i, 0)), pl.BlockSpec((tm, LANES), lambda i: (i, 0)),
                   pl.BlockSpec((tm, LANES), lambda i: (i, 0)), _small_spec((1, LANES))],
        out_shape=[jax.ShapeDtypeStruct((T, D_MODEL), F32), lane_out(I32), lane_out(F32), lane_out(I32),
                   jax.ShapeDtypeStruct((1, LANES), F32)],
        scratch_shapes=[pltpu.VMEM((1, LANES), F32)],
        input_output_aliases={5: 0},
        compiler_params=_params(("arbitrary",)), name=f"moe_route_{tag}",
    )(x, g_ffn.reshape(1, -1), w_router, b_router, counts_in, u_all)
    return outs


def _plan_kernel(cnt_ref, pstart_ref, blkexp_ref, blkslot_ref, nxtexp_ref, nblk_ref, *, n_blocks):
    def per_expert(e, carry):
        acc, k = carry
        pstart_ref[e] = acc
        nb = (cnt_ref[e] + MOE_ROWS - 1) // MOE_ROWS
        first = acc // MOE_ROWS

        def fill(b, c):
            blkexp_ref[b] = e
            blkslot_ref[b] = k % 2
            return c

        lax.fori_loop(first, first + nb, fill, 0)
        return acc + nb * MOE_ROWS, k + jnp.where(nb > 0, 1, 0)

    total, _ = lax.fori_loop(0, N_EXPERTS, per_expert, (jnp.int32(0), jnp.int32(0)))
    used = total // MOE_ROWS
    nblk_ref[0] = used
    last = jnp.maximum(used - 1, 0)

    def backwards(t, carry):
        cur, nxt = carry
        b = last - t
        e = blkexp_ref[b]
        nxt = jnp.where(e != cur, cur, nxt)
        nxtexp_ref[b] = nxt
        return e, nxt

    lax.fori_loop(0, used, backwards, (blkexp_ref[last], jnp.int32(-1)))

    def tail(b, c):
        blkexp_ref[b] = blkexp_ref[last]
        blkslot_ref[b] = blkslot_ref[last]
        nxtexp_ref[b] = -1
        return c

    lax.fori_loop(used, n_blocks, tail, 0)


def _plan(counts, n_blocks):
    smem = pl.BlockSpec(memory_space=pltpu.SMEM)
    per_block = jax.ShapeDtypeStruct((n_blocks,), I32)
    return pl.pallas_call(
        functools.partial(_plan_kernel, n_blocks=n_blocks),
        in_specs=[smem], out_specs=[smem] * 5,
        out_shape=[jax.ShapeDtypeStruct((N_EXPERTS,), I32), per_block, per_block, per_block,
                   jax.ShapeDtypeStruct((1,), I32)],
        name="moe_plan",
    )(counts)


def _dispatch_kernel(eid_ref, rank_ref, pstart_ref, cnt_ref, nblk_ref, u_ref, xs_hbm, zbuf, sem_z, sem,
                     *, n_tokens, n_blocks):
    R = MOE_ROWS
    i = pl.program_id(0)
    tm = u_ref.shape[0]

    def zero_copy(b):
        return pltpu.make_async_copy(zbuf, xs_hbm.at[pl.ds(pl.multiple_of(b * R, R), R), :], sem_z)

    def partial_block(e):
        c = cnt_ref[e]
        return (c % R) != 0, (pstart_ref[e] + c) // R

    def zero_partial(start):
        def body(e, n):
            has, b = partial_block(e)

            @pl.when(has)
            def _():
                zero_copy(b).start() if start else zero_copy(b).wait()

            return n
        return body

    def zero_unused(start):
        def body(b, n):
            zero_copy(b).start() if start else zero_copy(b).wait()
            return n
        return body

    @pl.when(i == 0)
    def _():
        zbuf[...] = jnp.zeros(zbuf.shape, F32)
        for start in (True, False):
            lax.fori_loop(0, N_EXPERTS, zero_partial(start), 0)
            lax.fori_loop(nblk_ref[0], n_blocks, zero_unused(start), 0)

    def row_copy(r, k):
        a = 2 * (i * tm + r) + k
        d = pstart_ref[eid_ref[a]] + rank_ref[a]
        return pltpu.make_async_copy(u_ref.at[pl.ds(r, 1), :], xs_hbm.at[pl.ds(d, 1), :], sem)

    def start_row(r, n):
        row_copy(r, 0).start()
        row_copy(r, 1).start()
        return n

    def wait_row(r, n):
        row_copy(r, 0).wait()
        row_copy(r, 1).wait()
        return n

    rem = n_tokens % tm

    @pl.when((i + 1) * tm <= n_tokens)
    def _():
        lax.fori_loop(0, tm, start_row, 0, unroll=8)
        lax.fori_loop(0, tm, wait_row, 0, unroll=8)

    if rem:
        @pl.when((i + 1) * tm > n_tokens)
        def _():
            lax.fori_loop(0, rem, start_row, 0, unroll=8)
            lax.fori_loop(0, rem, wait_row, 0, unroll=8)


def _dispatch(eid, rank, pstart, counts, n_used, u_all, n_blocks, tm):
    T = u_all.shape[0]
    grid_spec = pltpu.PrefetchScalarGridSpec(
        num_scalar_prefetch=5, grid=(pl.cdiv(T, tm),),
        in_specs=[pl.BlockSpec((tm, D_MODEL), lambda i, *_: (i, 0))],
        out_specs=pl.BlockSpec(memory_space=pl.ANY),
        scratch_shapes=[pltpu.VMEM((MOE_ROWS, D_MODEL), F32), pltpu.SemaphoreType.DMA(()),
                        pltpu.SemaphoreType.DMA(())])
    return pl.pallas_call(
        functools.partial(_dispatch_kernel, n_tokens=T, n_blocks=n_blocks), grid_spec=grid_spec,
        out_shape=jax.ShapeDtypeStruct((n_blocks * MOE_ROWS, D_MODEL), F32),
        compiler_params=_params(("arbitrary",)), name="moe_dispatch",
    )(eid, rank, pstart, counts, n_used, u_all)


def _expert_kernel(blkexp_ref, blkslot_ref, nxtexp_ref, nblk_ref, x_ref, wg_hbm, wu_hbm, wd_hbm, o_ref,
                   wg_f, wu_f, wd_f, wg_s, wu_s, wd_s, sem):
    i = pl.program_id(0)

    def fetch(e, slot):
        return [pltpu.make_async_copy(src.at[0, e], dst.at[slot], sem.at[slot])
                for src, dst in ((wg_hbm, wg_f), (wu_hbm, wu_f), (wd_hbm, wd_f))]

    @pl.when(i < nblk_ref[0])
    def _():
        e = blkexp_ref[i]
        slot = blkslot_ref[i]
        nxt = nxtexp_ref[i]

        @pl.when(i == 0)
        def _():
            for cp in fetch(e, slot):
                cp.start()

        @pl.when((i == 0) | (e != blkexp_ref[jnp.maximum(i - 1, 0)]))
        def _():
            for cp in fetch(e, slot):
                cp.wait()

            @pl.when(nxt >= 0)
            def _():
                for cp in fetch(nxt, 1 - slot):
                    cp.start()

            wg_s[...] = wg_f[slot].astype(BF16)
            wu_s[...] = wu_f[slot].astype(BF16)
            wd_s[...] = wd_f[slot].astype(BF16)

        x = x_ref[...].astype(BF16)
        hg = jnp.dot(x, wg_s[...], preferred_element_type=F32)
        hu = jnp.dot(x, wu_s[...], preferred_element_type=F32)
        h = (hg * jax.nn.sigmoid(hg) * hu).astype(BF16)
        o_ref[...] = jnp.dot(h, wd_s[...], preferred_element_type=F32)

    @pl.when(i >= nblk_ref[0])
    def _():
        o_ref[...] = jnp.zeros(o_ref.shape, F32)


def _experts(xs, blk_exp, blk_slot, nxt_exp, n_used, we_g, we_u, we_d, n_blocks):
    hbm = pl.BlockSpec(memory_space=pl.ANY)
    up, down = (D_MODEL, EXPERT_FF), (EXPERT_FF, D_MODEL)
    grid_spec = pltpu.PrefetchScalarGridSpec(
        num_scalar_prefetch=4, grid=(n_blocks,),
        in_specs=[pl.BlockSpec((MOE_ROWS, D_MODEL),
                               lambda i, be, bs, nx, nb: (jnp.maximum(jnp.minimum(i, nb[0] - 1), 0), 0)),
                  hbm, hbm, hbm],
        out_specs=pl.BlockSpec((MOE_ROWS, D_MODEL), lambda i, *_: (i, 0)),
        scratch_shapes=[pltpu.VMEM((2,) + up, F32), pltpu.VMEM((2,) + up, F32), pltpu.VMEM((2,) + down, F32),
                        pltpu.VMEM(up, BF16), pltpu.VMEM(up, BF16), pltpu.VMEM(down, BF16),
                        pltpu.SemaphoreType.DMA((2,))])
    return pl.pallas_call(
        _expert_kernel, grid_spec=grid_spec,
        out_shape=jax.ShapeDtypeStruct((n_blocks * MOE_ROWS, D_MODEL), F32),
        compiler_params=_params(("arbitrary",)), name="moe_experts",
    )(blk_exp, blk_slot, nxt_exp, n_used, xs, we_g, we_u, we_d)


def _combine_kernel(eid_ref, rank_ref, pstart_ref, h_ref, wt_ref, gf_ref, yb_hbm, o_ref, ybuf0, ybuf1, sem,
                    *, dest_off):
    i = pl.program_id(0)
    tm = h_ref.shape[0]
    base = dest_off + i * (2 * tm)

    def row_copy(r, k, buf):
        a = base + 2 * r + k
        d = pstart_ref[eid_ref[a]] + rank_ref[a]
        return pltpu.make_async_copy(yb_hbm.at[pl.ds(d, 1), :], buf.at[pl.ds(r, 1), :], sem)

    def start(r, c):
        row_copy(r, 0, ybuf0).start()
        row_copy(r, 1, ybuf1).start()
        return c

    def wait(r, c):
        row_copy(r, 0, ybuf0).wait()
        row_copy(r, 1, ybuf1).wait()
        return c

    lax.fori_loop(0, tm, start, 0, unroll=4)
    lax.fori_loop(0, tm, wait, 0, unroll=4)
    wt = wt_ref[...]
    h = h_ref[...] + (wt[:, 0:1] * ybuf0[...] + wt[:, 1:2] * ybuf1[...])
    o_ref[...] = _rms(h, gf_ref[...])


def _combine(h2, wts, eid, rank, pstart, yb, norm_final, dest_off, tm, tag):
    M = h2.shape[0]
    grid_spec = pltpu.PrefetchScalarGridSpec(
        num_scalar_prefetch=3, grid=(M // tm,),
        in_specs=[pl.BlockSpec((tm, D_MODEL), lambda i, *_: (i, 0)), pl.BlockSpec((tm, LANES), lambda i, *_: (i, 0)),
                  pl.BlockSpec((1, D_MODEL), lambda i, *_: (0, 0)), pl.BlockSpec(memory_space=pl.ANY)],
        out_specs=pl.BlockSpec((tm, D_MODEL), lambda i, *_: (i, 0)),
        scratch_shapes=[pltpu.VMEM((tm, D_MODEL), F32), pltpu.VMEM((tm, D_MODEL), F32),
                        pltpu.SemaphoreType.DMA(())])
    return pl.pallas_call(
        functools.partial(_combine_kernel, dest_off=dest_off), grid_spec=grid_spec,
        out_shape=jax.ShapeDtypeStruct((M, D_MODEL), F32),
        compiler_params=_params(("arbitrary",)), name=f"moe_combine_{tag}",
    )(eid, rank, pstart, h2, wts, norm_final.reshape(1, -1), yb)


def _moe_and_final_norm(h2_p, h2_s, g_ffn, wr_g, br_g, wr_e, br_e, we_g, we_u, we_d, norm_final, tm_p):
    Tp, Ts = h2_p.shape[0], h2_s.shape[0]
    T = Tp + Ts
    pad = LANES - N_GROUPS - N_EXPERTS
    w_router = jnp.concatenate([wr_g, wr_e, jnp.zeros((D_MODEL, pad), F32)], axis=1).astype(BF16)
    b_router = jnp.concatenate([br_g, br_e, jnp.zeros((pad,), F32)]).reshape(1, LANES)
    u_all = jnp.zeros((T, D_MODEL), F32)
    zero_counts = jnp.zeros((1, LANES), F32)
    u_all, eid_p, wt_p, rank_p, counts = _route(h2_p, g_ffn, w_router, b_router, zero_counts, u_all, 0, tm_p, "p")
    u_all, eid_s, wt_s, rank_s, counts = _route(h2_s, g_ffn, w_router, b_router, counts, u_all, Tp, Ts, "s")
    eid = jnp.concatenate([eid_p[:, :2].reshape(-1), eid_s[:, :2].reshape(-1)])
    rank = jnp.concatenate([rank_p[:, :2].reshape(-1), rank_s[:, :2].reshape(-1)])
    A = 2 * T
    n_blocks = (A + N_EXPERTS * (MOE_ROWS - 1)) // MOE_ROWS + 1
    cnt = counts[0, :N_EXPERTS].astype(I32)
    pstart, blk_exp, blk_slot, nxt_exp, n_used = _plan(cnt, n_blocks)
    xs = _dispatch(eid, rank, pstart, cnt, n_used, u_all, n_blocks, 256)
    yb = _experts(xs, blk_exp, blk_slot, nxt_exp, n_used, we_g, we_u, we_d, n_blocks)
    y_p = _combine(h2_p, wt_p, eid, rank, pstart, yb, norm_final, 0, 256, "p")
    y_s = _combine(h2_s, wt_s, eid, rank, pstart, yb, norm_final, 2 * Tp, Ts, "s")
    return y_p, y_s


def kernel(x_prompt, x_sample, mem_prompt, cache_k, cache_v, cache_mem_k, cache_mem_v, state_hgrn, page_table,
           norm_mix, w_in, lambda_q1, lambda_k1, lambda_q2, lambda_k2, subln, hgrn_lb, hgrn_norm, w_pa, w_pb,
           w_out, norm_cross, w_cq, w_ck, w_cv, w_co, norm_ffn, w_router_group, b_router_group,
           w_router_expert, b_router_expert, w_e_gate, w_e_up, w_e_down, norm_final):
    assert w_in.shape[0] == 1, "single-layer step"
    Bp, S, D = x_prompt.shape
    Bd, Ld, _ = x_sample.shape
    assert Bp == 1 and Ld == 1
    n_pages = page_table.shape[1]
    page = cache_k.shape[2]
    past_len = n_pages * page
    xp = x_prompt.reshape(S, D)
    xs = x_sample.reshape(Bd, D)
    lam = (lambda_q1, lambda_k1, lambda_q2, lambda_k2)
    TM = 1024
    w_in = w_in.astype(BF16)

    q, k, k_bf, v, v_bf, hg, gates = _in_projection(xp, norm_mix[0], w_in, jnp.arange(S, dtype=I32), TM, 512, "p",
                                                    BF16)
    oa = _attention_prompt(q, k_bf, v_bf, *lam, subln, ATTN_TQ, ATTN_TK)
    oh, st_p = _hgrn_prompt(hg, hgrn_lb, hgrn_norm, 512)
    merged = _merge(oa, oh, w_pa, w_pb, gates, TM, 512, "p")
    (h1_p,) = _mm(merged, w_out, col_off=0, n_cols=D, tm=TM, tn=512, epilogue=_epi_residual, out_dtypes=[F32],
                  extras=[(xp, pl.BlockSpec((TM, 512), lambda i, j: (i, j)))], name="outproj_p")

    pos_s = jnp.full((Bd,), past_len, I32)
    qs, ks, ks_bf, vs, vs_bf, hgs, gates_s = _in_projection(xs, norm_mix[0], w_in, pos_s, Bd, 512, "s", F32)
    ck = jnp.transpose(cache_k[0], (0, 2, 3, 4, 1)).reshape(cache_k.shape[1], ATT_WIDTH, page)
    cv = cache_v[0].reshape(cache_v.shape[1], page * ATT_HEADS, 2 * ATT_HEAD_DIM)
    oa_s = _attention_decode(qs, ks_bf, vs_bf, ck, cv, page_table, *lam, subln)
    oh_s, st_s = _hgrn_decode(hgs, state_hgrn[0], hgrn_lb, hgrn_norm)
    merged_s = _merge(oa_s, oh_s, w_pa, w_pb, gates_s, Bd, 512, "s")
    (h1_s,) = _mm(merged_s, w_out, col_off=0, n_cols=D, tm=Bd, tn=512, epilogue=_epi_residual, out_dtypes=[F32],
                  extras=[(xs, pl.BlockSpec((Bd, 512), lambda i, j: (i, j)))], name="outproj_s")

    mem = mem_prompt.reshape(-1, D)
    mk, mk_bf = _mm(mem, w_ck, col_off=0, n_cols=X_WIDTH, tm=mem.shape[0], tn=X_WIDTH, epilogue=_epi_v,
                    out_dtypes=[F32, BF16], name="mem_k")
    mv, mv_bf = _mm(mem, w_cv, col_off=0, n_cols=X_WIDTH, tm=mem.shape[0], tn=X_WIDTH, epilogue=_epi_v,
                    out_dtypes=[F32, BF16], name="mem_v")
    w_cq_bf = w_cq[0].astype(BF16)
    w_co_bf = w_co[0].astype(BF16)
    h2_p = _cross_prompt(h1_p, norm_cross[0], w_cq_bf, mk_bf, mv_bf, w_co_bf, 256)
    (qc_s,) = _mm(h1_s, w_cq, col_off=0, n_cols=X_WIDTH, tm=Bd, tn=X_WIDTH, epilogue=_epi_plain,
                  out_dtypes=[F32], norm_g=norm_cross[0], name="cross_q_s")
    oc_s = _cross_decode(qc_s, cache_mem_k[0], cache_mem_v[0])
    (h2_s,) = _mm(oc_s, w_co, col_off=0, n_cols=D, tm=Bd, tn=512, epilogue=_epi_residual, out_dtypes=[F32],
                  extras=[(h1_s, pl.BlockSpec((Bd, 512), lambda i, j: (i, j)))], name="cross_o_s")

    y_p, y_s = _moe_and_final_norm(h2_p, h2_s, norm_ffn[0], w_router_group[0], b_router_group[0],
                                   w_router_expert[0], b_router_expert[0], w_e_gate, w_e_up, w_e_down,
                                   norm_final, 256)

    return (y_p.reshape(Bp, S, D), y_s.reshape(Bd, Ld, D),
            k.reshape(1, Bp, S, ATT_HEADS, 2, ATT_HEAD_DIM), v.reshape(1, Bp, S, ATT_HEADS, 2 * ATT_HEAD_DIM),
            ks.reshape(1, Bd, Ld, ATT_HEADS, 2, ATT_HEAD_DIM), vs.reshape(1, Bd, Ld, ATT_HEADS, 2 * ATT_HEAD_DIM),
            st_p.reshape(1, Bp, HG_HEADS, HG_KEY, HG_VAL), st_s.reshape(1, Bd, HG_HEADS, HG_KEY, HG_VAL),
            mk.reshape(1, Bp, -1, X_HEADS, X_HEAD_DIM), mv.reshape(1, Bp, -1, X_HEADS, X_HEAD_DIM))
```

```python
import functools
import math

import numpy as np
import jax
import jax.numpy as jnp
from jax import lax
from jax.experimental import pallas as pl
from jax.experimental.pallas import tpu as pltpu

F32 = jnp.float32
BF16 = jnp.bfloat16
I32 = jnp.int32

D_MODEL = 2048
ATT_HEADS = 8
ATT_HEAD_DIM = 64
ATT_WIDTH = ATT_HEADS * 2 * ATT_HEAD_DIM
ROPE_THETA = 10000.0
HG_HEADS = 8
HG_KEY = 128
HG_VAL = 128
HG_WIDTH = HG_HEADS * HG_VAL
X_HEADS = 4
X_HEAD_DIM = 128
X_WIDTH = X_HEADS * X_HEAD_DIM
N_GROUPS = 4
EXPERTS_PER_GROUP = 8
N_EXPERTS = N_GROUPS * EXPERTS_PER_GROUP
EXPERT_FF = 512
RMS_EPS = 1e-6
NEG_INF = -1e30
LAMBDA_INIT = 0.8 - 0.6 * math.exp(-0.3 * 0)

LANES = 128
SUBLANES = 8
VMEM_LIMIT = 52 * 1024 * 1024

HG_CHUNK = 64
HG_LEVELS = (16, 32, 64)
MOE_ROWS = 256
PAGES_PER_STEP = 16
ATTN_TQ = 1024
ATTN_TK = 1024

_NT = (((1,), (1,)), ((), ()))
_TN = (((0,), (0,)), ((), ()))


def _params(sem):
    return pltpu.CompilerParams(dimension_semantics=sem, vmem_limit_bytes=VMEM_LIMIT)


def _rms(x, g):
    return x * lax.rsqrt(jnp.mean(x * x, axis=-1, keepdims=True) + RMS_EPS) * g


def _mm_kernel(*refs, n_extra, n_out, norm, epilogue):
    x_ref = refs[0]
    pos = 1
    if norm:
        g_ref = refs[1]
        pos = 2
    w_ref = refs[pos]
    extras = refs[pos + 1:pos + 1 + n_extra]
    outs = refs[pos + 1 + n_extra:pos + 1 + n_extra + n_out]
    if norm:
        u_ref = refs[-1]

        @pl.when(pl.program_id(1) == 0)
        def _():
            u_ref[...] = _rms(x_ref[...].astype(F32), g_ref[...]).astype(BF16)

        u = u_ref[...]
    else:
        u = x_ref[...].astype(BF16)
    acc = jnp.dot(u, w_ref[...].astype(BF16), preferred_element_type=F32)
    epilogue(acc, extras, outs)


def _mm(x, w, *, col_off, n_cols, tm, tn, epilogue, out_dtypes, norm_g=None, extras=(), name):
    M, K = x.shape
    assert M % tm == 0 and n_cols % tn == 0 and col_off % tn == 0
    joff = col_off // tn
    if w.ndim == 3:
        w_spec = pl.BlockSpec((None, K, tn), lambda i, j: (0, 0, j + joff))
    else:
        w_spec = pl.BlockSpec((K, tn), lambda i, j: (0, j + joff))
    in_specs = [pl.BlockSpec((tm, K), lambda i, j: (i, 0))]
    args = [x]
    if norm_g is not None:
        in_specs.append(pl.BlockSpec((1, K), lambda i, j: (0, 0)))
        args.append(norm_g.reshape(1, K))
    in_specs.append(w_spec)
    args.append(w)
    for arr, spec in extras:
        in_specs.append(spec)
        args.append(arr)
    out_specs = [pl.BlockSpec((tm, tn), lambda i, j: (i, j)) for _ in out_dtypes]
    out_shape = [jax.ShapeDtypeStruct((M, n_cols), dt) for dt in out_dtypes]
    scratch = [pltpu.VMEM((tm, K), BF16)] if norm_g is not None else []
    kern = functools.partial(_mm_kernel, n_extra=len(extras), n_out=len(out_dtypes),
                             norm=norm_g is not None, epilogue=epilogue)
    res = pl.pallas_call(
        kern, grid=(M // tm, n_cols // tn), in_specs=in_specs, out_specs=out_specs,
        out_shape=out_shape, scratch_shapes=scratch,
        compiler_params=_params(("parallel", "arbitrary")), name=name)(*args)
    return res


def _rope_tile(x, cos, sin_signed):
    first = (lax.broadcasted_iota(I32, (x.shape[0], LANES), 1) % ATT_HEAD_DIM) < ATT_HEAD_DIM // 2
    outs = []
    for c in range(x.shape[1] // LANES):
        xc = x[:, c * LANES:(c + 1) * LANES]
        rot = jnp.where(first, pltpu.roll(xc, LANES - ATT_HEAD_DIM // 2, 1),
                        pltpu.roll(xc, ATT_HEAD_DIM // 2, 1))
        outs.append(xc * cos + rot * sin_signed)
    return outs[0] if len(outs) == 1 else jnp.concatenate(outs, axis=1)


def _epi_q(acc, extras, outs, q_scale):
    cos_ref, sin_ref = extras
    outs[0][...] = (_rope_tile(acc, cos_ref[...], sin_ref[...]) * q_scale).astype(BF16)


def _epi_k(acc, extras, outs):
    cos_ref, sin_ref = extras
    r = _rope_tile(acc, cos_ref[...], sin_ref[...])
    outs[0][...] = r
    outs[1][...] = r.astype(BF16)


def _epi_v(acc, extras, outs):
    outs[0][...] = acc
    outs[1][...] = acc.astype(BF16)


def _epi_plain(acc, extras, outs):
    outs[0][...] = acc.astype(outs[0].dtype)


def _epi_residual(acc, extras, outs):
    outs[0][...] = extras[0][...] + acc


def _rope_tables(pos):
    half = ATT_HEAD_DIM // 2
    freqs = ROPE_THETA ** (-jnp.arange(half, dtype=F32) / half)
    ang = pos.astype(F32)[:, None] * freqs[None, :]
    c, s = jnp.cos(ang), jnp.sin(ang)
    return jnp.tile(c, (1, 4)), jnp.concatenate([-s, s, -s, s], axis=1)


def _inproj_kernel(x_ref, g_ref, w_ref, cos_ref, sin_ref, q_ref, k_ref, kb_ref, v_ref, vb_ref, hg_ref, gt_ref,
                   u_ref, *, bounds, q_scale):
    j = pl.program_id(1)

    @pl.when(j == 0)
    def _():
        u_ref[...] = _rms(x_ref[...].astype(F32), g_ref[...]).astype(BF16)

    acc = jnp.dot(u_ref[...], w_ref[...], preferred_element_type=F32)
    in_seg = lambda s: (j >= bounds[s]) & (j < bounds[s + 1])

    @pl.when(in_seg(0))
    def _():
        _epi_q(acc, (cos_ref, sin_ref), (q_ref,), q_scale)

    @pl.when(in_seg(1))
    def _():
        _epi_k(acc, (cos_ref, sin_ref), (k_ref, kb_ref))

    @pl.when(in_seg(2))
    def _():
        _epi_v(acc, (), (v_ref, vb_ref))

    @pl.when(in_seg(3))
    def _():
        hg_ref[...] = acc

    @pl.when(in_seg(4))
    def _():
        gt_ref[...] = acc.astype(gt_ref.dtype)


def _in_projection(x, g, w_in_bf, pos, tm, tn, tag, gate_dtype, q_scale):
    M, K = x.shape
    cos, sin = _rope_tables(pos)
    widths = (ATT_WIDTH, ATT_WIDTH, ATT_WIDTH, 4 * HG_WIDTH, 2 * D_MODEL)
    bounds = tuple(int(b) for b in np.cumsum((0,) + widths) // tn)

    def seg_spec(s):
        lo, n = bounds[s], bounds[s + 1] - bounds[s]
        return pl.BlockSpec((tm, tn), lambda i, j: (i, jnp.clip(j - lo, 0, n - 1)))

    seg_of_out = (0, 1, 1, 2, 2, 3, 4)
    out_dtypes = (BF16, F32, BF16, F32, BF16, F32, gate_dtype)
    row_tbl = pl.BlockSpec((tm, LANES), lambda i, j: (i, 0))
    return pl.pallas_call(
        functools.partial(_inproj_kernel, bounds=bounds, q_scale=q_scale),
        grid=(M // tm, bounds[-1]),
        in_specs=[pl.BlockSpec((tm, K), lambda i, j: (i, 0), pipeline_mode=pl.Buffered(1)),
                  pl.BlockSpec((1, K), lambda i, j: (0, 0)),
                  pl.BlockSpec((None, K, tn), lambda i, j: (0, 0, j)), row_tbl, row_tbl],
        out_specs=[seg_spec(s) for s in seg_of_out],
        out_shape=[jax.ShapeDtypeStruct((M, widths[s]), dt) for s, dt in zip(seg_of_out, out_dtypes)],
        scratch_shapes=[pltpu.VMEM((tm, K), BF16)],
        compiler_params=_params(("parallel", "arbitrary")), name=f"inproj_{tag}",
    )(x, g.reshape(1, K), w_in_bf, cos, sin)


def _lambda_value(lq1, lk1, lq2, lk2):
    return (jnp.exp(jnp.sum(lq1[...] * lk1[...], axis=-1, keepdims=True))
            - jnp.exp(jnp.sum(lq2[...] * lk2[...], axis=-1, keepdims=True)) + LAMBDA_INIT)


def _attn_kernel(q_ref, k_ref, v_ref, lq1, lk1, lq2, lk2, subln_ref, o_ref, *, tq, tk):
    i = pl.program_id(1)
    q = q_ref[...]
    lane = lax.broadcasted_iota(I32, q.shape, 1)
    zero = jnp.zeros_like(q)
    qm = (jnp.where(lane < ATT_HEAD_DIM, q, zero), jnp.where(lane >= ATT_HEAD_DIM, q, zero))

    def update(off, carry, mask):
        off = pl.multiple_of(off, tk)
        kt = k_ref[pl.ds(off, tk), :]
        vt = v_ref[pl.ds(off, tk), :]
        new = []
        for m in range(2):
            mx, l, acc = carry[3 * m:3 * m + 3]
            s = lax.dot_general(qm[m], kt, _NT, preferred_element_type=F32)
            if mask is not None:
                s = jnp.where(mask, s, NEG_INF)
            mn = jnp.maximum(mx, jnp.max(s, axis=1, keepdims=True))
            alpha = jnp.exp2(mx - mn)
            p = jnp.exp2(s - mn)
            l = alpha * l + jnp.sum(p, axis=1, keepdims=True)
            acc = alpha * acc + jnp.dot(p.astype(BF16), vt, preferred_element_type=F32)
            new += [mn, l, acc]
        return tuple(new)

    def body(j, carry):
        return update(j * tk, carry, None)

    init = (jnp.full((tq, 1), NEG_INF, F32), jnp.zeros((tq, 1), F32), jnp.zeros((tq, LANES), F32)) * 2
    carry = lax.fori_loop(0, i * (tq // tk), body, init)
    row = lax.broadcasted_iota(I32, (tq, tk), 0)
    col = lax.broadcasted_iota(I32, (tq, tk), 1)
    for d in range(tq // tk):
        carry = update(i * tq + d * tk, carry, col + d * tk <= row)
    _, l0, a0, _, l1, a1 = carry

    lam = _lambda_value(lq1, lk1, lq2, lk2)
    o = a0 / l0 - lam * (a1 / l1)
    o_ref[...] = (_rms(o, subln_ref[...]) * (1.0 - LAMBDA_INIT)).astype(BF16)


def _small_spec(shape):
    nd = len(shape)
    return pl.BlockSpec(shape, lambda *_: (0,) * nd)


def _attention_prompt(q, k_bf, v_bf, lq1, lk1, lq2, lk2, subln, tq, tk):
    S = q.shape[0]
    assert tq % tk == 0
    lam_specs = [_small_spec((1, ATT_HEAD_DIM))] * 4
    return pl.pallas_call(
        functools.partial(_attn_kernel, tq=tq, tk=tk),
        grid=(ATT_HEADS, S // tq),
        in_specs=[pl.BlockSpec((tq, LANES), lambda h, i: (i, h)),
                  pl.BlockSpec((S, LANES), lambda h, i: (0, h)),
                  pl.BlockSpec((S, LANES), lambda h, i: (0, h))] + lam_specs
                 + [_small_spec((1, LANES))],
        out_specs=pl.BlockSpec((tq, LANES), lambda h, i: (i, h)),
        out_shape=jax.ShapeDtypeStruct((S, ATT_WIDTH), BF16),
        compiler_params=_params(("parallel", "arbitrary")), name="attn_prompt",
    )(q, k_bf, v_bf, lq1, lk1, lq2, lk2, subln)


def _decode_attn_kernel(pt_ref, q_ref, kn_ref, vn_ref, lq1, lk1, lq2, lk2, subln_ref, *rest, n_pg):
    k_refs = rest[:n_pg]
    v_refs = rest[n_pg:2 * n_pg]
    o_ref = rest[2 * n_pg]
    qr_ref, s_scr, w_scr, wn_scr, acc_ref = rest[2 * n_pg + 1:]
    phase = pl.program_id(1)
    j = pl.program_id(2)
    last = pl.num_programs(2) - 1
    nrow = 2 * ATT_HEADS

    @pl.when((phase == 0) & (j == 0))
    def _():
        row = lax.broadcasted_iota(I32, (nrow, ATT_WIDTH), 0)
        lane = lax.broadcasted_iota(I32, (nrow, ATT_WIDTH), 1)
        sel = (lane // LANES == row % ATT_HEADS) & ((lane // ATT_HEAD_DIM) % 2 == row // ATT_HEADS)
        qb = jnp.broadcast_to(q_ref[...].astype(F32), (nrow, ATT_WIDTH))
        qr_ref[...] = jnp.where(sel, qb, 0.0).astype(BF16)

    @pl.when(phase == 0)
    def _():
        qr = qr_ref[...]
        s_scr[j] = jnp.concatenate(
            [jnp.dot(qr, kr[...].astype(BF16), preferred_element_type=F32) for kr in k_refs], axis=1)

    @pl.when((phase == 1) & (j == 0))
    def _():
        sn = jnp.sum(qr_ref[...].astype(F32) * kn_ref[...].astype(F32), axis=1, keepdims=True)
        s = s_scr[...]
        m = jnp.maximum(jnp.max(jnp.max(s, axis=2, keepdims=True), axis=0, keepdims=True), sn[None])
        e = jnp.exp(s - m)
        en = jnp.exp(sn[None] - m)
        l = jnp.sum(jnp.sum(e, axis=2, keepdims=True), axis=0, keepdims=True) + en
        p = e / l
        pn = (en / l)[0]
        lam = _lambda_value(lq1, lk1, lq2, lk2)
        w = p[:, :ATT_HEADS, :] - lam * p[:, ATT_HEADS:, :]
        w_scr[...] = jnp.concatenate([w, jnp.zeros_like(w)], axis=1).astype(BF16)
        wn_scr[...] = pn[:ATT_HEADS] - lam * pn[ATT_HEADS:]
        acc_ref[...] = jnp.zeros(acc_ref.shape, F32)

    @pl.when(phase == 1)
    def _():
        wb = w_scr[j]
        page = wb.shape[1] // n_pg
        heads = []
        for h in range(ATT_HEADS):
            vh = jnp.concatenate([vr[pl.ds(h, page, stride=ATT_HEADS), :].astype(BF16) for vr in v_refs], axis=0)
            heads.append(jnp.dot(wb, vh, preferred_element_type=F32))
        acc_ref[...] = acc_ref[...] + jnp.concatenate(heads, axis=1)

    @pl.when((phase == 1) & (j == last))
    def _():
        wn = wn_scr[...].astype(BF16).astype(F32)
        od = acc_ref[...][:ATT_HEADS] + wn * vn_ref[...].astype(F32)
        own = (lax.broadcasted_iota(I32, od.shape, 1) // LANES) == lax.broadcasted_iota(I32, od.shape, 0)
        od = jnp.where(own, od, 0.0)
        ms = jnp.sum(od * od, axis=1, keepdims=True) / (2 * ATT_HEAD_DIM)
        y = od * lax.rsqrt(ms + RMS_EPS) * subln_ref[...] * (1.0 - LAMBDA_INIT)
        o_ref[...] = jnp.sum(y, axis=0, keepdims=True).astype(BF16)


def _attention_decode(q, k_new_bf, v_new_bf, cache_kt, cache_v, page_table, lq1, lk1, lq2, lk2, subln):
    Bd = q.shape[0]
    page = cache_kt.shape[2]
    n_pages = page_table.shape[1]
    n_pg = math.gcd(n_pages, PAGES_PER_STEP)
    assert n_pages % n_pg == 0
    subln_w = jnp.tile(subln, (1, ATT_HEADS))

    n_groups = n_pages // n_pg
    nrow = 2 * ATT_HEADS

    def k_spec(g):
        return pl.BlockSpec((None, ATT_WIDTH, page), lambda b, ph, j, pt: (
            pt[b * n_pages + jnp.where(ph == 0, j, n_groups - 1) * n_pg + g], 0, 0))

    def v_spec(g):
        def index(b, ph, j, pt):
            held = jnp.maximum(b - 1, 0) * n_pages + (n_groups - 1) * n_pg
            return pt[jnp.where(ph == 0, held, b * n_pages + j * n_pg) + g], 0, 0
        return pl.BlockSpec((None, page * ATT_HEADS, 2 * ATT_HEAD_DIM), index)

    row_spec = pl.BlockSpec((None, 1, ATT_WIDTH), lambda b, ph, j, pt: (b, 0, 0))
    small = lambda shape: pl.BlockSpec(shape, lambda b, ph, j, pt: (0,) * len(shape))
    grid_spec = pltpu.PrefetchScalarGridSpec(
        num_scalar_prefetch=1, grid=(Bd, 2, n_groups),
        in_specs=[row_spec, row_spec, row_spec] + [small((1, ATT_HEAD_DIM))] * 4 + [small((1, ATT_WIDTH))]
                 + [k_spec(g) for g in range(n_pg)] + [v_spec(g) for g in range(n_pg)],
        out_specs=row_spec,
        scratch_shapes=[pltpu.VMEM((nrow, ATT_WIDTH), BF16),
                        pltpu.VMEM((n_groups, nrow, n_pg * page), F32),
                        pltpu.VMEM((n_groups, nrow, n_pg * page), BF16),
                        pltpu.VMEM((ATT_HEADS, 1), F32),
                        pltpu.VMEM((nrow, ATT_WIDTH), F32)])
    out = pl.pallas_call(
        functools.partial(_decode_attn_kernel, n_pg=n_pg), grid_spec=grid_spec,
        out_shape=jax.ShapeDtypeStruct((Bd, 1, ATT_WIDTH), BF16),
        compiler_params=_params(("parallel", "arbitrary", "arbitrary")), name="attn_decode",
    )(page_table.reshape(-1), q.reshape(Bd, 1, -1), k_new_bf.reshape(Bd, 1, -1), v_new_bf.reshape(Bd, 1, -1),
      lq1, lk1, lq2, lk2, subln_w, *([cache_kt] * n_pg), *([cache_v] * n_pg))
    return out.reshape(Bd, ATT_WIDTH)


def _hgrn_masks():
    C = HG_CHUNK
    t = np.arange(C)[:, None]
    r = np.arange(C)[None, :]
    blocks = [(r <= t), (r > t)]
    for B in HG_LEVELS:
        mid = (t // B) * B + B // 2 - 1
        second = (t % B) >= B // 2
        blocks.append(np.where(second, (r > mid) & (r <= t), (r > t) & (r <= mid)))
    return jnp.asarray(np.concatenate(blocks, axis=0).astype(np.float32), dtype=BF16)


def _lower_bound(lb_ref):
    a = lb_ref[...].astype(F32)
    e = jnp.exp(a - jnp.max(a, axis=0, keepdims=True))
    return e[0:1] / jnp.sum(e, axis=0, keepdims=True)


def _group_rows(x, j):
    return jnp.broadcast_to(x[:, j:j + 1, :], x.shape)


def _split3(x):
    hi = x.astype(BF16)
    r1 = x - hi.astype(F32)
    mid = r1.astype(BF16)
    lo = (r1 - mid.astype(F32)).astype(BF16)
    return hi, mid, lo


def _hgrn_tile(q, z, v, lb, msk, s_t):
    C = HG_CHUNK
    n = q.shape[0] // C
    dot = lambda a, b: jnp.dot(a, b, preferred_element_type=F32)
    rows = lambda x, c: x[c * C:(c + 1) * C]
    chunks = range(n)

    logf = jnp.log(lb + (1.0 - lb) * jax.nn.sigmoid(z)) * math.log2(math.e)
    kk = (1.0 - lb) * jax.nn.sigmoid(-z)
    v_bf = v.astype(BF16)
    parts = _split3(logf)
    e = [sum(dot(msk, rows(p, c)) for p in parts) for c in chunks]
    b = [ec[0:C] for ec in e]
    qi = [(rows(q, c) * jnp.exp2(b[c])).astype(BF16) for c in chunks]
    kl = [(rows(kk, c) * jnp.exp2(e[c][C:2 * C])).astype(BF16) for c in chunks]
    kv = [lax.dot_general(rows(v_bf, c), kl[c], _TN, preferred_element_type=F32) for c in chunks]

    t_idx = lax.broadcasted_iota(I32, (C, 1), 0)
    row = lax.broadcasted_iota(I32, (C, C), 0)
    col = lax.broadcasted_iota(I32, (C, C), 1)
    a = [jnp.zeros((C, C), F32) for _ in chunks]
    for li, B in enumerate(HG_LEVELS):
        second = (t_idx % B) >= B // 2
        same_block = (row // B) == (col // B)
        for c in chunks:
            x = jnp.exp2(e[c][(2 + li) * C:(3 + li) * C])
            qt = jnp.where(second, rows(q, c) * x, 0.0).astype(BF16)
            kt = jnp.where(second, 0.0, rows(kk, c) * x).astype(BF16)
            al = lax.dot_general(qt, kt, _NT, preferred_element_type=F32)
            a[c] = a[c] + (jnp.where(same_block, al, 0.0) if B < C else al)

    states = [s_t]
    for c in chunks:
        states.append(states[-1] * jnp.exp2(b[c][C - 1:C, :]) + kv[c])
    o = [lax.dot_general(qi[c], states[c].astype(BF16), _NT, preferred_element_type=F32)
         + dot(a[c].astype(BF16), rows(v_bf, c)) for c in chunks]

    G = n * C // SUBLANES
    b_all = jnp.concatenate(b, axis=0) if n > 1 else b[0]
    q3, k3, v3, b3 = (x.reshape(G, SUBLANES, LANES) for x in (q, kk, v, b_all))
    p3 = lax.broadcasted_iota(I32, (G, SUBLANES, 1), 1)
    o3 = (jnp.concatenate(o, axis=0) if n > 1 else o[0]).reshape(G, SUBLANES, LANES)
    for j in range(SUBLANES):
        term = q3 * _group_rows(k3, j) * jnp.exp2(jnp.minimum(b3 - _group_rows(b3, j), 0.0))
        aj = jnp.where(p3 >= j, jnp.sum(term, axis=-1, keepdims=True), 0.0)
        o3 = o3 + aj * _group_rows(v3, j)
    return o3.reshape(n * C, LANES), states[-1]


def _hgrn_kernel(q_ref, z_ref, v_ref, g_ref, lb_ref, hgn_ref, msk_ref, o_ref, st_ref, s_scr):
    i = pl.program_id(1)

    @pl.when(i == 0)
    def _():
        s_scr[...] = jnp.zeros(s_scr.shape, F32)

    o, s_t = _hgrn_tile(q_ref[...], z_ref[...], v_ref[...], _lower_bound(lb_ref), msk_ref[...], s_scr[...])
    s_scr[...] = s_t
    g = g_ref[...]
    o_ref[...] = (_rms(o, hgn_ref[...]) * (g * jax.nn.sigmoid(g))).astype(BF16)

    @pl.when(i == pl.num_programs(1) - 1)
    def _():
        st_ref[...] = s_scr[...].T


def _hgrn_prompt(hg, hgrn_lb, hgrn_norm, tt):
    S = hg.shape[0]
    msk = _hgrn_masks()
    col = lambda seg: pl.BlockSpec((tt, LANES), lambda h, i, seg=seg: (i, seg * HG_HEADS + h))
    o, st = pl.pallas_call(
        _hgrn_kernel, grid=(HG_HEADS, S // tt),
        in_specs=[col(0), col(1), col(2), col(3),
                  pl.BlockSpec((hgrn_lb.shape[0], LANES), lambda h, i: (0, h)),
                  _small_spec((1, HG_VAL)), _small_spec(tuple(msk.shape))],
        out_specs=[pl.BlockSpec((tt, LANES), lambda h, i: (i, h)),
                   pl.BlockSpec((None, HG_KEY, HG_VAL), lambda h, i: (h, 0, 0))],
        out_shape=[jax.ShapeDtypeStruct((S, HG_WIDTH), BF16),
                   jax.ShapeDtypeStruct((HG_HEADS, HG_KEY, HG_VAL), F32)],
        scratch_shapes=[pltpu.VMEM((HG_VAL, HG_KEY), F32)],
        compiler_params=_params(("parallel", "arbitrary")), name="hgrn_prompt",
    )(hg, hg, hg, hg, hgrn_lb, hgrn_norm, msk)
    return o, st


def _hgrn_decode_kernel(hg_ref, s_ref, lb_ref, hgn_ref, o_ref, sn_ref):
    W = HG_WIDTH
    row = hg_ref[...]
    q, z, v, g = (row[:, s * W:(s + 1) * W] for s in range(4))
    lb = _lower_bound(lb_ref)
    f = lb + (1.0 - lb) * jax.nn.sigmoid(z)
    kk = (1.0 - lb) * jax.nn.sigmoid(-z)
    rnd = lambda x: x.astype(BF16).astype(F32)
    qf = rnd(q * f)
    qk = rnd(q) * rnd(kk)
    pad = jnp.zeros((SUBLANES - 3, LANES), F32)
    outs = []
    for h in range(HG_HEADS):
        hs = slice(h * LANES, (h + 1) * LANES)
        cols = jnp.concatenate([f[:, hs], kk[:, hs], qf[:, hs], pad], axis=0).T
        f_c, k_c, qf_c = cols[:, 0:1], cols[:, 1:2], cols[:, 2:3]
        s0 = s_ref[h]
        vh = v[:, hs]
        sn_ref[h] = f_c * s0 + k_c * vh
        o = (jnp.sum(qf_c * rnd(s0), axis=0, keepdims=True)
             + rnd(jnp.sum(qk[:, hs], axis=1, keepdims=True)) * rnd(vh))
        gh = g[:, hs]
        outs.append(_rms(o, hgn_ref[...]) * (gh * jax.nn.sigmoid(gh)))
    o_ref[...] = jnp.concatenate(outs, axis=1).astype(BF16)


def _hgrn_decode(hg, state, hgrn_lb, hgrn_norm):
    Bd = hg.shape[0]
    o, sn = pl.pallas_call(
        _hgrn_decode_kernel, grid=(Bd,),
        in_specs=[pl.BlockSpec((None, 1, 4 * HG_WIDTH), lambda b: (b, 0, 0)),
                  pl.BlockSpec((None, HG_HEADS, HG_KEY, HG_VAL), lambda b: (b, 0, 0, 0)),
                  _small_spec(tuple(hgrn_lb.shape)), _small_spec((1, HG_VAL))],
        out_specs=[pl.BlockSpec((None, 1, HG_WIDTH), lambda b: (b, 0, 0)),
                   pl.BlockSpec((None, HG_HEADS, HG_KEY, HG_VAL), lambda b: (b, 0, 0, 0))],
        out_shape=[jax.ShapeDtypeStruct((Bd, 1, HG_WIDTH), BF16),
                   jax.ShapeDtypeStruct(state.shape, F32)],
        compiler_params=_params(("parallel",)), name="hgrn_decode",
    )(hg.reshape(Bd, 1, -1), state, hgrn_lb, hgrn_norm)
    return o.reshape(Bd, HG_WIDTH), sn


def _merge_kernel(oa_ref, oh_ref, wa_ref, wb_ref, ga_ref, gb_ref, o_ref):
    a = jnp.dot(oa_ref[...], wa_ref[...].astype(BF16), preferred_element_type=F32)
    b = jnp.dot(oh_ref[...], wb_ref[...].astype(BF16), preferred_element_type=F32)
    ga = jax.nn.sigmoid(ga_ref[...].astype(F32))
    gb = jax.nn.sigmoid(gb_ref[...].astype(F32))
    o_ref[...] = (ga * a + gb * b).astype(BF16)


def _merge(oa, oh, w_pa, w_pb, gates, tm, tn, tag):
    M = oa.shape[0]
    nj = D_MODEL // tn
    return pl.pallas_call(
        _merge_kernel, grid=(M // tm, nj),
        in_specs=[pl.BlockSpec((tm, ATT_WIDTH), lambda i, j: (i, 0)),
                  pl.BlockSpec((tm, HG_WIDTH), lambda i, j: (i, 0)),
                  pl.BlockSpec((None, ATT_WIDTH, tn), lambda i, j: (0, 0, j)),
                  pl.BlockSpec((None, HG_WIDTH, tn), lambda i, j: (0, 0, j)),
                  pl.BlockSpec((tm, tn), lambda i, j: (i, j)),
                  pl.BlockSpec((tm, tn), lambda i, j: (i, j + nj))],
        out_specs=pl.BlockSpec((tm, tn), lambda i, j: (i, j)),
        out_shape=jax.ShapeDtypeStruct((M, D_MODEL), BF16),
        compiler_params=_params(("parallel", "arbitrary")), name=f"merge_{tag}",
    )(oa, oh, w_pa, w_pb, gates, gates)


def _cross_prompt_kernel(x_ref, g_ref, wq_ref, mk_ref, mv_ref, wo_ref, o_ref):
    x = x_ref[...]
    u = _rms(x, g_ref[...]).astype(BF16)
    q = jnp.dot(u, wq_ref[...], preferred_element_type=F32).astype(BF16)
    heads = []
    for h in range(X_HEADS):
        hs = slice(h * X_HEAD_DIM, (h + 1) * X_HEAD_DIM)
        s = lax.dot_general(q[:, hs], mk_ref[:, hs], _NT, preferred_element_type=F32) * (X_HEAD_DIM ** -0.5)
        p = jnp.exp(s - jnp.max(s, axis=1, keepdims=True))
        p = p / jnp.sum(p, axis=1, keepdims=True)
        heads.append(jnp.dot(p.astype(BF16), mv_ref[:, hs], preferred_element_type=F32))
    o = jnp.concatenate(heads, axis=1).astype(BF16)
    o_ref[...] = x + jnp.dot(o, wo_ref[...], preferred_element_type=F32)


def _cross_prompt(h1, g_cross, w_cq_bf, mk_bf, mv_bf, w_co_bf, tm):
    M = h1.shape[0]
    return pl.pallas_call(
        _cross_prompt_kernel, grid=(M // tm,),
        in_specs=[pl.BlockSpec((tm, D_MODEL), lambda i: (i, 0)), _small_spec((1, D_MODEL)),
                  _small_spec((D_MODEL, X_WIDTH)), _small_spec(tuple(mk_bf.shape)),
                  _small_spec(tuple(mv_bf.shape)), _small_spec((X_WIDTH, D_MODEL))],
        out_specs=pl.BlockSpec((tm, D_MODEL), lambda i: (i, 0)),
        out_shape=jax.ShapeDtypeStruct((M, D_MODEL), F32),
        compiler_params=_params(("parallel",)), name="cross_prompt",
    )(h1, g_cross.reshape(1, -1), w_cq_bf, mk_bf, mv_bf, w_co_bf)


def _cross_decode_kernel(q_ref, mk_ref, mv_ref, o_ref):
    rnd = lambda x: x.astype(BF16).astype(F32)
    q = rnd(q_ref[...])
    mem = mk_ref.shape[0] // X_HEADS
    outs = []
    for h in range(X_HEADS):
        hs = slice(h * X_HEAD_DIM, (h + 1) * X_HEAD_DIM)
        rows = pl.ds(h, mem, stride=X_HEADS)
        s = jnp.sum(rnd(mk_ref[rows, :]) * q[:, hs], axis=1, keepdims=True) * (X_HEAD_DIM ** -0.5)
        p = jnp.exp(s - jnp.max(s, axis=0, keepdims=True))
        p = p / jnp.sum(p, axis=0, keepdims=True)
        outs.append(jnp.sum(rnd(p) * rnd(mv_ref[rows, :]), axis=0, keepdims=True))
    o_ref[...] = jnp.concatenate(outs, axis=1).astype(BF16)


def _cross_decode(q, mem_k, mem_v):
    Bd, mem = mem_k.shape[0], mem_k.shape[1]
    row = pl.BlockSpec((None, 1, X_WIDTH), lambda b: (b, 0, 0))
    mem_spec = pl.BlockSpec((None, mem * X_HEADS, X_HEAD_DIM), lambda b: (b, 0, 0))
    flat = lambda m: m.reshape(Bd, mem * X_HEADS, X_HEAD_DIM)
    out = pl.pallas_call(
        _cross_decode_kernel, grid=(Bd,), in_specs=[row, mem_spec, mem_spec], out_specs=row,
        out_shape=jax.ShapeDtypeStruct((Bd, 1, X_WIDTH), BF16),
        compiler_params=_params(("parallel",)), name="cross_decode",
    )(q.reshape(Bd, 1, X_WIDTH), flat(mem_k), flat(mem_v))
    return out.reshape(Bd, X_WIDTH)


def _route_kernel(x_ref, g_ref, w_ref, b_ref, cin_ref, uin_ref, u_ref, eid_ref, wt_ref, rank_ref, cnt_ref,
                  c_scr, *, n_rows):
    i = pl.program_id(0)
    tm = x_ref.shape[0]

    @pl.when(i == 0)
    def _():
        c_scr[...] = cin_ref[...]

    u = _rms(x_ref[...], g_ref[...])
    u_ref[...] = u
    dot = lambda a, b: jnp.dot(a, b, preferred_element_type=F32)
    logits = dot(u.astype(BF16), w_ref[...]) + b_ref[...]

    lane = lax.broadcasted_iota(I32, (tm, LANES), 1)
    big = jnp.int32(1 << 20)
    first_max = lambda vals, vmax: jnp.min(jnp.where(vals == vmax, lane, big), axis=1, keepdims=True)
    glv = jnp.where(lane < N_GROUPS, logits, -jnp.inf)
    gmax = jnp.max(glv, axis=1, keepdims=True)
    gsel = first_max(glv, gmax)
    p_group = 1.0 / jnp.sum(jnp.exp(glv - gmax), axis=1, keepdims=True)
    e_lane = lane - N_GROUPS
    in_group = (e_lane >= 0) & (e_lane < N_EXPERTS) & (e_lane // EXPERTS_PER_GROUP == gsel)
    ev = jnp.where(in_group, logits, -jnp.inf)
    v1 = jnp.max(ev, axis=1, keepdims=True)
    i1 = first_max(ev, v1)
    ev2 = jnp.where(lane == i1, -jnp.inf, ev)
    v2 = jnp.max(ev2, axis=1, keepdims=True)
    i2 = first_max(ev2, v2)
    t2 = jnp.exp(v2 - v1)
    w1 = p_group / (1.0 + t2)
    w2 = w1 * t2
    e1 = i1 - N_GROUPS
    e2 = i2 - N_GROUPS
    eid_ref[...] = jnp.where(lane == 0, e1, jnp.where(lane == 1, e2, 0))
    wt_ref[...] = jnp.where(lane == 0, w1, jnp.where(lane == 1, w2, 0.0))

    valid = (lax.broadcasted_iota(I32, (tm, 1), 0) + i * tm) < n_rows
    oh1 = jnp.where((lane == e1) & valid, 1.0, 0.0)
    oh2 = jnp.where((lane == e2) & valid, 1.0, 0.0)
    cnt = (oh1 + oh2).astype(BF16)
    r_i = lax.broadcasted_iota(I32, (tm, tm), 0)
    c_i = lax.broadcasted_iota(I32, (tm, tm), 1)
    strict = jnp.where(c_i < r_i, 1.0, 0.0).astype(BF16)
    before = dot(strict, cnt) + c_scr[...]
    r1 = jnp.sum(oh1 * before, axis=1, keepdims=True)
    r2 = jnp.sum(oh2 * before, axis=1, keepdims=True)
    rank_ref[...] = jnp.where(lane == 0, r1, jnp.where(lane == 1, r2, 0.0)).astype(I32)
    c_scr[...] = c_scr[...] + jnp.sum(oh1 + oh2, axis=0, keepdims=True)
    cnt_ref[...] = c_scr[...]


def _route(x, g_ffn, w_router, b_router, counts_in, u_all, row_off, tm, tag):
    M = x.shape[0]
    T = u_all.shape[0]
    boff = row_off // tm
    kern = functools.partial(_route_kernel, n_rows=M)
    lane_out = lambda dt: jax.ShapeDtypeStruct((M, LANES), dt)
    outs = pl.pallas_call(
        kern, grid=(M // tm,),
        in_specs=[pl.BlockSpec((tm, D_MODEL), lambda i: (i, 0)), _small_spec((1, D_MODEL)),
                  _small_spec((D_MODEL, LANES)), _small_spec((1, LANES)), _small_spec((1, LANES)),
                  pl.BlockSpec(memory_space=pl.ANY)],
        out_specs=[pl.BlockSpec((tm, D_MODEL), lambda i: (i + boff, 0)),
                   pl.BlockSpec((tm, LANES), lambda i: (i, 0)), pl.BlockSpec((tm, LANES), lambda i: (i, 0)),
                   pl.BlockSpec((tm, LANES), lambda i: (i, 0)), _small_spec((1, LANES))],
        out_shape=[jax.ShapeDtypeStruct((T, D_MODEL), F32), lane_out(I32), lane_out(F32), lane_out(I32),
                   jax.ShapeDtypeStruct((1, LANES), F32)],
        scratch_shapes=[pltpu.VMEM((1, LANES), F32)],
        input_output_aliases={5: 0},
        compiler_params=_params(("arbitrary",)), name=f"moe_route_{tag}",
    )(x, g_ffn.reshape(1, -1), w_router, b_router, counts_in, u_all)
    return outs


def _plan_kernel(cnt_ref, pstart_ref, blkexp_ref, blkslot_ref, nxtexp_ref, nblk_ref, *, n_blocks):
    def per_expert(e, carry):
        acc, k = carry
        pstart_ref[e] = acc
        nb = (cnt_ref[e] + MOE_ROWS - 1) // MOE_ROWS
        first = acc // MOE_ROWS

        def fill(b, c):
            blkexp_ref[b] = e
            blkslot_ref[b] = k % 2
            return c

        lax.fori_loop(first, first + nb, fill, 0)
        return acc + nb * MOE_ROWS, k + jnp.where(nb > 0, 1, 0)

    total, _ = lax.fori_loop(0, N_EXPERTS, per_expert, (jnp.int32(0), jnp.int32(0)))
    used = total // MOE_ROWS
    nblk_ref[0] = used
    last = jnp.maximum(used - 1, 0)

    def backwards(t, carry):
        cur, nxt = carry
        b = last - t
        e = blkexp_ref[b]
        nxt = jnp.where(e != cur, cur, nxt)
        nxtexp_ref[b] = nxt
        return e, nxt

    lax.fori_loop(0, used, backwards, (blkexp_ref[last], jnp.int32(-1)))

    def tail(b, c):
        blkexp_ref[b] = blkexp_ref[last]
        blkslot_ref[b] = blkslot_ref[last]
        nxtexp_ref[b] = -1
        return c

    lax.fori_loop(used, n_blocks, tail, 0)


def _plan(counts, n_blocks):
    smem = pl.BlockSpec(memory_space=pltpu.SMEM)
    per_block = jax.ShapeDtypeStruct((n_blocks,), I32)
    return pl.pallas_call(
        functools.partial(_plan_kernel, n_blocks=n_blocks),
        in_specs=[smem], out_specs=[smem] * 5,
        out_shape=[jax.ShapeDtypeStruct((N_EXPERTS,), I32), per_block, per_block, per_block,
                   jax.ShapeDtypeStruct((1,), I32)],
        name="moe_plan",
    )(counts)


def _dispatch_kernel(eid_ref, rank_ref, pstart_ref, cnt_ref, nblk_ref, u_ref, xs_hbm, zbuf, sem_z, sem,
                     *, n_tokens, n_blocks):
    R = MOE_ROWS
    i = pl.program_id(0)
    tm = u_ref.shape[0]

    def zero_copy(b):
        return pltpu.make_async_copy(zbuf, xs_hbm.at[pl.ds(pl.multiple_of(b * R, R), R), :], sem_z)

    def partial_block(e):
        c = cnt_ref[e]
        return (c % R) != 0, (pstart_ref[e] + c) // R

    def zero_partial(start):
        def body(e, n):
            has, b = partial_block(e)

            @pl.when(has)
            def _():
                zero_copy(b).start() if start else zero_copy(b).wait()

            return n
        return body

    def zero_unused(start):
        def body(b, n):
            zero_copy(b).start() if start else zero_copy(b).wait()
            return n
        return body

    @pl.when(i == 0)
    def _():
        zbuf[...] = jnp.zeros(zbuf.shape, F32)
        for start in (True, False):
            lax.fori_loop(0, N_EXPERTS, zero_partial(start), 0)
            lax.fori_loop(nblk_ref[0], n_blocks, zero_unused(start), 0)

    def row_copy(r, k):
        a = 2 * (i * tm + r) + k
        d = pstart_ref[eid_ref[a]] + rank_ref[a]
        return pltpu.make_async_copy(u_ref.at[pl.ds(r, 1), :], xs_hbm.at[pl.ds(d, 1), :], sem)

    def start_row(r, n):
        row_copy(r, 0).start()
        row_copy(r, 1).start()
        return n

    def wait_row(r, n):
        row_copy(r, 0).wait()
        row_copy(r, 1).wait()
        return n

    rem = n_tokens % tm

    @pl.when((i + 1) * tm <= n_tokens)
    def _():
        lax.fori_loop(0, tm, start_row, 0, unroll=8)
        lax.fori_loop(0, tm, wait_row, 0, unroll=8)

    if rem:
        @pl.when((i + 1) * tm > n_tokens)
        def _():
            lax.fori_loop(0, rem, start_row, 0, unroll=8)
            lax.fori_loop(0, rem, wait_row, 0, unroll=8)


def _dispatch(eid, rank, pstart, counts, n_used, u_all, n_blocks, tm):
    T = u_all.shape[0]
    grid_spec = pltpu.PrefetchScalarGridSpec(
        num_scalar_prefetch=5, grid=(pl.cdiv(T, tm),),
        in_specs=[pl.BlockSpec((tm, D_MODEL), lambda i, *_: (i, 0))],
        out_specs=pl.BlockSpec(memory_space=pl.ANY),
        scratch_shapes=[pltpu.VMEM((MOE_ROWS, D_MODEL), F32), pltpu.SemaphoreType.DMA(()),
                        pltpu.SemaphoreType.DMA(())])
    return pl.pallas_call(
        functools.partial(_dispatch_kernel, n_tokens=T, n_blocks=n_blocks), grid_spec=grid_spec,
        out_shape=jax.ShapeDtypeStruct((n_blocks * MOE_ROWS, D_MODEL), F32),
        compiler_params=_params(("arbitrary",)), name="moe_dispatch",
    )(eid, rank, pstart, counts, n_used, u_all)


def _expert_kernel(blkexp_ref, blkslot_ref, nxtexp_ref, nblk_ref, x_ref, wg_hbm, wu_hbm, wd_hbm, o_ref,
                   wg_f, wu_f, wd_f, wg_s, wu_s, wd_s, sem):
    i = pl.program_id(0)

    def fetch(e, slot):
        return [pltpu.make_async_copy(src.at[0, e], dst.at[slot], sem.at[slot])
                for src, dst in ((wg_hbm, wg_f), (wu_hbm, wu_f), (wd_hbm, wd_f))]

    @pl.when(i < nblk_ref[0])
    def _():
        e = blkexp_ref[i]
        slot = blkslot_ref[i]
        nxt = nxtexp_ref[i]

        @pl.when(i == 0)
        def _():
            for cp in fetch(e, slot):
                cp.start()

        @pl.when((i == 0) | (e != blkexp_ref[jnp.maximum(i - 1, 0)]))
        def _():
            for cp in fetch(e, slot):
                cp.wait()

            @pl.when(nxt >= 0)
            def _():
                for cp in fetch(nxt, 1 - slot):
                    cp.start()

            wg_s[...] = wg_f[slot].astype(BF16)
            wu_s[...] = wu_f[slot].astype(BF16)
            wd_s[...] = wd_f[slot].astype(BF16)

        x = x_ref[...].astype(BF16)
        hg = jnp.dot(x, wg_s[...], preferred_element_type=F32)
        hu = jnp.dot(x, wu_s[...], preferred_element_type=F32)
        h = (hg * jax.nn.sigmoid(hg) * hu).astype(BF16)
        o_ref[...] = jnp.dot(h, wd_s[...], preferred_element_type=F32)

    @pl.when(i >= nblk_ref[0])
    def _():
        o_ref[...] = jnp.zeros(o_ref.shape, F32)


def _experts(xs, blk_exp, blk_slot, nxt_exp, n_used, we_g, we_u, we_d, n_blocks):
    hbm = pl.BlockSpec(memory_space=pl.ANY)
    up, down = (D_MODEL, EXPERT_FF), (EXPERT_FF, D_MODEL)
    grid_spec = pltpu.PrefetchScalarGridSpec(
        num_scalar_prefetch=4, grid=(n_blocks,),
        in_specs=[pl.BlockSpec((MOE_ROWS, D_MODEL),
                               lambda i, be, bs, nx, nb: (jnp.maximum(jnp.minimum(i, nb[0] - 1), 0), 0)),
                  hbm, hbm, hbm],
        out_specs=pl.BlockSpec((MOE_ROWS, D_MODEL), lambda i, *_: (i, 0)),
        scratch_shapes=[pltpu.VMEM((2,) + up, F32), pltpu.VMEM((2,) + up, F32), pltpu.VMEM((2,) + down, F32),
                        pltpu.VMEM(up, BF16), pltpu.VMEM(up, BF16), pltpu.VMEM(down, BF16),
                        pltpu.SemaphoreType.DMA((2,))])
    return pl.pallas_call(
        _expert_kernel, grid_spec=grid_spec,
        out_shape=jax.ShapeDtypeStruct((n_blocks * MOE_ROWS, D_MODEL), F32),
        compiler_params=_params(("arbitrary",)), name="moe_experts",
    )(blk_exp, blk_slot, nxt_exp, n_used, xs, we_g, we_u, we_d)


def _combine_kernel(eid_ref, rank_ref, pstart_ref, h_ref, wt_ref, gf_ref, yb_hbm, o_ref, ybuf0, ybuf1, sem,
                    *, dest_off):
    i = pl.program_id(0)
    tm = h_ref.shape[0]
    base = dest_off + i * (2 * tm)

    def row_copy(r, k, buf):
        a = base + 2 * r + k
        d = pstart_ref[eid_ref[a]] + rank_ref[a]
        return pltpu.make_async_copy(yb_hbm.at[pl.ds(d, 1), :], buf.at[pl.ds(r, 1), :], sem)

    def start(r, c):
        row_copy(r, 0, ybuf0).start()
        row_copy(r, 1, ybuf1).start()
        return c

    def wait(r, c):
        row_copy(r, 0, ybuf0).wait()
        row_copy(r, 1, ybuf1).wait()
        return c

    lax.fori_loop(0, tm, start, 0, unroll=4)
    lax.fori_loop(0, tm, wait, 0, unroll=4)
    wt = wt_ref[...]
    h = h_ref[...] + (wt[:, 0:1] * ybuf0[...] + wt[:, 1:2] * ybuf1[...])
    o_ref[...] = _rms(h, gf_ref[...])


def _combine(h2, wts, eid, rank, pstart, yb, norm_final, dest_off, tm, tag):
    M = h2.shape[0]
    grid_spec = pltpu.PrefetchScalarGridSpec(
        num_scalar_prefetch=3, grid=(M // tm,),
        in_specs=[pl.BlockSpec((tm, D_MODEL), lambda i, *_: (i, 0)), pl.BlockSpec((tm, LANES), lambda i, *_: (i, 0)),
                  pl.BlockSpec((1, D_MODEL), lambda i, *_: (0, 0)), pl.BlockSpec(memory_space=pl.ANY)],
        out_specs=pl.BlockSpec((tm, D_MODEL), lambda i, *_: (i, 0)),
        scratch_shapes=[pltpu.VMEM((tm, D_MODEL), F32), pltpu.VMEM((tm, D_MODEL), F32),
                        pltpu.SemaphoreType.DMA(())])
    return pl.pallas_call(
        functools.partial(_combine_kernel, dest_off=dest_off), grid_spec=grid_spec,
        out_shape=jax.ShapeDtypeStruct((M, D_MODEL), F32),
        compiler_params=_params(("arbitrary",)), name=f"moe_combine_{tag}",
    )(eid, rank, pstart, h2, wts, norm_final.reshape(1, -1), yb)


def _moe_and_final_norm(h2_p, h2_s, g_ffn, wr_g, br_g, wr_e, br_e, we_g, we_u, we_d, norm_final, tm_p):
    Tp, Ts = h2_p.shape[0], h2_s.shape[0]
    T = Tp + Ts
    pad = LANES - N_GROUPS - N_EXPERTS
    w_router = jnp.concatenate([wr_g, wr_e, jnp.zeros((D_MODEL, pad), F32)], axis=1).astype(BF16)
    b_router = jnp.concatenate([br_g, br_e, jnp.zeros((pad,), F32)]).reshape(1, LANES)
    u_all = jnp.zeros((T, D_MODEL), F32)
    zero_counts = jnp.zeros((1, LANES), F32)
    u_all, eid_p, wt_p, rank_p, counts = _route(h2_p, g_ffn, w_router, b_router, zero_counts, u_all, 0, tm_p, "p")
    u_all, eid_s, wt_s, rank_s, counts = _route(h2_s, g_ffn, w_router, b_router, counts, u_all, Tp, Ts, "s")
    eid = jnp.concatenate([eid_p[:, :2].reshape(-1), eid_s[:, :2].reshape(-1)])
    rank = jnp.concatenate([rank_p[:, :2].reshape(-1), rank_s[:, :2].reshape(-1)])
    A = 2 * T
    n_blocks = (A + N_EXPERTS * (MOE_ROWS - 1)) // MOE_ROWS + 1
    cnt = counts[0, :N_EXPERTS].astype(I32)
    pstart, blk_exp, blk_slot, nxt_exp, n_used = _plan(cnt, n_blocks)
    xs = _dispatch(eid, rank, pstart, cnt, n_used, u_all, n_blocks, 256)
    yb = _experts(xs, blk_exp, blk_slot, nxt_exp, n_used, we_g, we_u, we_d, n_blocks)
    y_p = _combine(h2_p, wt_p, eid, rank, pstart, yb, norm_final, 0, 256, "p")
    y_s = _combine(h2_s, wt_s, eid, rank, pstart, yb, norm_final, 2 * Tp, Ts, "s")
    return y_p, y_s


def kernel(x_prompt, x_sample, mem_prompt, cache_k, cache_v, cache_mem_k, cache_mem_v, state_hgrn, page_table,
           norm_mix, w_in, lambda_q1, lambda_k1, lambda_q2, lambda_k2, subln, hgrn_lb, hgrn_norm, w_pa, w_pb,
           w_out, norm_cross, w_cq, w_ck, w_cv, w_co, norm_ffn, w_router_group, b_router_group,
           w_router_expert, b_router_expert, w_e_gate, w_e_up, w_e_down, norm_final):
    assert w_in.shape[0] == 1, "single-layer step"
    Bp, S, D = x_prompt.shape
    Bd, Ld, _ = x_sample.shape
    assert Bp == 1 and Ld == 1
    n_pages = page_table.shape[1]
    page = cache_k.shape[2]
    past_len = n_pages * page
    xp = x_prompt.reshape(S, D)
    xs = x_sample.reshape(Bd, D)
    lam = (lambda_q1, lambda_k1, lambda_q2, lambda_k2)
    TM = 1024
    w_in = w_in.astype(BF16)

    q, k, k_bf, v, v_bf, hg, gates = _in_projection(xp, norm_mix[0], w_in, jnp.arange(S, dtype=I32), TM, 512, "p",
                                                    BF16, ATT_HEAD_DIM ** -0.5 * math.log2(math.e))
    oa = _attention_prompt(q, k_bf, v_bf, *lam, subln, ATTN_TQ, ATTN_TK)
    oh, st_p = _hgrn_prompt(hg, hgrn_lb, hgrn_norm, 512)
    merged = _merge(oa, oh, w_pa, w_pb, gates, TM, 512, "p")
    (h1_p,) = _mm(merged, w_out, col_off=0, n_cols=D, tm=TM, tn=512, epilogue=_epi_residual, out_dtypes=[F32],
                  extras=[(xp, pl.BlockSpec((TM, 512), lambda i, j: (i, j)))], name="outproj_p")

    pos_s = jnp.full((Bd,), past_len, I32)
    qs, ks, ks_bf, vs, vs_bf, hgs, gates_s = _in_projection(xs, norm_mix[0], w_in, pos_s, Bd, 512, "s", F32,
                                                            ATT_HEAD_DIM ** -0.5)
    ck = jnp.transpose(cache_k[0], (0, 2, 3, 4, 1)).reshape(cache_k.shape[1], ATT_WIDTH, page)
    cv = cache_v[0].reshape(cache_v.shape[1], page * ATT_HEADS, 2 * ATT_HEAD_DIM)
    oa_s = _attention_decode(qs, ks_bf, vs_bf, ck, cv, page_table, *lam, subln)
    oh_s, st_s = _hgrn_decode(hgs, state_hgrn[0], hgrn_lb, hgrn_norm)
    merged_s = _merge(oa_s, oh_s, w_pa, w_pb, gates_s, Bd, 512, "s")
    (h1_s,) = _mm(merged_s, w_out, col_off=0, n_cols=D, tm=Bd, tn=512, epilogue=_epi_residual, out_dtypes=[F32],
                  extras=[(xs, pl.BlockSpec((Bd, 512), lambda i, j: (i, j)))], name="outproj_s")

    mem = mem_prompt.reshape(-1, D)
    mk, mk_bf = _mm(mem, w_ck, col_off=0, n_cols=X_WIDTH, tm=mem.shape[0], tn=X_WIDTH, epilogue=_epi_v,
                    out_dtypes=[F32, BF16], name="mem_k")
    mv, mv_bf = _mm(mem, w_cv, col_off=0, n_cols=X_WIDTH, tm=mem.shape[0], tn=X_WIDTH, epilogue=_epi_v,
                    out_dtypes=[F32, BF16], name="mem_v")
    w_cq_bf = w_cq[0].astype(BF16)
    w_co_bf = w_co[0].astype(BF16)
    h2_p = _cross_prompt(h1_p, norm_cross[0], w_cq_bf, mk_bf, mv_bf, w_co_bf, 256)
    (qc_s,) = _mm(h1_s, w_cq, col_off=0, n_cols=X_WIDTH, tm=Bd, tn=X_WIDTH, epilogue=_epi_plain,
                  out_dtypes=[F32], norm_g=norm_cross[0], name="cross_q_s")
    oc_s = _cross_decode(qc_s, cache_mem_k[0], cache_mem_v[0])
    (h2_s,) = _mm(oc_s, w_co, col_off=0, n_cols=D, tm=Bd, tn=512, epilogue=_epi_residual, out_dtypes=[F32],
                  extras=[(h1_s, pl.BlockSpec((Bd, 512), lambda i, j: (i, j)))], name="cross_o_s")

    y_p, y_s = _moe_and_final_norm(h2_p, h2_s, norm_ffn[0], w_router_group[0], b_router_group[0],
                                   w_router_expert[0], b_router_expert[0], w_e_gate, w_e_up, w_e_down,
                                   norm_final, 256)

    return (y_p.reshape(Bp, S, D), y_s.reshape(Bd, Ld, D),
            k.reshape(1, Bp, S, ATT_HEADS, 2, ATT_HEAD_DIM), v.reshape(1, Bp, S, ATT_HEADS, 2 * ATT_HEAD_DIM),
            ks.reshape(1, Bd, Ld, ATT_HEADS, 2, ATT_HEAD_DIM), vs.reshape(1, Bd, Ld, ATT_HEADS, 2 * ATT_HEAD_DIM),
            st_p.reshape(1, Bp, HG_HEADS, HG_KEY, HG_VAL), st_s.reshape(1, Bd, HG_HEADS, HG_KEY, HG_VAL),
            mk.reshape(1, Bp, -1, X_HEADS, X_HEAD_DIM), mv.reshape(1, Bp, -1, X_HEADS, X_HEAD_DIM))
```

```python
import functools
import math

import numpy as np
import jax
import jax.numpy as jnp
from jax import lax
from jax.experimental import pallas as pl
from jax.experimental.pallas import tpu as pltpu

F32 = jnp.float32
BF16 = jnp.bfloat16
I32 = jnp.int32

D_MODEL = 2048
ATT_HEADS = 8
ATT_HEAD_DIM = 64
ATT_WIDTH = ATT_HEADS * 2 * ATT_HEAD_DIM
ROPE_THETA = 10000.0
HG_HEADS = 8
HG_KEY = 128
HG_VAL = 128
HG_WIDTH = HG_HEADS * HG_VAL
X_HEADS = 4
X_HEAD_DIM = 128
X_WIDTH = X_HEADS * X_HEAD_DIM
N_GROUPS = 4
EXPERTS_PER_GROUP = 8
N_EXPERTS = N_GROUPS * EXPERTS_PER_GROUP
EXPERT_FF = 512
RMS_EPS = 1e-6
NEG_INF = -1e30
LAMBDA_INIT = 0.8 - 0.6 * math.exp(-0.3 * 0)

LANES = 128
SUBLANES = 8
VMEM_LIMIT = 52 * 1024 * 1024

HG_CHUNK = 64
HG_LEVELS = (16, 32, 64)
MOE_ROWS = 256
PAGES_PER_STEP = 16
ATTN_TQ = 1024
ATTN_TK = 1024

_NT = (((1,), (1,)), ((), ()))
_TN = (((0,), (0,)), ((), ()))


def _params(sem):
    return pltpu.CompilerParams(dimension_semantics=sem, vmem_limit_bytes=VMEM_LIMIT)


def _rms(x, g):
    return x * lax.rsqrt(jnp.mean(x * x, axis=-1, keepdims=True) + RMS_EPS) * g


def _mm_kernel(*refs, n_extra, n_out, norm, epilogue):
    x_ref = refs[0]
    pos = 1
    if norm:
        g_ref = refs[1]
        pos = 2
    w_ref = refs[pos]
    extras = refs[pos + 1:pos + 1 + n_extra]
    outs = refs[pos + 1 + n_extra:pos + 1 + n_extra + n_out]
    if norm:
        u_ref = refs[-1]

        @pl.when(pl.program_id(1) == 0)
        def _():
            u_ref[...] = _rms(x_ref[...].astype(F32), g_ref[...]).astype(BF16)

        u = u_ref[...]
    else:
        u = x_ref[...].astype(BF16)
    acc = jnp.dot(u, w_ref[...].astype(BF16), preferred_element_type=F32)
    epilogue(acc, extras, outs)


def _mm(x, w, *, col_off, n_cols, tm, tn, epilogue, out_dtypes, norm_g=None, extras=(), name):
    M, K = x.shape
    assert M % tm == 0 and n_cols % tn == 0 and col_off % tn == 0
    joff = col_off // tn
    if w.ndim == 3:
        w_spec = pl.BlockSpec((None, K, tn), lambda i, j: (0, 0, j + joff))
    else:
        w_spec = pl.BlockSpec((K, tn), lambda i, j: (0, j + joff))
    in_specs = [pl.BlockSpec((tm, K), lambda i, j: (i, 0))]
    args = [x]
    if norm_g is not None:
        in_specs.append(pl.BlockSpec((1, K), lambda i, j: (0, 0)))
        args.append(norm_g.reshape(1, K))
    in_specs.append(w_spec)
    args.append(w)
    for arr, spec in extras:
        in_specs.append(spec)
        args.append(arr)
    out_specs = [pl.BlockSpec((tm, tn), lambda i, j: (i, j)) for _ in out_dtypes]
    out_shape = [jax.ShapeDtypeStruct((M, n_cols), dt) for dt in out_dtypes]
    scratch = [pltpu.VMEM((tm, K), BF16)] if norm_g is not None else []
    kern = functools.partial(_mm_kernel, n_extra=len(extras), n_out=len(out_dtypes),
                             norm=norm_g is not None, epilogue=epilogue)
    res = pl.pallas_call(
        kern, grid=(M // tm, n_cols // tn), in_specs=in_specs, out_specs=out_specs,
        out_shape=out_shape, scratch_shapes=scratch,
        compiler_params=_params(("parallel", "arbitrary")), name=name)(*args)
    return res


def _rope_tile(x, cos, sin_signed):
    first = (lax.broadcasted_iota(I32, (x.shape[0], LANES), 1) % ATT_HEAD_DIM) < ATT_HEAD_DIM // 2
    outs = []
    for c in range(x.shape[1] // LANES):
        xc = x[:, c * LANES:(c + 1) * LANES]
        rot = jnp.where(first, pltpu.roll(xc, LANES - ATT_HEAD_DIM // 2, 1),
                        pltpu.roll(xc, ATT_HEAD_DIM // 2, 1))
        outs.append(xc * cos + rot * sin_signed)
    return outs[0] if len(outs) == 1 else jnp.concatenate(outs, axis=1)


def _epi_q(acc, extras, outs, q_scale):
    cos_ref, sin_ref = extras
    outs[0][...] = (_rope_tile(acc, cos_ref[...], sin_ref[...]) * q_scale).astype(BF16)


def _epi_k(acc, extras, outs):
    cos_ref, sin_ref = extras
    r = _rope_tile(acc, cos_ref[...], sin_ref[...])
    outs[0][...] = r
    outs[1][...] = r.astype(BF16)


def _epi_v(acc, extras, outs):
    outs[0][...] = acc
    outs[1][...] = acc.astype(BF16)


def _epi_plain(acc, extras, outs):
    outs[0][...] = acc.astype(outs[0].dtype)


def _epi_residual(acc, extras, outs):
    outs[0][...] = extras[0][...] + acc


def _rope_tables(pos):
    half = ATT_HEAD_DIM // 2
    freqs = ROPE_THETA ** (-jnp.arange(half, dtype=F32) / half)
    ang = pos.astype(F32)[:, None] * freqs[None, :]
    c, s = jnp.cos(ang), jnp.sin(ang)
    return jnp.tile(c, (1, 4)), jnp.concatenate([-s, s, -s, s], axis=1)


def _inproj_kernel(x_ref, g_ref, w_ref, cos_ref, sin_ref, q_ref, k_ref, kb_ref, v_ref, vb_ref, hg_ref, gt_ref,
                   u_ref, *, bounds, q_scale):
    j = pl.program_id(1)

    @pl.when(j == 0)
    def _():
        u_ref[...] = _rms(x_ref[...].astype(F32), g_ref[...]).astype(BF16)

    acc = jnp.dot(u_ref[...], w_ref[...], preferred_element_type=F32)
    in_seg = lambda s: (j >= bounds[s]) & (j < bounds[s + 1])

    @pl.when(in_seg(0))
    def _():
        _epi_q(acc, (cos_ref, sin_ref), (q_ref,), q_scale)

    @pl.when(in_seg(1))
    def _():
        _epi_k(acc, (cos_ref, sin_ref), (k_ref, kb_ref))

    @pl.when(in_seg(2))
    def _():
        _epi_v(acc, (), (v_ref, vb_ref))

    @pl.when(in_seg(3))
    def _():
        hg_ref[...] = acc

    @pl.when(in_seg(4))
    def _():
        gt_ref[...] = acc.astype(gt_ref.dtype)


def _in_projection(x, g, w_in_bf, pos, tm, tn, tag, gate_dtype, q_scale):
    M, K = x.shape
    cos, sin = _rope_tables(pos)
    widths = (ATT_WIDTH, ATT_WIDTH, ATT_WIDTH, 4 * HG_WIDTH, 2 * D_MODEL)
    bounds = tuple(int(b) for b in np.cumsum((0,) + widths) // tn)

    def seg_spec(s):
        lo, n = bounds[s], bounds[s + 1] - bounds[s]
        return pl.BlockSpec((tm, tn), lambda i, j: (i, jnp.clip(j - lo, 0, n - 1)))

    seg_of_out = (0, 1, 1, 2, 2, 3, 4)
    out_dtypes = (BF16, F32, BF16, F32, BF16, F32, gate_dtype)
    row_tbl = pl.BlockSpec((tm, LANES), lambda i, j: (i, 0))
    return pl.pallas_call(
        functools.partial(_inproj_kernel, bounds=bounds, q_scale=q_scale),
        grid=(M // tm, bounds[-1]),
        in_specs=[pl.BlockSpec((tm, K), lambda i, j: (i, 0)),
                  pl.BlockSpec((1, K), lambda i, j: (0, 0)),
                  pl.BlockSpec((None, K, tn), lambda i, j: (0, 0, j)), row_tbl, row_tbl],
        out_specs=[seg_spec(s) for s in seg_of_out],
        out_shape=[jax.ShapeDtypeStruct((M, widths[s]), dt) for s, dt in zip(seg_of_out, out_dtypes)],
        scratch_shapes=[pltpu.VMEM((tm, K), BF16)],
        compiler_params=_params(("parallel", "arbitrary")), name=f"inproj_{tag}",
    )(x, g.reshape(1, K), w_in_bf, cos, sin)


def _lambda_value(lq1, lk1, lq2, lk2):
    return (jnp.exp(jnp.sum(lq1[...] * lk1[...], axis=-1, keepdims=True))
            - jnp.exp(jnp.sum(lq2[...] * lk2[...], axis=-1, keepdims=True)) + LAMBDA_INIT)


def _attn_kernel(q_ref, k_ref, v_ref, lq1, lk1, lq2, lk2, subln_ref, o_ref, *, tq, tk):
    i = pl.program_id(1)
    q = q_ref[...]
    lane = lax.broadcasted_iota(I32, q.shape, 1)
    zero = jnp.zeros_like(q)
    qm = (jnp.where(lane < ATT_HEAD_DIM, q, zero), jnp.where(lane >= ATT_HEAD_DIM, q, zero))

    def update(off, carry, mask):
        off = pl.multiple_of(off, tk)
        kt = k_ref[pl.ds(off, tk), :]
        vt = v_ref[pl.ds(off, tk), :]
        new = []
        for m in range(2):
            mx, l, acc = carry[3 * m:3 * m + 3]
            s = lax.dot_general(qm[m], kt, _NT, preferred_element_type=F32)
            if mask is not None:
                s = jnp.where(mask, s, NEG_INF)
            mn = jnp.maximum(mx, jnp.max(s, axis=1, keepdims=True))
            alpha = jnp.exp2(mx - mn)
            p = jnp.exp2(s - mn)
            l = alpha * l + jnp.sum(p, axis=1, keepdims=True)
            acc = alpha * acc + jnp.dot(p.astype(BF16), vt, preferred_element_type=F32)
            new += [mn, l, acc]
        return tuple(new)

    def body(j, carry):
        return update(j * tk, carry, None)

    init = (jnp.full((tq, 1), NEG_INF, F32), jnp.zeros((tq, 1), F32), jnp.zeros((tq, LANES), F32)) * 2
    carry = lax.fori_loop(0, i * (tq // tk), body, init)
    row = lax.broadcasted_iota(I32, (tq, tk), 0)
    col = lax.broadcasted_iota(I32, (tq, tk), 1)
    for d in range(tq // tk):
        carry = update(i * tq + d * tk, carry, col + d * tk <= row)
    _, l0, a0, _, l1, a1 = carry

    lam = _lambda_value(lq1, lk1, lq2, lk2)
    o = a0 / l0 - lam * (a1 / l1)
    o_ref[...] = (_rms(o, subln_ref[...]) * (1.0 - LAMBDA_INIT)).astype(BF16)


def _small_spec(shape):
    nd = len(shape)
    return pl.BlockSpec(shape, lambda *_: (0,) * nd)


def _attention_prompt(q, k_bf, v_bf, lq1, lk1, lq2, lk2, subln, tq, tk):
    S = q.shape[0]
    assert tq % tk == 0
    lam_specs = [_small_spec((1, ATT_HEAD_DIM))] * 4
    return pl.pallas_call(
        functools.partial(_attn_kernel, tq=tq, tk=tk),
        grid=(ATT_HEADS, S // tq),
        in_specs=[pl.BlockSpec((tq, LANES), lambda h, i: (i, h)),
                  pl.BlockSpec((S, LANES), lambda h, i: (0, h)),
                  pl.BlockSpec((S, LANES), lambda h, i: (0, h))] + lam_specs
                 + [_small_spec((1, LANES))],
        out_specs=pl.BlockSpec((tq, LANES), lambda h, i: (i, h)),
        out_shape=jax.ShapeDtypeStruct((S, ATT_WIDTH), BF16),
        compiler_params=_params(("parallel", "arbitrary")), name="attn_prompt",
    )(q, k_bf, v_bf, lq1, lk1, lq2, lk2, subln)


def _decode_attn_kernel(pt_ref, q_ref, kn_ref, vn_ref, lq1, lk1, lq2, lk2, subln_ref, *rest, n_pg):
    k_refs = rest[:n_pg]
    v_refs = rest[n_pg:2 * n_pg]
    o_ref = rest[2 * n_pg]
    qr_ref, s_scr, w_scr, wn_scr, acc_ref = rest[2 * n_pg + 1:]
    phase = pl.program_id(1)
    j = pl.program_id(2)
    last = pl.num_programs(2) - 1
    nrow = 2 * ATT_HEADS

    @pl.when((phase == 0) & (j == 0))
    def _():
        row = lax.broadcasted_iota(I32, (nrow, ATT_WIDTH), 0)
        lane = lax.broadcasted_iota(I32, (nrow, ATT_WIDTH), 1)
        sel = (lane // LANES == row % ATT_HEADS) & ((lane // ATT_HEAD_DIM) % 2 == row // ATT_HEADS)
        qb = jnp.broadcast_to(q_ref[...].astype(F32), (nrow, ATT_WIDTH))
        qr_ref[...] = jnp.where(sel, qb, 0.0).astype(BF16)

    @pl.when(phase == 0)
    def _():
        qr = qr_ref[...]
        s_scr[j] = jnp.concatenate(
            [jnp.dot(qr, kr[...].astype(BF16), preferred_element_type=F32) for kr in k_refs], axis=1)

    @pl.when((phase == 1) & (j == 0))
    def _():
        sn = jnp.sum(qr_ref[...].astype(F32) * kn_ref[...].astype(F32), axis=1, keepdims=True)
        s = s_scr[...]
        m = jnp.maximum(jnp.max(jnp.max(s, axis=2, keepdims=True), axis=0, keepdims=True), sn[None])
        e = jnp.exp(s - m)
        en = jnp.exp(sn[None] - m)
        l = jnp.sum(jnp.sum(e, axis=2, keepdims=True), axis=0, keepdims=True) + en
        p = e / l
        pn = (en / l)[0]
        lam = _lambda_value(lq1, lk1, lq2, lk2)
        w = p[:, :ATT_HEADS, :] - lam * p[:, ATT_HEADS:, :]
        w_scr[...] = jnp.concatenate([w, jnp.zeros_like(w)], axis=1).astype(BF16)
        wn_scr[...] = pn[:ATT_HEADS] - lam * pn[ATT_HEADS:]
        acc_ref[...] = jnp.zeros(acc_ref.shape, F32)

    @pl.when(phase == 1)
    def _():
        wb = w_scr[j]
        page = wb.shape[1] // n_pg
        heads = []
        for h in range(ATT_HEADS):
            vh = jnp.concatenate([vr[pl.ds(h, page, stride=ATT_HEADS), :].astype(BF16) for vr in v_refs], axis=0)
            heads.append(jnp.dot(wb, vh, preferred_element_type=F32))
        acc_ref[...] = acc_ref[...] + jnp.concatenate(heads, axis=1)

    @pl.when((phase == 1) & (j == last))
    def _():
        wn = wn_scr[...].astype(BF16).astype(F32)
        od = acc_ref[...][:ATT_HEADS] + wn * vn_ref[...].astype(F32)
        own = (lax.broadcasted_iota(I32, od.shape, 1) // LANES) == lax.broadcasted_iota(I32, od.shape, 0)
        od = jnp.where(own, od, 0.0)
        ms = jnp.sum(od * od, axis=1, keepdims=True) / (2 * ATT_HEAD_DIM)
        y = od * lax.rsqrt(ms + RMS_EPS) * subln_ref[...] * (1.0 - LAMBDA_INIT)
        o_ref[...] = jnp.sum(y, axis=0, keepdims=True).astype(BF16)


def _attention_decode(q, k_new_bf, v_new_bf, cache_kt, cache_v, page_table, lq1, lk1, lq2, lk2, subln):
    Bd = q.shape[0]
    page = cache_kt.shape[2]
    n_pages = page_table.shape[1]
    n_pg = math.gcd(n_pages, PAGES_PER_STEP)
    assert n_pages % n_pg == 0
    subln_w = jnp.tile(subln, (1, ATT_HEADS))

    n_groups = n_pages // n_pg
    nrow = 2 * ATT_HEADS

    def k_spec(g):
        return pl.BlockSpec((None, ATT_WIDTH, page), lambda b, ph, j, pt: (
            pt[b * n_pages + jnp.where(ph == 0, j, n_groups - 1) * n_pg + g], 0, 0))

    def v_spec(g):
        def index(b, ph, j, pt):
            held = jnp.maximum(b - 1, 0) * n_pages + (n_groups - 1) * n_pg
            return pt[jnp.where(ph == 0, held, b * n_pages + j * n_pg) + g], 0, 0
        return pl.BlockSpec((None, page * ATT_HEADS, 2 * ATT_HEAD_DIM), index)

    row_spec = pl.BlockSpec((None, 1, ATT_WIDTH), lambda b, ph, j, pt: (b, 0, 0))
    small = lambda shape: pl.BlockSpec(shape, lambda b, ph, j, pt: (0,) * len(shape))
    grid_spec = pltpu.PrefetchScalarGridSpec(
        num_scalar_prefetch=1, grid=(Bd, 2, n_groups),
        in_specs=[row_spec, row_spec, row_spec] + [small((1, ATT_HEAD_DIM))] * 4 + [small((1, ATT_WIDTH))]
                 + [k_spec(g) for g in range(n_pg)] + [v_spec(g) for g in range(n_pg)],
        out_specs=row_spec,
        scratch_shapes=[pltpu.VMEM((nrow, ATT_WIDTH), BF16),
                        pltpu.VMEM((n_groups, nrow, n_pg * page), F32),
                        pltpu.VMEM((n_groups, nrow, n_pg * page), BF16),
                        pltpu.VMEM((ATT_HEADS, 1), F32),
                        pltpu.VMEM((nrow, ATT_WIDTH), F32)])
    out = pl.pallas_call(
        functools.partial(_decode_attn_kernel, n_pg=n_pg), grid_spec=grid_spec,
        out_shape=jax.ShapeDtypeStruct((Bd, 1, ATT_WIDTH), BF16),
        compiler_params=_params(("parallel", "arbitrary", "arbitrary")), name="attn_decode",
    )(page_table.reshape(-1), q.reshape(Bd, 1, -1), k_new_bf.reshape(Bd, 1, -1), v_new_bf.reshape(Bd, 1, -1),
      lq1, lk1, lq2, lk2, subln_w, *([cache_kt] * n_pg), *([cache_v] * n_pg))
    return out.reshape(Bd, ATT_WIDTH)


def _hgrn_masks():
    C = HG_CHUNK
    t = np.arange(C)[:, None]
    r = np.arange(C)[None, :]
    blocks = [(r <= t), (r > t)]
    for B in HG_LEVELS:
        mid = (t // B) * B + B // 2 - 1
        second = (t % B) >= B // 2
        blocks.append(np.where(second, (r > mid) & (r <= t), (r > t) & (r <= mid)))
    return jnp.asarray(np.concatenate(blocks, axis=0).astype(np.float32), dtype=BF16)


def _lower_bound(lb_ref):
    a = lb_ref[...].astype(F32)
    e = jnp.exp(a - jnp.max(a, axis=0, keepdims=True))
    return e[0:1] / jnp.sum(e, axis=0, keepdims=True)


def _group_rows(x, j):
    return jnp.broadcast_to(x[:, j:j + 1, :], x.shape)


def _split3(x):
    hi = x.astype(BF16)
    r1 = x - hi.astype(F32)
    mid = r1.astype(BF16)
    lo = (r1 - mid.astype(F32)).astype(BF16)
    return hi, mid, lo


def _hgrn_tile(q, z, v, lb, msk, s_t):
    C = HG_CHUNK
    n = q.shape[0] // C
    dot = lambda a, b: jnp.dot(a, b, preferred_element_type=F32)
    rows = lambda x, c: x[c * C:(c + 1) * C]
    chunks = range(n)

    logf = jnp.log(lb + (1.0 - lb) * jax.nn.sigmoid(z)) * math.log2(math.e)
    kk = (1.0 - lb) * jax.nn.sigmoid(-z)
    v_bf = v.astype(BF16)
    parts = _split3(logf)
    e = [sum(dot(msk, rows(p, c)) for p in parts) for c in chunks]
    b = [ec[0:C] for ec in e]
    qi = [(rows(q, c) * jnp.exp2(b[c])).astype(BF16) for c in chunks]
    kl = [(rows(kk, c) * jnp.exp2(e[c][C:2 * C])).astype(BF16) for c in chunks]
    kv = [lax.dot_general(rows(v_bf, c), kl[c], _TN, preferred_element_type=F32) for c in chunks]

    t_idx = lax.broadcasted_iota(I32, (C, 1), 0)
    row = lax.broadcasted_iota(I32, (C, C), 0)
    col = lax.broadcasted_iota(I32, (C, C), 1)
    a = [jnp.zeros((C, C), F32) for _ in chunks]
    for li, B in enumerate(HG_LEVELS):
        second = (t_idx % B) >= B // 2
        same_block = (row // B) == (col // B)
        for c in chunks:
            x = jnp.exp2(e[c][(2 + li) * C:(3 + li) * C])
            qt = jnp.where(second, rows(q, c) * x, 0.0).astype(BF16)
            kt = jnp.where(second, 0.0, rows(kk, c) * x).astype(BF16)
            al = lax.dot_general(qt, kt, _NT, preferred_element_type=F32)
            a[c] = a[c] + (jnp.where(same_block, al, 0.0) if B < C else al)

    states = [s_t]
    for c in chunks:
        states.append(states[-1] * jnp.exp2(b[c][C - 1:C, :]) + kv[c])
    o = [lax.dot_general(qi[c], states[c].astype(BF16), _NT, preferred_element_type=F32)
         + dot(a[c].astype(BF16), rows(v_bf, c)) for c in chunks]

    G = n * C // SUBLANES
    b_all = jnp.concatenate(b, axis=0) if n > 1 else b[0]
    q3, k3, v3, b3 = (x.reshape(G, SUBLANES, LANES) for x in (q, kk, v, b_all))
    p3 = lax.broadcasted_iota(I32, (G, SUBLANES, 1), 1)
    o3 = (jnp.concatenate(o, axis=0) if n > 1 else o[0]).reshape(G, SUBLANES, LANES)
    for j in range(SUBLANES):
        term = q3 * _group_rows(k3, j) * jnp.exp2(jnp.minimum(b3 - _group_rows(b3, j), 0.0))
        aj = jnp.where(p3 >= j, jnp.sum(term, axis=-1, keepdims=True), 0.0)
        o3 = o3 + aj * _group_rows(v3, j)
    return o3.reshape(n * C, LANES), states[-1]


def _hgrn_kernel(q_ref, z_ref, v_ref, g_ref, lb_ref, hgn_ref, msk_ref, o_ref, st_ref, s_scr):
    i = pl.program_id(1)

    @pl.when(i == 0)
    def _():
        s_scr[...] = jnp.zeros(s_scr.shape, F32)

    o, s_t = _hgrn_tile(q_ref[...], z_ref[...], v_ref[...], _lower_bound(lb_ref), msk_ref[...], s_scr[...])
    s_scr[...] = s_t
    g = g_ref[...]
    o_ref[...] = (_rms(o, hgn_ref[...]) * (g * jax.nn.sigmoid(g))).astype(BF16)

    @pl.when(i == pl.num_programs(1) - 1)
    def _():
        st_ref[...] = s_scr[...].T


def _hgrn_prompt(hg, hgrn_lb, hgrn_norm, tt):
    S = hg.shape[0]
    msk = _hgrn_masks()
    col = lambda seg: pl.BlockSpec((tt, LANES), lambda h, i, seg=seg: (i, seg * HG_HEADS + h))
    o, st = pl.pallas_call(
        _hgrn_kernel, grid=(HG_HEADS, S // tt),
        in_specs=[col(0), col(1), col(2), col(3),
                  pl.BlockSpec((hgrn_lb.shape[0], LANES), lambda h, i: (0, h)),
                  _small_spec((1, HG_VAL)), _small_spec(tuple(msk.shape))],
        out_specs=[pl.BlockSpec((tt, LANES), lambda h, i: (i, h)),
                   pl.BlockSpec((None, HG_KEY, HG_VAL), lambda h, i: (h, 0, 0))],
        out_shape=[jax.ShapeDtypeStruct((S, HG_WIDTH), BF16),
                   jax.ShapeDtypeStruct((HG_HEADS, HG_KEY, HG_VAL), F32)],
        scratch_shapes=[pltpu.VMEM((HG_VAL, HG_KEY), F32)],
        compiler_params=_params(("parallel", "arbitrary")), name="hgrn_prompt",
    )(hg, hg, hg, hg, hgrn_lb, hgrn_norm, msk)
    return o, st


def _hgrn_decode_kernel(hg_ref, s_ref, lb_ref, hgn_ref, o_ref, sn_ref):
    W = HG_WIDTH
    row = hg_ref[...]
    q, z, v, g = (row[:, s * W:(s + 1) * W] for s in range(4))
    lb = _lower_bound(lb_ref)
    f = jnp.exp(jnp.log(lb + (1.0 - lb) * jax.nn.sigmoid(z)))
    kk = (1.0 - lb) * jax.nn.sigmoid(-z)
    rnd = lambda x: x.astype(BF16).astype(F32)
    qf = rnd(q * f)
    qk = q * kk
    pad = jnp.zeros((SUBLANES - 3, LANES), F32)
    outs = []
    for h in range(HG_HEADS):
        hs = slice(h * LANES, (h + 1) * LANES)
        cols = jnp.concatenate([f[:, hs], kk[:, hs], qf[:, hs], pad], axis=0).T
        f_c, k_c, qf_c = cols[:, 0:1], cols[:, 1:2], cols[:, 2:3]
        s0 = s_ref[h]
        vh = v[:, hs]
        sn_ref[h] = f_c * s0 + k_c * vh
        o = jnp.sum(qf_c * rnd(s0), axis=0, keepdims=True) + jnp.sum(qk[:, hs], axis=1, keepdims=True) * vh
        gh = g[:, hs]
        outs.append(_rms(o, hgn_ref[...]) * (gh * jax.nn.sigmoid(gh)))
    o_ref[...] = jnp.concatenate(outs, axis=1).astype(BF16)


def _hgrn_decode(hg, state, hgrn_lb, hgrn_norm):
    Bd = hg.shape[0]
    o, sn = pl.pallas_call(
        _hgrn_decode_kernel, grid=(Bd,),
        in_specs=[pl.BlockSpec((None, 1, 4 * HG_WIDTH), lambda b: (b, 0, 0)),
                  pl.BlockSpec((None, HG_HEADS, HG_KEY, HG_VAL), lambda b: (b, 0, 0, 0)),
                  _small_spec(tuple(hgrn_lb.shape)), _small_spec((1, HG_VAL))],
        out_specs=[pl.BlockSpec((None, 1, HG_WIDTH), lambda b: (b, 0, 0)),
                   pl.BlockSpec((None, HG_HEADS, HG_KEY, HG_VAL), lambda b: (b, 0, 0, 0))],
        out_shape=[jax.ShapeDtypeStruct((Bd, 1, HG_WIDTH), BF16),
                   jax.ShapeDtypeStruct(state.shape, F32)],
        compiler_params=_params(("parallel",)), name="hgrn_decode",
    )(hg.reshape(Bd, 1, -1), state, hgrn_lb, hgrn_norm)
    return o.reshape(Bd, HG_WIDTH), sn


def _merge_kernel(oa_ref, oh_ref, wa_ref, wb_ref, ga_ref, gb_ref, o_ref):
    a = jnp.dot(oa_ref[...], wa_ref[...].astype(BF16), preferred_element_type=F32)
    b = jnp.dot(oh_ref[...], wb_ref[...].astype(BF16), preferred_element_type=F32)
    ga = jax.nn.sigmoid(ga_ref[...].astype(F32))
    gb = jax.nn.sigmoid(gb_ref[...].astype(F32))
    o_ref[...] = (ga * a + gb * b).astype(BF16)


def _merge(oa, oh, w_pa, w_pb, gates, tm, tn, tag):
    M = oa.shape[0]
    nj = D_MODEL // tn
    return pl.pallas_call(
        _merge_kernel, grid=(M // tm, nj),
        in_specs=[pl.BlockSpec((tm, ATT_WIDTH), lambda i, j: (i, 0)),
                  pl.BlockSpec((tm, HG_WIDTH), lambda i, j: (i, 0)),
                  pl.BlockSpec((None, ATT_WIDTH, tn), lambda i, j: (0, 0, j)),
                  pl.BlockSpec((None, HG_WIDTH, tn), lambda i, j: (0, 0, j)),
                  pl.BlockSpec((tm, tn), lambda i, j: (i, j)),
                  pl.BlockSpec((tm, tn), lambda i, j: (i, j + nj))],
        out_specs=pl.BlockSpec((tm, tn), lambda i, j: (i, j)),
        out_shape=jax.ShapeDtypeStruct((M, D_MODEL), BF16),
        compiler_params=_params(("parallel", "arbitrary")), name=f"merge_{tag}",
    )(oa, oh, w_pa, w_pb, gates, gates)


def _cross_prompt_kernel(x_ref, g_ref, wq_ref, mk_ref, mv_ref, wo_ref, o_ref):
    x = x_ref[...]
    u = _rms(x, g_ref[...]).astype(BF16)
    q = jnp.dot(u, wq_ref[...], preferred_element_type=F32).astype(BF16)
    heads = []
    for h in range(X_HEADS):
        hs = slice(h * X_HEAD_DIM, (h + 1) * X_HEAD_DIM)
        s = lax.dot_general(q[:, hs], mk_ref[:, hs], _NT, preferred_element_type=F32) * (X_HEAD_DIM ** -0.5)
        p = jnp.exp(s - jnp.max(s, axis=1, keepdims=True))
        p = p / jnp.sum(p, axis=1, keepdims=True)
        heads.append(jnp.dot(p.astype(BF16), mv_ref[:, hs], preferred_element_type=F32))
    o = jnp.concatenate(heads, axis=1).astype(BF16)
    o_ref[...] = x + jnp.dot(o, wo_ref[...], preferred_element_type=F32)


def _cross_prompt(h1, g_cross, w_cq_bf, mk_bf, mv_bf, w_co_bf, tm):
    M = h1.shape[0]
    return pl.pallas_call(
        _cross_prompt_kernel, grid=(M // tm,),
        in_specs=[pl.BlockSpec((tm, D_MODEL), lambda i: (i, 0)), _small_spec((1, D_MODEL)),
                  _small_spec((D_MODEL, X_WIDTH)), _small_spec(tuple(mk_bf.shape)),
                  _small_spec(tuple(mv_bf.shape)), _small_spec((X_WIDTH, D_MODEL))],
        out_specs=pl.BlockSpec((tm, D_MODEL), lambda i: (i, 0)),
        out_shape=jax.ShapeDtypeStruct((M, D_MODEL), F32),
        compiler_params=_params(("parallel",)), name="cross_prompt",
    )(h1, g_cross.reshape(1, -1), w_cq_bf, mk_bf, mv_bf, w_co_bf)


def _cross_decode_kernel(q_ref, mk_ref, mv_ref, o_ref):
    rnd = lambda x: x.astype(BF16).astype(F32)
    q = rnd(q_ref[...])
    mem = mk_ref.shape[0] // X_HEADS
    outs = []
    for h in range(X_HEADS):
        hs = slice(h * X_HEAD_DIM, (h + 1) * X_HEAD_DIM)
        rows = pl.ds(h, mem, stride=X_HEADS)
        s = jnp.sum(rnd(mk_ref[rows, :]) * q[:, hs], axis=1, keepdims=True) * (X_HEAD_DIM ** -0.5)
        p = jnp.exp(s - jnp.max(s, axis=0, keepdims=True))
        p = p / jnp.sum(p, axis=0, keepdims=True)
        outs.append(jnp.sum(rnd(p) * rnd(mv_ref[rows, :]), axis=0, keepdims=True))
    o_ref[...] = jnp.concatenate(outs, axis=1).astype(BF16)


def _cross_decode(q, mem_k, mem_v):
    Bd, mem = mem_k.shape[0], mem_k.shape[1]
    row = pl.BlockSpec((None, 1, X_WIDTH), lambda b: (b, 0, 0))
    mem_spec = pl.BlockSpec((None, mem * X_HEADS, X_HEAD_DIM), lambda b: (b, 0, 0))
    flat = lambda m: m.reshape(Bd, mem * X_HEADS, X_HEAD_DIM)
    out = pl.pallas_call(
        _cross_decode_kernel, grid=(Bd,), in_specs=[row, mem_spec, mem_spec], out_specs=row,
        out_shape=jax.ShapeDtypeStruct((Bd, 1, X_WIDTH), BF16),
        compiler_params=_params(("parallel",)), name="cross_decode",
    )(q.reshape(Bd, 1, X_WIDTH), flat(mem_k), flat(mem_v))
    return out.reshape(Bd, X_WIDTH)


def _route_kernel(x_ref, g_ref, w_ref, b_ref, cin_ref, uin_ref, u_ref, eid_ref, wt_ref, rank_ref, cnt_ref,
                  c_scr, *, n_rows):
    i = pl.program_id(0)
    tm = x_ref.shape[0]

    @pl.when(i == 0)
    def _():
        c_scr[...] = cin_ref[...]

    u = _rms(x_ref[...], g_ref[...])
    u_ref[...] = u
    dot = lambda a, b: jnp.dot(a, b, preferred_element_type=F32)
    logits = dot(u.astype(BF16), w_ref[...]) + b_ref[...]

    lane = lax.broadcasted_iota(I32, (tm, LANES), 1)
    big = jnp.int32(1 << 20)
    first_max = lambda vals, vmax: jnp.min(jnp.where(vals == vmax, lane, big), axis=1, keepdims=True)
    glv = jnp.where(lane < N_GROUPS, logits, -jnp.inf)
    gmax = jnp.max(glv, axis=1, keepdims=True)
    gsel = first_max(glv, gmax)
    p_group = 1.0 / jnp.sum(jnp.exp(glv - gmax), axis=1, keepdims=True)
    e_lane = lane - N_GROUPS
    in_group = (e_lane >= 0) & (e_lane < N_EXPERTS) & (e_lane // EXPERTS_PER_GROUP == gsel)
    ev = jnp.where(in_group, logits, -jnp.inf)
    v1 = jnp.max(ev, axis=1, keepdims=True)
    i1 = first_max(ev, v1)
    ev2 = jnp.where(lane == i1, -jnp.inf, ev)
    v2 = jnp.max(ev2, axis=1, keepdims=True)
    i2 = first_max(ev2, v2)
    t2 = jnp.exp(v2 - v1)
    w1 = p_group / (1.0 + t2)
    w2 = w1 * t2
    e1 = i1 - N_GROUPS
    e2 = i2 - N_GROUPS
    eid_ref[...] = jnp.where(lane == 0, e1, jnp.where(lane == 1, e2, 0))
    wt_ref[...] = jnp.where(lane == 0, w1, jnp.where(lane == 1, w2, 0.0))

    valid = (lax.broadcasted_iota(I32, (tm, 1), 0) + i * tm) < n_rows
    oh1 = jnp.where((lane == e1) & valid, 1.0, 0.0)
    oh2 = jnp.where((lane == e2) & valid, 1.0, 0.0)
    cnt = (oh1 + oh2).astype(BF16)
    r_i = lax.broadcasted_iota(I32, (tm, tm), 0)
    c_i = lax.broadcasted_iota(I32, (tm, tm), 1)
    strict = jnp.where(c_i < r_i, 1.0, 0.0).astype(BF16)
    before = dot(strict, cnt) + c_scr[...]
    r1 = jnp.sum(oh1 * before, axis=1, keepdims=True)
    r2 = jnp.sum(oh2 * before, axis=1, keepdims=True)
    rank_ref[...] = jnp.where(lane == 0, r1, jnp.where(lane == 1, r2, 0.0)).astype(I32)
    c_scr[...] = c_scr[...] + jnp.sum(oh1 + oh2, axis=0, keepdims=True)
    cnt_ref[...] = c_scr[...]


def _route(x, g_ffn, w_router, b_router, counts_in, u_all, row_off, tm, tag):
    M = x.shape[0]
    T = u_all.shape[0]
    boff = row_off // tm
    kern = functools.partial(_route_kernel, n_rows=M)
    lane_out = lambda dt: jax.ShapeDtypeStruct((M, LANES), dt)
    outs = pl.pallas_call(
        kern, grid=(M // tm,),
        in_specs=[pl.BlockSpec((tm, D_MODEL), lambda i: (i, 0)), _small_spec((1, D_MODEL)),
                  _small_spec((D_MODEL, LANES)), _small_spec((1, LANES)), _small_spec((1, LANES)),
                  pl.BlockSpec(memory_space=pl.ANY)],
        out_specs=[pl.BlockSpec((tm, D_MODEL), lambda i: (i + boff, 0)),
                   pl.BlockSpec((tm, LANES), lambda i: (i, 0)), pl.BlockSpec((tm, LANES), lambda i: (i, 0)),
                   pl.BlockSpec((tm, LANES), lambda i: (i, 0)), _small_spec((1, LANES))],
        out_shape=[jax.ShapeDtypeStruct((T, D_MODEL), F32), lane_out(I32), lane_out(F32), lane_out(I32),
                   jax.ShapeDtypeStruct((1, LANES), F32)],
        scratch_shapes=[pltpu.VMEM((1, LANES), F32)],
        input_output_aliases={5: 0},
        compiler_params=_params(("arbitrary",)), name=f"moe_route_{tag}",
    )(x, g_ffn.reshape(1, -1), w_router, b_router, counts_in, u_all)
    return outs


def _plan_kernel(cnt_ref, pstart_ref, blkexp_ref, blkslot_ref, nxtexp_ref, nblk_ref, *, n_blocks):
    def per_expert(e, carry):
        acc, k = carry
        pstart_ref[e] = acc
        nb = (cnt_ref[e] + MOE_ROWS - 1) // MOE_ROWS
        first = acc // MOE_ROWS

        def fill(b, c):
            blkexp_ref[b] = e
            blkslot_ref[b] = k % 2
            return c

        lax.fori_loop(first, first + nb, fill, 0)
        return acc + nb * MOE_ROWS, k + jnp.where(nb > 0, 1, 0)

    total, _ = lax.fori_loop(0, N_EXPERTS, per_expert, (jnp.int32(0), jnp.int32(0)))
    used = total // MOE_ROWS
    nblk_ref[0] = used
    last = jnp.maximum(used - 1, 0)

    def backwards(t, carry):
        cur, nxt = carry
        b = last - t
        e = blkexp_ref[b]
        nxt = jnp.where(e != cur, cur, nxt)
        nxtexp_ref[b] = nxt
        return e, nxt

    lax.fori_loop(0, used, backwards, (blkexp_ref[last], jnp.int32(-1)))

    def tail(b, c):
        blkexp_ref[b] = blkexp_ref[last]
        blkslot_ref[b] = blkslot_ref[last]
        nxtexp_ref[b] = -1
        return c

    lax.fori_loop(used, n_blocks, tail, 0)


def _plan(counts, n_blocks):
    smem = pl.BlockSpec(memory_space=pltpu.SMEM)
    per_block = jax.ShapeDtypeStruct((n_blocks,), I32)
    return pl.pallas_call(
        functools.partial(_plan_kernel, n_blocks=n_blocks),
        in_specs=[smem], out_specs=[smem] * 5,
        out_shape=[jax.ShapeDtypeStruct((N_EXPERTS,), I32), per_block, per_block, per_block,
                   jax.ShapeDtypeStruct((1,), I32)],
        name="moe_plan",
    )(counts)


def _dest_kernel(pstart_ref, eid_ref, rank_ref, dest_ref):
    e = eid_ref[...]
    start = jnp.zeros(e.shape, I32)
    for x in range(N_EXPERTS):
        start = jnp.where(e == x, pstart_ref[x], start)
    dest_ref[...] = start + rank_ref[...]


def _dest_rows(pstart, eid, rank, tm):
    M = eid.shape[0]
    tm = math.gcd(M, tm)
    spec = pl.BlockSpec((tm, LANES), lambda i, ps: (i, 0))
    grid_spec = pltpu.PrefetchScalarGridSpec(num_scalar_prefetch=1, grid=(M // tm,), in_specs=[spec, spec],
                                             out_specs=spec)
    return pl.pallas_call(_dest_kernel, grid_spec=grid_spec, out_shape=jax.ShapeDtypeStruct((M, LANES), I32),
                          compiler_params=_params(("parallel",)), name="moe_dest")(pstart, eid, rank)


def _dispatch_kernel(dest_ref, pstart_ref, cnt_ref, nblk_ref, u_ref, xs_hbm, zbuf, sem_z, sem,
                     *, n_tokens, n_blocks):
    R = MOE_ROWS
    i = pl.program_id(0)
    tm = u_ref.shape[0]

    def zero_copy(b):
        return pltpu.make_async_copy(zbuf, xs_hbm.at[pl.ds(pl.multiple_of(b * R, R), R), :], sem_z)

    def partial_block(e):
        c = cnt_ref[e]
        return (c % R) != 0, (pstart_ref[e] + c) // R

    def zero_partial(start):
        def body(e, n):
            has, b = partial_block(e)

            @pl.when(has)
            def _():
                zero_copy(b).start() if start else zero_copy(b).wait()

            return n
        return body

    def zero_unused(start):
        def body(b, n):
            zero_copy(b).start() if start else zero_copy(b).wait()
            return n
        return body

    @pl.when(i == 0)
    def _():
        zbuf[...] = jnp.zeros(zbuf.shape, F32)
        for start in (True, False):
            lax.fori_loop(0, N_EXPERTS, zero_partial(start), 0)
            lax.fori_loop(nblk_ref[0], n_blocks, zero_unused(start), 0)

    def row_copy(r, k):
        d = dest_ref[2 * (i * tm + r) + k]
        return pltpu.make_async_copy(u_ref.at[pl.ds(r, 1), :], xs_hbm.at[pl.ds(d, 1), :], sem)

    def start_row(r, n):
        row_copy(r, 0).start()
        row_copy(r, 1).start()
        return n

    def wait_row(r, n):
        row_copy(r, 0).wait()
        row_copy(r, 1).wait()
        return n

    rem = n_tokens % tm

    @pl.when((i + 1) * tm <= n_tokens)
    def _():
        lax.fori_loop(0, tm, start_row, 0, unroll=8)
        lax.fori_loop(0, tm, wait_row, 0, unroll=8)

    if rem:
        @pl.when((i + 1) * tm > n_tokens)
        def _():
            lax.fori_loop(0, rem, start_row, 0, unroll=8)
            lax.fori_loop(0, rem, wait_row, 0, unroll=8)


def _dispatch(dest, pstart, counts, n_used, u_all, n_blocks, tm):
    T = u_all.shape[0]
    grid_spec = pltpu.PrefetchScalarGridSpec(
        num_scalar_prefetch=4, grid=(pl.cdiv(T, tm),),
        in_specs=[pl.BlockSpec((tm, D_MODEL), lambda i, *_: (i, 0))],
        out_specs=pl.BlockSpec(memory_space=pl.ANY),
        scratch_shapes=[pltpu.VMEM((MOE_ROWS, D_MODEL), F32), pltpu.SemaphoreType.DMA(()),
                        pltpu.SemaphoreType.DMA(())])
    return pl.pallas_call(
        functools.partial(_dispatch_kernel, n_tokens=T, n_blocks=n_blocks), grid_spec=grid_spec,
        out_shape=jax.ShapeDtypeStruct((n_blocks * MOE_ROWS, D_MODEL), F32),
        compiler_params=_params(("arbitrary",)), name="moe_dispatch",
    )(dest, pstart, counts, n_used, u_all)


def _expert_kernel(blkexp_ref, blkslot_ref, nxtexp_ref, nblk_ref, x_ref, wg_hbm, wu_hbm, wd_hbm, o_ref,
                   wg_f, wu_f, wd_f, wg_s, wu_s, wd_s, sem):
    i = pl.program_id(0)

    def fetch(e, slot):
        return [pltpu.make_async_copy(src.at[0, e], dst.at[slot], sem.at[slot])
                for src, dst in ((wg_hbm, wg_f), (wu_hbm, wu_f), (wd_hbm, wd_f))]

    @pl.when(i < nblk_ref[0])
    def _():
        e = blkexp_ref[i]
        slot = blkslot_ref[i]
        nxt = nxtexp_ref[i]

        @pl.when(i == 0)
        def _():
            for cp in fetch(e, slot):
                cp.start()

        @pl.when((i == 0) | (e != blkexp_ref[jnp.maximum(i - 1, 0)]))
        def _():
            for cp in fetch(e, slot):
                cp.wait()

            @pl.when(nxt >= 0)
            def _():
                for cp in fetch(nxt, 1 - slot):
                    cp.start()

            wg_s[...] = wg_f[slot].astype(BF16)
            wu_s[...] = wu_f[slot].astype(BF16)
            wd_s[...] = wd_f[slot].astype(BF16)

        x = x_ref[...].astype(BF16)
        hg = jnp.dot(x, wg_s[...], preferred_element_type=F32)
        hu = jnp.dot(x, wu_s[...], preferred_element_type=F32)
        h = (hg * jax.nn.sigmoid(hg) * hu).astype(BF16)
        o_ref[...] = jnp.dot(h, wd_s[...], preferred_element_type=F32)

    @pl.when(i >= nblk_ref[0])
    def _():
        o_ref[...] = jnp.zeros(o_ref.shape, F32)


def _experts(xs, blk_exp, blk_slot, nxt_exp, n_used, we_g, we_u, we_d, n_blocks):
    hbm = pl.BlockSpec(memory_space=pl.ANY)
    up, down = (D_MODEL, EXPERT_FF), (EXPERT_FF, D_MODEL)
    grid_spec = pltpu.PrefetchScalarGridSpec(
        num_scalar_prefetch=4, grid=(n_blocks,),
        in_specs=[pl.BlockSpec((MOE_ROWS, D_MODEL),
                               lambda i, be, bs, nx, nb: (jnp.maximum(jnp.minimum(i, nb[0] - 1), 0), 0)),
                  hbm, hbm, hbm],
        out_specs=pl.BlockSpec((MOE_ROWS, D_MODEL), lambda i, *_: (i, 0)),
        scratch_shapes=[pltpu.VMEM((2,) + up, F32), pltpu.VMEM((2,) + up, F32), pltpu.VMEM((2,) + down, F32),
                        pltpu.VMEM(up, BF16), pltpu.VMEM(up, BF16), pltpu.VMEM(down, BF16),
                        pltpu.SemaphoreType.DMA((2,))])
    return pl.pallas_call(
        _expert_kernel, grid_spec=grid_spec,
        out_shape=jax.ShapeDtypeStruct((n_blocks * MOE_ROWS, D_MODEL), F32),
        compiler_params=_params(("arbitrary",)), name="moe_experts",
    )(blk_exp, blk_slot, nxt_exp, n_used, xs, we_g, we_u, we_d)


def _combine_kernel(dest_ref, h_ref, wt_ref, gf_ref, yb_hbm, o_ref, ybuf0, ybuf1, sem, *, dest_off):
    i = pl.program_id(0)
    tm = h_ref.shape[0]
    base = dest_off + i * (2 * tm)

    def row_copy(r, k, buf):
        d = dest_ref[base + 2 * r + k]
        return pltpu.make_async_copy(yb_hbm.at[pl.ds(d, 1), :], buf.at[pl.ds(r, 1), :], sem)

    def start(r, c):
        row_copy(r, 0, ybuf0).start()
        row_copy(r, 1, ybuf1).start()
        return c

    def wait(r, c):
        row_copy(r, 0, ybuf0).wait()
        row_copy(r, 1, ybuf1).wait()
        return c

    lax.fori_loop(0, tm, start, 0, unroll=4)
    lax.fori_loop(0, tm, wait, 0, unroll=4)
    wt = wt_ref[...]
    h = h_ref[...] + (wt[:, 0:1] * ybuf0[...] + wt[:, 1:2] * ybuf1[...])
    o_ref[...] = _rms(h, gf_ref[...])


def _combine(h2, wts, dest, yb, norm_final, dest_off, tm, tag):
    M = h2.shape[0]
    grid_spec = pltpu.PrefetchScalarGridSpec(
        num_scalar_prefetch=1, grid=(M // tm,),
        in_specs=[pl.BlockSpec((tm, D_MODEL), lambda i, *_: (i, 0)), pl.BlockSpec((tm, LANES), lambda i, *_: (i, 0)),
                  pl.BlockSpec((1, D_MODEL), lambda i, *_: (0, 0)), pl.BlockSpec(memory_space=pl.ANY)],
        out_specs=pl.BlockSpec((tm, D_MODEL), lambda i, *_: (i, 0)),
        scratch_shapes=[pltpu.VMEM((tm, D_MODEL), F32), pltpu.VMEM((tm, D_MODEL), F32),
                        pltpu.SemaphoreType.DMA(())])
    return pl.pallas_call(
        functools.partial(_combine_kernel, dest_off=dest_off), grid_spec=grid_spec,
        out_shape=jax.ShapeDtypeStruct((M, D_MODEL), F32),
        compiler_params=_params(("arbitrary",)), name=f"moe_combine_{tag}",
    )(dest, h2, wts, norm_final.reshape(1, -1), yb)


def _moe_and_final_norm(h2_p, h2_s, g_ffn, wr_g, br_g, wr_e, br_e, we_g, we_u, we_d, norm_final, tm_p):
    Tp, Ts = h2_p.shape[0], h2_s.shape[0]
    T = Tp + Ts
    pad = LANES - N_GROUPS - N_EXPERTS
    w_router = jnp.concatenate([wr_g, wr_e, jnp.zeros((D_MODEL, pad), F32)], axis=1).astype(BF16)
    b_router = jnp.concatenate([br_g, br_e, jnp.zeros((pad,), F32)]).reshape(1, LANES)
    u_all = jnp.zeros((T, D_MODEL), F32)
    zero_counts = jnp.zeros((1, LANES), F32)
    u_all, eid_p, wt_p, rank_p, counts = _route(h2_p, g_ffn, w_router, b_router, zero_counts, u_all, 0, tm_p, "p")
    u_all, eid_s, wt_s, rank_s, counts = _route(h2_s, g_ffn, w_router, b_router, counts, u_all, Tp, Ts, "s")
    A = 2 * T
    n_blocks = (A + N_EXPERTS * (MOE_ROWS - 1)) // MOE_ROWS + 1
    cnt = counts[0, :N_EXPERTS].astype(I32)
    pstart, blk_exp, blk_slot, nxt_exp, n_used = _plan(cnt, n_blocks)
    dest = jnp.concatenate([_dest_rows(pstart, eid_p, rank_p, 1024)[:, :2].reshape(-1),
                            _dest_rows(pstart, eid_s, rank_s, Ts)[:, :2].reshape(-1)])
    xs = _dispatch(dest, pstart, cnt, n_used, u_all, n_blocks, 256)
    yb = _experts(xs, blk_exp, blk_slot, nxt_exp, n_used, we_g, we_u, we_d, n_blocks)
    y_p = _combine(h2_p, wt_p, dest, yb, norm_final, 0, 256, "p")
    y_s = _combine(h2_s, wt_s, dest, yb, norm_final, 2 * Tp, Ts, "s")
    return y_p, y_s


def kernel(x_prompt, x_sample, mem_prompt, cache_k, cache_v, cache_mem_k, cache_mem_v, state_hgrn, page_table,
           norm_mix, w_in, lambda_q1, lambda_k1, lambda_q2, lambda_k2, subln, hgrn_lb, hgrn_norm, w_pa, w_pb,
           w_out, norm_cross, w_cq, w_ck, w_cv, w_co, norm_ffn, w_router_group, b_router_group,
           w_router_expert, b_router_expert, w_e_gate, w_e_up, w_e_down, norm_final):
    assert w_in.shape[0] == 1, "single-layer step"
    Bp, S, D = x_prompt.shape
    Bd, Ld, _ = x_sample.shape
    assert Bp == 1 and Ld == 1
    n_pages = page_table.shape[1]
    page = cache_k.shape[2]
    past_len = n_pages * page
    xp = x_prompt.reshape(S, D)
    xs = x_sample.reshape(Bd, D)
    lam = (lambda_q1, lambda_k1, lambda_q2, lambda_k2)
    TM = 1024
    w_in = w_in.astype(BF16)

    q, k, k_bf, v, v_bf, hg, gates = _in_projection(xp, norm_mix[0], w_in, jnp.arange(S, dtype=I32), TM, 512, "p",
                                                    BF16, ATT_HEAD_DIM ** -0.5 * math.log2(math.e))
    oa = _attention_prompt(q, k_bf, v_bf, *lam, subln, ATTN_TQ, ATTN_TK)
    oh, st_p = _hgrn_prompt(hg, hgrn_lb, hgrn_norm, 1024)
    merged = _merge(oa, oh, w_pa, w_pb, gates, TM, 512, "p")
    (h1_p,) = _mm(merged, w_out, col_off=0, n_cols=D, tm=TM, tn=512, epilogue=_epi_residual, out_dtypes=[F32],
                  extras=[(xp, pl.BlockSpec((TM, 512), lambda i, j: (i, j)))], name="outproj_p")

    pos_s = jnp.full((Bd,), past_len, I32)
    qs, ks, ks_bf, vs, vs_bf, hgs, gates_s = _in_projection(xs, norm_mix[0], w_in, pos_s, Bd, 512, "s", F32,
                                                            ATT_HEAD_DIM ** -0.5)
    ck = jnp.transpose(cache_k[0], (0, 2, 3, 4, 1)).reshape(cache_k.shape[1], ATT_WIDTH, page)
    cv = cache_v[0].reshape(cache_v.shape[1], page * ATT_HEADS, 2 * ATT_HEAD_DIM)
    oa_s = _attention_decode(qs, ks_bf, vs_bf, ck, cv, page_table, *lam, subln)
    oh_s, st_s = _hgrn_decode(hgs, state_hgrn[0], hgrn_lb, hgrn_norm)
    merged_s = _merge(oa_s, oh_s, w_pa, w_pb, gates_s, Bd, 512, "s")
    (h1_s,) = _mm(merged_s, w_out, col_off=0, n_cols=D, tm=Bd, tn=512, epilogue=_epi_residual, out_dtypes=[F32],
                  extras=[(xs, pl.BlockSpec((Bd, 512), lambda i, j: (i, j)))], name="outproj_s")

    mem = mem_prompt.reshape(-1, D)
    mk, mk_bf = _mm(mem, w_ck, col_off=0, n_cols=X_WIDTH, tm=mem.shape[0], tn=X_WIDTH, epilogue=_epi_v,
                    out_dtypes=[F32, BF16], name="mem_k")
    mv, mv_bf = _mm(mem, w_cv, col_off=0, n_cols=X_WIDTH, tm=mem.shape[0], tn=X_WIDTH, epilogue=_epi_v,
                    out_dtypes=[F32, BF16], name="mem_v")
    w_cq_bf = w_cq[0].astype(BF16)
    w_co_bf = w_co[0].astype(BF16)
    h2_p = _cross_prompt(h1_p, norm_cross[0], w_cq_bf, mk_bf, mv_bf, w_co_bf, 256)
    (qc_s,) = _mm(h1_s, w_cq, col_off=0, n_cols=X_WIDTH, tm=Bd, tn=X_WIDTH, epilogue=_epi_plain,
                  out_dtypes=[F32], norm_g=norm_cross[0], name="cross_q_s")
    oc_s = _cross_decode(qc_s, cache_mem_k[0], cache_mem_v[0])
    (h2_s,) = _mm(oc_s, w_co, col_off=0, n_cols=D, tm=Bd, tn=512, epilogue=_epi_residual, out_dtypes=[F32],
                  extras=[(h1_s, pl.BlockSpec((Bd, 512), lambda i, j: (i, j)))], name="cross_o_s")

    y_p, y_s = _moe_and_final_norm(h2_p, h2_s, norm_ffn[0], w_router_group[0], b_router_group[0],
                                   w_router_expert[0], b_router_expert[0], w_e_gate, w_e_up, w_e_down,
                                   norm_final, 256)

    return (y_p.reshape(Bp, S, D), y_s.reshape(Bd, Ld, D),
            k.reshape(1, Bp, S, ATT_HEADS, 2, ATT_HEAD_DIM), v.reshape(1, Bp, S, ATT_HEADS, 2 * ATT_HEAD_DIM),
            ks.reshape(1, Bd, Ld, ATT_HEADS, 2, ATT_HEAD_DIM), vs.reshape(1, Bd, Ld, ATT_HEADS, 2 * ATT_HEAD_DIM),
            st_p.reshape(1, Bp, HG_HEADS, HG_KEY, HG_VAL), st_s.reshape(1, Bd, HG_HEADS, HG_KEY, HG_VAL),
            mk.reshape(1, Bp, -1, X_HEADS, X_HEAD_DIM), mv.reshape(1, Bp, -1, X_HEADS, X_HEAD_DIM))
```

```python
import functools
import math

import numpy as np
import jax
import jax.numpy as jnp
from jax import lax
from jax.experimental import pallas as pl
from jax.experimental.pallas import tpu as pltpu

F32 = jnp.float32
BF16 = jnp.bfloat16
I32 = jnp.int32

D_MODEL = 2048
ATT_HEADS = 8
ATT_HEAD_DIM = 64
ATT_WIDTH = ATT_HEADS * 2 * ATT_HEAD_DIM
ROPE_THETA = 10000.0
HG_HEADS = 8
HG_KEY = 128
HG_VAL = 128
HG_WIDTH = HG_HEADS * HG_VAL
X_HEADS = 4
X_HEAD_DIM = 128
X_WIDTH = X_HEADS * X_HEAD_DIM
N_GROUPS = 4
EXPERTS_PER_GROUP = 8
N_EXPERTS = N_GROUPS * EXPERTS_PER_GROUP
EXPERT_FF = 512
RMS_EPS = 1e-6
NEG_INF = -1e30
LAMBDA_INIT = 0.8 - 0.6 * math.exp(-0.3 * 0)

LANES = 128
SUBLANES = 8
VMEM_LIMIT = 52 * 1024 * 1024

HG_CHUNK = 64
HG_LEVELS = (16, 32, 64)
MOE_ROWS = 256
PAGES_PER_STEP = 16
ATTN_TQ = 1024
ATTN_TK = 1024

_NT = (((1,), (1,)), ((), ()))
_TN = (((0,), (0,)), ((), ()))


def _params(sem):
    return pltpu.CompilerParams(dimension_semantics=sem, vmem_limit_bytes=VMEM_LIMIT)


def _rms(x, g):
    return x * lax.rsqrt(jnp.mean(x * x, axis=-1, keepdims=True) + RMS_EPS) * g


def _mm_kernel(*refs, n_extra, n_out, norm, epilogue):
    x_ref = refs[0]
    pos = 1
    if norm:
        g_ref = refs[1]
        pos = 2
    w_ref = refs[pos]
    extras = refs[pos + 1:pos + 1 + n_extra]
    outs = refs[pos + 1 + n_extra:pos + 1 + n_extra + n_out]
    if norm:
        u_ref = refs[-1]

        @pl.when(pl.program_id(1) == 0)
        def _():
            u_ref[...] = _rms(x_ref[...].astype(F32), g_ref[...]).astype(BF16)

        u = u_ref[...]
    else:
        u = x_ref[...].astype(BF16)
    acc = jnp.dot(u, w_ref[...].astype(BF16), preferred_element_type=F32)
    epilogue(acc, extras, outs)


def _mm(x, w, *, col_off, n_cols, tm, tn, epilogue, out_dtypes, norm_g=None, extras=(), name):
    M, K = x.shape
    assert M % tm == 0 and n_cols % tn == 0 and col_off % tn == 0
    joff = col_off // tn
    if w.ndim == 3:
        w_spec = pl.BlockSpec((None, K, tn), lambda i, j: (0, 0, j + joff))
    else:
        w_spec = pl.BlockSpec((K, tn), lambda i, j: (0, j + joff))
    in_specs = [pl.BlockSpec((tm, K), lambda i, j: (i, 0))]
    args = [x]
    if norm_g is not None:
        in_specs.append(pl.BlockSpec((1, K), lambda i, j: (0, 0)))
        args.append(norm_g.reshape(1, K))
    in_specs.append(w_spec)
    args.append(w)
    for arr, spec in extras:
        in_specs.append(spec)
        args.append(arr)
    out_specs = [pl.BlockSpec((tm, tn), lambda i, j: (i, j)) for _ in out_dtypes]
    out_shape = [jax.ShapeDtypeStruct((M, n_cols), dt) for dt in out_dtypes]
    scratch = [pltpu.VMEM((tm, K), BF16)] if norm_g is not None else []
    kern = functools.partial(_mm_kernel, n_extra=len(extras), n_out=len(out_dtypes),
                             norm=norm_g is not None, epilogue=epilogue)
    res = pl.pallas_call(
        kern, grid=(M // tm, n_cols // tn), in_specs=in_specs, out_specs=out_specs,
        out_shape=out_shape, scratch_shapes=scratch,
        compiler_params=_params(("parallel", "arbitrary")), name=name)(*args)
    return res


def _rope_tile(x, cos, sin_signed):
    first = (lax.broadcasted_iota(I32, (x.shape[0], LANES), 1) % ATT_HEAD_DIM) < ATT_HEAD_DIM // 2
    outs = []
    for c in range(x.shape[1] // LANES):
        xc = x[:, c * LANES:(c + 1) * LANES]
        rot = jnp.where(first, pltpu.roll(xc, LANES - ATT_HEAD_DIM // 2, 1),
                        pltpu.roll(xc, ATT_HEAD_DIM // 2, 1))
        outs.append(xc * cos + rot * sin_signed)
    return outs[0] if len(outs) == 1 else jnp.concatenate(outs, axis=1)


def _epi_q(acc, extras, outs, q_scale):
    cos_ref, sin_ref = extras
    outs[0][...] = (_rope_tile(acc, cos_ref[...], sin_ref[...]) * q_scale).astype(BF16)


def _epi_k(acc, extras, outs):
    cos_ref, sin_ref = extras
    r = _rope_tile(acc, cos_ref[...], sin_ref[...])
    outs[0][...] = r
    outs[1][...] = r.astype(BF16)


def _epi_v(acc, extras, outs):
    outs[0][...] = acc
    outs[1][...] = acc.astype(BF16)


def _epi_plain(acc, extras, outs):
    outs[0][...] = acc.astype(outs[0].dtype)


def _epi_residual(acc, extras, outs):
    outs[0][...] = extras[0][...] + acc


def _rope_tables(pos):
    half = ATT_HEAD_DIM // 2
    freqs = ROPE_THETA ** (-jnp.arange(half, dtype=F32) / half)
    ang = pos.astype(F32)[:, None] * freqs[None, :]
    c, s = jnp.cos(ang), jnp.sin(ang)
    return jnp.tile(c, (1, 4)), jnp.concatenate([-s, s, -s, s], axis=1)


def _inproj_kernel(x_ref, g_ref, w_ref, cos_ref, sin_ref, q_ref, k_ref, kb_ref, v_ref, vb_ref, hg_ref, gt_ref,
                   u_ref, *, bounds, q_scale):
    j = pl.program_id(1)

    @pl.when(j == 0)
    def _():
        u_ref[...] = _rms(x_ref[...].astype(F32), g_ref[...]).astype(BF16)

    acc = jnp.dot(u_ref[...], w_ref[...], preferred_element_type=F32)
    in_seg = lambda s: (j >= bounds[s]) & (j < bounds[s + 1])

    @pl.when(in_seg(0))
    def _():
        _epi_q(acc, (cos_ref, sin_ref), (q_ref,), q_scale)

    @pl.when(in_seg(1))
    def _():
        _epi_k(acc, (cos_ref, sin_ref), (k_ref, kb_ref))

    @pl.when(in_seg(2))
    def _():
        _epi_v(acc, (), (v_ref, vb_ref))

    @pl.when(in_seg(3))
    def _():
        hg_ref[...] = acc

    @pl.when(in_seg(4))
    def _():
        gt_ref[...] = acc.astype(gt_ref.dtype)


def _in_projection(x, g, w_in_bf, pos, tm, tn, tag, gate_dtype, q_scale, row_buffers):
    M, K = x.shape
    cos, sin = _rope_tables(pos)
    widths = (ATT_WIDTH, ATT_WIDTH, ATT_WIDTH, 4 * HG_WIDTH, 2 * D_MODEL)
    bounds = tuple(int(b) for b in np.cumsum((0,) + widths) // tn)

    def seg_spec(s):
        lo, n = bounds[s], bounds[s + 1] - bounds[s]
        return pl.BlockSpec((tm, tn), lambda i, j: (i, jnp.clip(j - lo, 0, n - 1)))

    seg_of_out = (0, 1, 1, 2, 2, 3, 4)
    out_dtypes = (BF16, F32, BF16, F32, BF16, F32, gate_dtype)
    row_tbl = pl.BlockSpec((tm, LANES), lambda i, j: (i, 0))
    return pl.pallas_call(
        functools.partial(_inproj_kernel, bounds=bounds, q_scale=q_scale),
        grid=(M // tm, bounds[-1]),
        in_specs=[pl.BlockSpec((tm, K), lambda i, j: (i, 0), pipeline_mode=pl.Buffered(row_buffers)),
                  pl.BlockSpec((1, K), lambda i, j: (0, 0)),
                  pl.BlockSpec((None, K, tn), lambda i, j: (0, 0, j)), row_tbl, row_tbl],
        out_specs=[seg_spec(s) for s in seg_of_out],
        out_shape=[jax.ShapeDtypeStruct((M, widths[s]), dt) for s, dt in zip(seg_of_out, out_dtypes)],
        scratch_shapes=[pltpu.VMEM((tm, K), BF16)],
        compiler_params=_params(("parallel", "arbitrary")), name=f"inproj_{tag}",
    )(x, g.reshape(1, K), w_in_bf, cos, sin)


def _lambda_value(lq1, lk1, lq2, lk2):
    return (jnp.exp(jnp.sum(lq1[...] * lk1[...], axis=-1, keepdims=True))
            - jnp.exp(jnp.sum(lq2[...] * lk2[...], axis=-1, keepdims=True)) + LAMBDA_INIT)


def _attn_kernel(q_ref, k_ref, v_ref, lq1, lk1, lq2, lk2, subln_ref, o_ref, *, tq, tk):
    i = pl.program_id(1)
    q = q_ref[...]
    lane = lax.broadcasted_iota(I32, q.shape, 1)
    zero = jnp.zeros_like(q)
    qm = (jnp.where(lane < ATT_HEAD_DIM, q, zero), jnp.where(lane >= ATT_HEAD_DIM, q, zero))

    def update(off, carry, mask):
        off = pl.multiple_of(off, tk)
        kt = k_ref[pl.ds(off, tk), :]
        vt = v_ref[pl.ds(off, tk), :]
        new = []
        for m in range(2):
            mx, l, acc = carry[3 * m:3 * m + 3]
            s = lax.dot_general(qm[m], kt, _NT, preferred_element_type=F32)
            if mask is not None:
                s = jnp.where(mask, s, NEG_INF)
            mn = jnp.maximum(mx, jnp.max(s, axis=1, keepdims=True))
            alpha = jnp.exp2(mx - mn)
            p = jnp.exp2(s - mn)
            l = alpha * l + jnp.sum(p, axis=1, keepdims=True)
            acc = alpha * acc + jnp.dot(p.astype(BF16), vt, preferred_element_type=F32)
            new += [mn, l, acc]
        return tuple(new)

    def body(j, carry):
        return update(j * tk, carry, None)

    init = (jnp.full((tq, 1), NEG_INF, F32), jnp.zeros((tq, 1), F32), jnp.zeros((tq, LANES), F32)) * 2
    carry = lax.fori_loop(0, i * (tq // tk), body, init)
    row = lax.broadcasted_iota(I32, (tq, tk), 0)
    col = lax.broadcasted_iota(I32, (tq, tk), 1)
    for d in range(tq // tk):
        carry = update(i * tq + d * tk, carry, col + d * tk <= row)
    _, l0, a0, _, l1, a1 = carry

    lam = _lambda_value(lq1, lk1, lq2, lk2)
    o = a0 / l0 - lam * (a1 / l1)
    o_ref[...] = (_rms(o, subln_ref[...]) * (1.0 - LAMBDA_INIT)).astype(BF16)


def _small_spec(shape):
    nd = len(shape)
    return pl.BlockSpec(shape, lambda *_: (0,) * nd)


def _attention_prompt(q, k_bf, v_bf, lq1, lk1, lq2, lk2, subln, tq, tk):
    S = q.shape[0]
    assert tq % tk == 0
    lam_specs = [_small_spec((1, ATT_HEAD_DIM))] * 4
    return pl.pallas_call(
        functools.partial(_attn_kernel, tq=tq, tk=tk),
        grid=(ATT_HEADS, S // tq),
        in_specs=[pl.BlockSpec((tq, LANES), lambda h, i: (i, h)),
                  pl.BlockSpec((S, LANES), lambda h, i: (0, h)),
                  pl.BlockSpec((S, LANES), lambda h, i: (0, h))] + lam_specs
                 + [_small_spec((1, LANES))],
        out_specs=pl.BlockSpec((tq, LANES), lambda h, i: (i, h)),
        out_shape=jax.ShapeDtypeStruct((S, ATT_WIDTH), BF16),
        compiler_params=_params(("parallel", "arbitrary")), name="attn_prompt",
    )(q, k_bf, v_bf, lq1, lk1, lq2, lk2, subln)


def _decode_attn_kernel(pt_ref, q_ref, kn_ref, vn_ref, lq1, lk1, lq2, lk2, subln_ref, *rest, n_pg):
    k_refs = rest[:n_pg]
    v_refs = rest[n_pg:2 * n_pg]
    o_ref = rest[2 * n_pg]
    qr_ref, s_scr, w_scr, wn_scr, acc_ref = rest[2 * n_pg + 1:]
    phase = pl.program_id(1)
    j = pl.program_id(2)
    last = pl.num_programs(2) - 1
    nrow = 2 * ATT_HEADS

    @pl.when((phase == 0) & (j == 0))
    def _():
        row = lax.broadcasted_iota(I32, (nrow, ATT_WIDTH), 0)
        lane = lax.broadcasted_iota(I32, (nrow, ATT_WIDTH), 1)
        sel = (lane // LANES == row % ATT_HEADS) & ((lane // ATT_HEAD_DIM) % 2 == row // ATT_HEADS)
        qb = jnp.broadcast_to(q_ref[...].astype(F32), (nrow, ATT_WIDTH))
        qr_ref[...] = jnp.where(sel, qb, 0.0).astype(BF16)

    @pl.when(phase == 0)
    def _():
        qr = qr_ref[...]
        s_scr[j] = jnp.concatenate(
            [jnp.dot(qr, kr[...].astype(BF16), preferred_element_type=F32) for kr in k_refs], axis=1)

    @pl.when((phase == 1) & (j == 0))
    def _():
        sn = jnp.sum(qr_ref[...].astype(F32) * kn_ref[...].astype(F32), axis=1, keepdims=True)
        s = s_scr[...]
        m = jnp.maximum(jnp.max(jnp.max(s, axis=2, keepdims=True), axis=0, keepdims=True), sn[None])
        e = jnp.exp(s - m)
        en = jnp.exp(sn[None] - m)
        l = jnp.sum(jnp.sum(e, axis=2, keepdims=True), axis=0, keepdims=True) + en
        p = e / l
        pn = (en / l)[0]
        lam = _lambda_value(lq1, lk1, lq2, lk2)
        w = p[:, :ATT_HEADS, :] - lam * p[:, ATT_HEADS:, :]
        w_scr[...] = jnp.concatenate([w, jnp.zeros_like(w)], axis=1).astype(BF16)
        wn_scr[...] = pn[:ATT_HEADS] - lam * pn[ATT_HEADS:]
        acc_ref[...] = jnp.zeros(acc_ref.shape, F32)

    @pl.when(phase == 1)
    def _():
        wb = w_scr[j]
        page = wb.shape[1] // n_pg
        heads = []
        for h in range(ATT_HEADS):
            vh = jnp.concatenate([vr[pl.ds(h, page, stride=ATT_HEADS), :].astype(BF16) for vr in v_refs], axis=0)
            heads.append(jnp.dot(wb, vh, preferred_element_type=F32))
        acc_ref[...] = acc_ref[...] + jnp.concatenate(heads, axis=1)

    @pl.when((phase == 1) & (j == last))
    def _():
        wn = wn_scr[...].astype(BF16).astype(F32)
        od = acc_ref[...][:ATT_HEADS] + wn * vn_ref[...].astype(F32)
        own = (lax.broadcasted_iota(I32, od.shape, 1) // LANES) == lax.broadcasted_iota(I32, od.shape, 0)
        od = jnp.where(own, od, 0.0)
        ms = jnp.sum(od * od, axis=1, keepdims=True) / (2 * ATT_HEAD_DIM)
        y = od * lax.rsqrt(ms + RMS_EPS) * subln_ref[...] * (1.0 - LAMBDA_INIT)
        o_ref[...] = jnp.sum(y, axis=0, keepdims=True).astype(BF16)


def _attention_decode(q, k_new_bf, v_new_bf, cache_kt, cache_v, page_table, lq1, lk1, lq2, lk2, subln):
    Bd = q.shape[0]
    page = cache_kt.shape[2]
    n_pages = page_table.shape[1]
    n_pg = math.gcd(n_pages, PAGES_PER_STEP)
    assert n_pages % n_pg == 0
    subln_w = jnp.tile(subln, (1, ATT_HEADS))

    n_groups = n_pages // n_pg
    nrow = 2 * ATT_HEADS

    def k_spec(g):
        return pl.BlockSpec((None, ATT_WIDTH, page), lambda b, ph, j, pt: (
            pt[b * n_pages + jnp.where(ph == 0, j, n_groups - 1) * n_pg + g], 0, 0))

    def v_spec(g):
        def index(b, ph, j, pt):
            held = jnp.maximum(b - 1, 0) * n_pages + (n_groups - 1) * n_pg
            return pt[jnp.where(ph == 0, held, b * n_pages + j * n_pg) + g], 0, 0
        return pl.BlockSpec((None, page * ATT_HEADS, 2 * ATT_HEAD_DIM), index)

    row_spec = pl.BlockSpec((None, 1, ATT_WIDTH), lambda b, ph, j, pt: (b, 0, 0))
    small = lambda shape: pl.BlockSpec(shape, lambda b, ph, j, pt: (0,) * len(shape))
    grid_spec = pltpu.PrefetchScalarGridSpec(
        num_scalar_prefetch=1, grid=(Bd, 2, n_groups),
        in_specs=[row_spec, row_spec, row_spec] + [small((1, ATT_HEAD_DIM))] * 4 + [small((1, ATT_WIDTH))]
                 + [k_spec(g) for g in range(n_pg)] + [v_spec(g) for g in range(n_pg)],
        out_specs=row_spec,
        scratch_shapes=[pltpu.VMEM((nrow, ATT_WIDTH), BF16),
                        pltpu.VMEM((n_groups, nrow, n_pg * page), F32),
                        pltpu.VMEM((n_groups, nrow, n_pg * page), BF16),
                        pltpu.VMEM((ATT_HEADS, 1), F32),
                        pltpu.VMEM((nrow, ATT_WIDTH), F32)])
    out = pl.pallas_call(
        functools.partial(_decode_attn_kernel, n_pg=n_pg), grid_spec=grid_spec,
        out_shape=jax.ShapeDtypeStruct((Bd, 1, ATT_WIDTH), BF16),
        compiler_params=_params(("parallel", "arbitrary", "arbitrary")), name="attn_decode",
    )(page_table.reshape(-1), q.reshape(Bd, 1, -1), k_new_bf.reshape(Bd, 1, -1), v_new_bf.reshape(Bd, 1, -1),
      lq1, lk1, lq2, lk2, subln_w, *([cache_kt] * n_pg), *([cache_v] * n_pg))
    return out.reshape(Bd, ATT_WIDTH)


def _hgrn_masks():
    C = HG_CHUNK
    t = np.arange(C)[:, None]
    r = np.arange(C)[None, :]
    blocks = [(r <= t), (r > t)]
    for B in HG_LEVELS:
        mid = (t // B) * B + B // 2 - 1
        second = (t % B) >= B // 2
        blocks.append(np.where(second, (r > mid) & (r <= t), (r > t) & (r <= mid)))
    return jnp.asarray(np.concatenate(blocks, axis=0).astype(np.float32), dtype=BF16)


def _lower_bound(lb_ref):
    a = lb_ref[...].astype(F32)
    e = jnp.exp(a - jnp.max(a, axis=0, keepdims=True))
    return e[0:1] / jnp.sum(e, axis=0, keepdims=True)


def _group_rows(x, j):
    return jnp.broadcast_to(x[:, j:j + 1, :], x.shape)


def _split3(x):
    hi = x.astype(BF16)
    r1 = x - hi.astype(F32)
    mid = r1.astype(BF16)
    lo = (r1 - mid.astype(F32)).astype(BF16)
    return hi, mid, lo


def _hgrn_tile(q, z, v, lb, msk, s_t):
    C = HG_CHUNK
    n = q.shape[0] // C
    dot = lambda a, b: jnp.dot(a, b, preferred_element_type=F32)
    rows = lambda x, c: x[c * C:(c + 1) * C]
    chunks = range(n)

    logf = jnp.log(lb + (1.0 - lb) * jax.nn.sigmoid(z)) * math.log2(math.e)
    kk = (1.0 - lb) * jax.nn.sigmoid(-z)
    v_bf = v.astype(BF16)
    parts = _split3(logf)
    e = [sum(dot(msk, rows(p, c)) for p in parts) for c in chunks]
    b = [ec[0:C] for ec in e]
    qi = [(rows(q, c) * jnp.exp2(b[c])).astype(BF16) for c in chunks]
    kl = [(rows(kk, c) * jnp.exp2(e[c][C:2 * C])).astype(BF16) for c in chunks]
    kv = [lax.dot_general(rows(v_bf, c), kl[c], _TN, preferred_element_type=F32) for c in chunks]

    t_idx = lax.broadcasted_iota(I32, (C, 1), 0)
    row = lax.broadcasted_iota(I32, (C, C), 0)
    col = lax.broadcasted_iota(I32, (C, C), 1)
    a = [jnp.zeros((C, C), F32) for _ in chunks]
    for li, B in enumerate(HG_LEVELS):
        second = (t_idx % B) >= B // 2
        same_block = (row // B) == (col // B)
        for c in chunks:
            x = jnp.exp2(e[c][(2 + li) * C:(3 + li) * C])
            qt = jnp.where(second, rows(q, c) * x, 0.0).astype(BF16)
            kt = jnp.where(second, 0.0, rows(kk, c) * x).astype(BF16)
            al = lax.dot_general(qt, kt, _NT, preferred_element_type=F32)
            a[c] = a[c] + (jnp.where(same_block, al, 0.0) if B < C else al)

    states = [s_t]
    for c in chunks:
        states.append(states[-1] * jnp.exp2(b[c][C - 1:C, :]) + kv[c])
    o = [lax.dot_general(qi[c], states[c].astype(BF16), _NT, preferred_element_type=F32)
         + dot(a[c].astype(BF16), rows(v_bf, c)) for c in chunks]

    G = n * C // SUBLANES
    b_all = jnp.concatenate(b, axis=0) if n > 1 else b[0]
    q3, k3, v3, b3 = (x.reshape(G, SUBLANES, LANES) for x in (q, kk, v, b_all))
    p3 = lax.broadcasted_iota(I32, (G, SUBLANES, 1), 1)
    o3 = (jnp.concatenate(o, axis=0) if n > 1 else o[0]).reshape(G, SUBLANES, LANES)
    for j in range(SUBLANES):
        term = q3 * _group_rows(k3, j) * jnp.exp2(jnp.minimum(b3 - _group_rows(b3, j), 0.0))
        aj = jnp.where(p3 >= j, jnp.sum(term, axis=-1, keepdims=True), 0.0)
        o3 = o3 + aj * _group_rows(v3, j)
    return o3.reshape(n * C, LANES), states[-1]


def _hgrn_kernel(q_ref, z_ref, v_ref, g_ref, lb_ref, hgn_ref, msk_ref, o_ref, st_ref, s_scr):
    i = pl.program_id(1)

    @pl.when(i == 0)
    def _():
        s_scr[...] = jnp.zeros(s_scr.shape, F32)

    o, s_t = _hgrn_tile(q_ref[...], z_ref[...], v_ref[...], _lower_bound(lb_ref), msk_ref[...], s_scr[...])
    s_scr[...] = s_t
    g = g_ref[...]
    o_ref[...] = (_rms(o, hgn_ref[...]) * (g * jax.nn.sigmoid(g))).astype(BF16)

    @pl.when(i == pl.num_programs(1) - 1)
    def _():
        st_ref[...] = s_scr[...].T


def _hgrn_prompt(hg, hgrn_lb, hgrn_norm, tt):
    S = hg.shape[0]
    tt = math.gcd(S, tt)
    msk = _hgrn_masks()
    col = lambda seg: pl.BlockSpec((tt, LANES), lambda h, i, seg=seg: (i, seg * HG_HEADS + h))
    o, st = pl.pallas_call(
        _hgrn_kernel, grid=(HG_HEADS, S // tt),
        in_specs=[col(0), col(1), col(2), col(3),
                  pl.BlockSpec((hgrn_lb.shape[0], LANES), lambda h, i: (0, h)),
                  _small_spec((1, HG_VAL)), _small_spec(tuple(msk.shape))],
        out_specs=[pl.BlockSpec((tt, LANES), lambda h, i: (i, h)),
                   pl.BlockSpec((None, HG_KEY, HG_VAL), lambda h, i: (h, 0, 0))],
        out_shape=[jax.ShapeDtypeStruct((S, HG_WIDTH), BF16),
                   jax.ShapeDtypeStruct((HG_HEADS, HG_KEY, HG_VAL), F32)],
        scratch_shapes=[pltpu.VMEM((HG_VAL, HG_KEY), F32)],
        compiler_params=_params(("parallel", "arbitrary")), name="hgrn_prompt",
    )(hg, hg, hg, hg, hgrn_lb, hgrn_norm, msk)
    return o, st


def _hgrn_decode_kernel(hg_ref, s_ref, lb_ref, hgn_ref, o_ref, sn_ref):
    W = HG_WIDTH
    row = hg_ref[...]
    q, z, v, g = (row[:, s * W:(s + 1) * W] for s in range(4))
    lb = _lower_bound(lb_ref)
    f = jnp.exp(jnp.log(lb + (1.0 - lb) * jax.nn.sigmoid(z)))
    kk = (1.0 - lb) * jax.nn.sigmoid(-z)
    rnd = lambda x: x.astype(BF16).astype(F32)
    qf = rnd(q * f)
    qk = q * kk
    pad = jnp.zeros((SUBLANES - 3, LANES), F32)
    outs = []
    for h in range(HG_HEADS):
        hs = slice(h * LANES, (h + 1) * LANES)
        cols = jnp.concatenate([f[:, hs], kk[:, hs], qf[:, hs], pad], axis=0).T
        f_c, k_c, qf_c = cols[:, 0:1], cols[:, 1:2], cols[:, 2:3]
        s0 = s_ref[h]
        vh = v[:, hs]
        sn_ref[h] = f_c * s0 + k_c * vh
        o = jnp.sum(qf_c * rnd(s0), axis=0, keepdims=True) + jnp.sum(qk[:, hs], axis=1, keepdims=True) * vh
        gh = g[:, hs]
        outs.append(_rms(o, hgn_ref[...]) * (gh * jax.nn.sigmoid(gh)))
    o_ref[...] = jnp.concatenate(outs, axis=1).astype(BF16)


def _hgrn_decode(hg, state, hgrn_lb, hgrn_norm):
    Bd = hg.shape[0]
    o, sn = pl.pallas_call(
        _hgrn_decode_kernel, grid=(Bd,),
        in_specs=[pl.BlockSpec((None, 1, 4 * HG_WIDTH), lambda b: (b, 0, 0)),
                  pl.BlockSpec((None, HG_HEADS, HG_KEY, HG_VAL), lambda b: (b, 0, 0, 0)),
                  _small_spec(tuple(hgrn_lb.shape)), _small_spec((1, HG_VAL))],
        out_specs=[pl.BlockSpec((None, 1, HG_WIDTH), lambda b: (b, 0, 0)),
                   pl.BlockSpec((None, HG_HEADS, HG_KEY, HG_VAL), lambda b: (b, 0, 0, 0))],
        out_shape=[jax.ShapeDtypeStruct((Bd, 1, HG_WIDTH), BF16),
                   jax.ShapeDtypeStruct(state.shape, F32)],
        compiler_params=_params(("parallel",)), name="hgrn_decode",
    )(hg.reshape(Bd, 1, -1), state, hgrn_lb, hgrn_norm)
    return o.reshape(Bd, HG_WIDTH), sn


def _merge_kernel(oa_ref, oh_ref, wa_ref, wb_ref, ga_ref, gb_ref, o_ref):
    a = jnp.dot(oa_ref[...], wa_ref[...].astype(BF16), preferred_element_type=F32)
    b = jnp.dot(oh_ref[...], wb_ref[...].astype(BF16), preferred_element_type=F32)
    ga = jax.nn.sigmoid(ga_ref[...].astype(F32))
    gb = jax.nn.sigmoid(gb_ref[...].astype(F32))
    o_ref[...] = (ga * a + gb * b).astype(BF16)


def _merge(oa, oh, w_pa, w_pb, gates, tm, tn, tag):
    M = oa.shape[0]
    nj = D_MODEL // tn
    return pl.pallas_call(
        _merge_kernel, grid=(M // tm, nj),
        in_specs=[pl.BlockSpec((tm, ATT_WIDTH), lambda i, j: (i, 0)),
                  pl.BlockSpec((tm, HG_WIDTH), lambda i, j: (i, 0)),
                  pl.BlockSpec((None, ATT_WIDTH, tn), lambda i, j: (0, 0, j)),
                  pl.BlockSpec((None, HG_WIDTH, tn), lambda i, j: (0, 0, j)),
                  pl.BlockSpec((tm, tn), lambda i, j: (i, j)),
                  pl.BlockSpec((tm, tn), lambda i, j: (i, j + nj))],
        out_specs=pl.BlockSpec((tm, tn), lambda i, j: (i, j)),
        out_shape=jax.ShapeDtypeStruct((M, D_MODEL), BF16),
        compiler_params=_params(("parallel", "arbitrary")), name=f"merge_{tag}",
    )(oa, oh, w_pa, w_pb, gates, gates)


def _cross_prompt_kernel(x_ref, g_ref, wq_ref, mk_ref, mv_ref, wo_ref, o_ref):
    x = x_ref[...]
    u = _rms(x, g_ref[...]).astype(BF16)
    q = jnp.dot(u, wq_ref[...], preferred_element_type=F32).astype(BF16)
    heads = []
    for h in range(X_HEADS):
        hs = slice(h * X_HEAD_DIM, (h + 1) * X_HEAD_DIM)
        s = lax.dot_general(q[:, hs], mk_ref[:, hs], _NT, preferred_element_type=F32) * (X_HEAD_DIM ** -0.5)
        p = jnp.exp(s - jnp.max(s, axis=1, keepdims=True))
        p = p / jnp.sum(p, axis=1, keepdims=True)
        heads.append(jnp.dot(p.astype(BF16), mv_ref[:, hs], preferred_element_type=F32))
    o = jnp.concatenate(heads, axis=1).astype(BF16)
    o_ref[...] = x + jnp.dot(o, wo_ref[...], preferred_element_type=F32)


def _cross_prompt(h1, g_cross, w_cq_bf, mk_bf, mv_bf, w_co_bf, tm):
    M = h1.shape[0]
    return pl.pallas_call(
        _cross_prompt_kernel, grid=(M // tm,),
        in_specs=[pl.BlockSpec((tm, D_MODEL), lambda i: (i, 0)), _small_spec((1, D_MODEL)),
                  _small_spec((D_MODEL, X_WIDTH)), _small_spec(tuple(mk_bf.shape)),
                  _small_spec(tuple(mv_bf.shape)), _small_spec((X_WIDTH, D_MODEL))],
        out_specs=pl.BlockSpec((tm, D_MODEL), lambda i: (i, 0)),
        out_shape=jax.ShapeDtypeStruct((M, D_MODEL), F32),
        compiler_params=_params(("parallel",)), name="cross_prompt",
    )(h1, g_cross.reshape(1, -1), w_cq_bf, mk_bf, mv_bf, w_co_bf)


def _cross_decode_kernel(q_ref, mk_ref, mv_ref, o_ref):
    rnd = lambda x: x.astype(BF16).astype(F32)
    q = rnd(q_ref[...])
    mem = mk_ref.shape[0] // X_HEADS
    outs = []
    for h in range(X_HEADS):
        hs = slice(h * X_HEAD_DIM, (h + 1) * X_HEAD_DIM)
        rows = pl.ds(h, mem, stride=X_HEADS)
        s = jnp.sum(rnd(mk_ref[rows, :]) * q[:, hs], axis=1, keepdims=True) * (X_HEAD_DIM ** -0.5)
        p = jnp.exp(s - jnp.max(s, axis=0, keepdims=True))
        p = p / jnp.sum(p, axis=0, keepdims=True)
        outs.append(jnp.sum(rnd(p) * rnd(mv_ref[rows, :]), axis=0, keepdims=True))
    o_ref[...] = jnp.concatenate(outs, axis=1).astype(BF16)


def _cross_decode(q, mem_k, mem_v):
    Bd, mem = mem_k.shape[0], mem_k.shape[1]
    row = pl.BlockSpec((None, 1, X_WIDTH), lambda b: (b, 0, 0))
    mem_spec = pl.BlockSpec((None, mem * X_HEADS, X_HEAD_DIM), lambda b: (b, 0, 0))
    flat = lambda m: m.reshape(Bd, mem * X_HEADS, X_HEAD_DIM)
    out = pl.pallas_call(
        _cross_decode_kernel, grid=(Bd,), in_specs=[row, mem_spec, mem_spec], out_specs=row,
        out_shape=jax.ShapeDtypeStruct((Bd, 1, X_WIDTH), BF16),
        compiler_params=_params(("parallel",)), name="cross_decode",
    )(q.reshape(Bd, 1, X_WIDTH), flat(mem_k), flat(mem_v))
    return out.reshape(Bd, X_WIDTH)


def _route_kernel(x_ref, g_ref, w_ref, b_ref, cin_ref, uin_ref, u_ref, eid_ref, wt_ref, rank_ref, cnt_ref,
                  c_scr, *, n_rows):
    i = pl.program_id(0)
    tm = x_ref.shape[0]

    @pl.when(i == 0)
    def _():
        c_scr[...] = cin_ref[...]

    u = _rms(x_ref[...], g_ref[...])
    u_ref[...] = u
    dot = lambda a, b: jnp.dot(a, b, preferred_element_type=F32)
    logits = dot(u.astype(BF16), w_ref[...]) + b_ref[...]

    lane = lax.broadcasted_iota(I32, (tm, LANES), 1)
    big = jnp.int32(1 << 20)
    first_max = lambda vals, vmax: jnp.min(jnp.where(vals == vmax, lane, big), axis=1, keepdims=True)
    glv = jnp.where(lane < N_GROUPS, logits, -jnp.inf)
    gmax = jnp.max(glv, axis=1, keepdims=True)
    gsel = first_max(glv, gmax)
    p_group = 1.0 / jnp.sum(jnp.exp(glv - gmax), axis=1, keepdims=True)
    e_lane = lane - N_GROUPS
    in_group = (e_lane >= 0) & (e_lane < N_EXPERTS) & (e_lane // EXPERTS_PER_GROUP == gsel)
    ev = jnp.where(in_group, logits, -jnp.inf)
    v1 = jnp.max(ev, axis=1, keepdims=True)
    i1 = first_max(ev, v1)
    ev2 = jnp.where(lane == i1, -jnp.inf, ev)
    v2 = jnp.max(ev2, axis=1, keepdims=True)
    i2 = first_max(ev2, v2)
    t2 = jnp.exp(v2 - v1)
    w1 = p_group / (1.0 + t2)
    w2 = w1 * t2
    e1 = i1 - N_GROUPS
    e2 = i2 - N_GROUPS
    eid_ref[...] = jnp.where(lane == 0, e1, jnp.where(lane == 1, e2, 0))
    wt_ref[...] = jnp.where(lane == 0, w1, jnp.where(lane == 1, w2, 0.0))

    valid = (lax.broadcasted_iota(I32, (tm, 1), 0) + i * tm) < n_rows
    oh1 = jnp.where((lane == e1) & valid, 1.0, 0.0)
    oh2 = jnp.where((lane == e2) & valid, 1.0, 0.0)
    cnt = (oh1 + oh2).astype(BF16)
    r_i = lax.broadcasted_iota(I32, (tm, tm), 0)
    c_i = lax.broadcasted_iota(I32, (tm, tm), 1)
    strict = jnp.where(c_i < r_i, 1.0, 0.0).astype(BF16)
    before = dot(strict, cnt) + c_scr[...]
    r1 = jnp.sum(oh1 * before, axis=1, keepdims=True)
    r2 = jnp.sum(oh2 * before, axis=1, keepdims=True)
    rank_ref[...] = jnp.where(lane == 0, r1, jnp.where(lane == 1, r2, 0.0)).astype(I32)
    c_scr[...] = c_scr[...] + jnp.sum(oh1 + oh2, axis=0, keepdims=True)
    cnt_ref[...] = c_scr[...]


def _route(x, g_ffn, w_router, b_router, counts_in, u_all, row_off, tm, tag):
    M = x.shape[0]
    T = u_all.shape[0]
    boff = row_off // tm
    kern = functools.partial(_route_kernel, n_rows=M)
    lane_out = lambda dt: jax.ShapeDtypeStruct((M, LANES), dt)
    outs = pl.pallas_call(
        kern, grid=(M // tm,),
        in_specs=[pl.BlockSpec((tm, D_MODEL), lambda i: (i, 0)), _small_spec((1, D_MODEL)),
                  _small_spec((D_MODEL, LANES)), _small_spec((1, LANES)), _small_spec((1, LANES)),
                  pl.BlockSpec(memory_space=pl.ANY)],
        out_specs=[pl.BlockSpec((tm, D_MODEL), lambda i: (i + boff, 0)),
                   pl.BlockSpec((tm, LANES), lambda i: (i, 0)), pl.BlockSpec((tm, LANES), lambda i: (i, 0)),
                   pl.BlockSpec((tm, LANES), lambda i: (i, 0)), _small_spec((1, LANES))],
        out_shape=[jax.ShapeDtypeStruct((T, D_MODEL), F32), lane_out(I32), lane_out(F32), lane_out(I32),
                   jax.ShapeDtypeStruct((1, LANES), F32)],
        scratch_shapes=[pltpu.VMEM((1, LANES), F32)],
        input_output_aliases={5: 0},
        compiler_params=_params(("arbitrary",)), name=f"moe_route_{tag}",
    )(x, g_ffn.reshape(1, -1), w_router, b_router, counts_in, u_all)
    return outs


def _plan_kernel(cnt_ref, pstart_ref, blkexp_ref, blkslot_ref, nxtexp_ref, nblk_ref, *, n_blocks):
    def per_expert(e, carry):
        acc, k = carry
        pstart_ref[e] = acc
        nb = (cnt_ref[e] + MOE_ROWS - 1) // MOE_ROWS
        first = acc // MOE_ROWS

        def fill(b, c):
            blkexp_ref[b] = e
            blkslot_ref[b] = k % 2
            return c

        lax.fori_loop(first, first + nb, fill, 0)
        return acc + nb * MOE_ROWS, k + jnp.where(nb > 0, 1, 0)

    total, _ = lax.fori_loop(0, N_EXPERTS, per_expert, (jnp.int32(0), jnp.int32(0)))
    used = total // MOE_ROWS
    nblk_ref[0] = used
    last = jnp.maximum(used - 1, 0)

    def backwards(t, carry):
        cur, nxt = carry
        b = last - t
        e = blkexp_ref[b]
        nxt = jnp.where(e != cur, cur, nxt)
        nxtexp_ref[b] = nxt
        return e, nxt

    lax.fori_loop(0, used, backwards, (blkexp_ref[last], jnp.int32(-1)))

    def tail(b, c):
        blkexp_ref[b] = blkexp_ref[last]
        blkslot_ref[b] = blkslot_ref[last]
        nxtexp_ref[b] = -1
        return c

    lax.fori_loop(used, n_blocks, tail, 0)


def _plan(counts, n_blocks):
    smem = pl.BlockSpec(memory_space=pltpu.SMEM)
    per_block = jax.ShapeDtypeStruct((n_blocks,), I32)
    return pl.pallas_call(
        functools.partial(_plan_kernel, n_blocks=n_blocks),
        in_specs=[smem], out_specs=[smem] * 5,
        out_shape=[jax.ShapeDtypeStruct((N_EXPERTS,), I32), per_block, per_block, per_block,
                   jax.ShapeDtypeStruct((1,), I32)],
        name="moe_plan",
    )(counts)


def _dest_kernel(pstart_ref, eid_ref, rank_ref, dest_ref):
    e = eid_ref[...]
    start = jnp.zeros(e.shape, I32)
    for x in range(N_EXPERTS):
        start = jnp.where(e == x, pstart_ref[x], start)
    dest_ref[...] = start + rank_ref[...]


def _dest_rows(pstart, eid, rank, tm):
    M = eid.shape[0]
    tm = math.gcd(M, tm)
    spec = pl.BlockSpec((tm, LANES), lambda i, ps: (i, 0))
    grid_spec = pltpu.PrefetchScalarGridSpec(num_scalar_prefetch=1, grid=(M // tm,), in_specs=[spec, spec],
                                             out_specs=spec)
    return pl.pallas_call(_dest_kernel, grid_spec=grid_spec, out_shape=jax.ShapeDtypeStruct((M, LANES), I32),
                          compiler_params=_params(("parallel",)), name="moe_dest")(pstart, eid, rank)


def _dispatch_kernel(dest_ref, pstart_ref, cnt_ref, nblk_ref, u_ref, xs_hbm, zbuf, sem_z, sem,
                     *, n_tokens, n_blocks):
    R = MOE_ROWS
    i = pl.program_id(0)
    tm = u_ref.shape[0]

    def zero_copy(b):
        return pltpu.make_async_copy(zbuf, xs_hbm.at[pl.ds(pl.multiple_of(b * R, R), R), :], sem_z)

    def partial_block(e):
        c = cnt_ref[e]
        return (c % R) != 0, (pstart_ref[e] + c) // R

    def zero_partial(start):
        def body(e, n):
            has, b = partial_block(e)

            @pl.when(has)
            def _():
                zero_copy(b).start() if start else zero_copy(b).wait()

            return n
        return body

    def zero_unused(start):
        def body(b, n):
            zero_copy(b).start() if start else zero_copy(b).wait()
            return n
        return body

    @pl.when(i == 0)
    def _():
        zbuf[...] = jnp.zeros(zbuf.shape, F32)
        for start in (True, False):
            lax.fori_loop(0, N_EXPERTS, zero_partial(start), 0)
            lax.fori_loop(nblk_ref[0], n_blocks, zero_unused(start), 0)

    def row_copy(r, k):
        d = dest_ref[2 * (i * tm + r) + k]
        return pltpu.make_async_copy(u_ref.at[pl.ds(r, 1), :], xs_hbm.at[pl.ds(d, 1), :], sem)

    def start_row(r, n):
        row_copy(r, 0).start()
        row_copy(r, 1).start()
        return n

    def wait_row(r, n):
        row_copy(r, 0).wait()
        row_copy(r, 1).wait()
        return n

    rem = n_tokens % tm

    @pl.when((i + 1) * tm <= n_tokens)
    def _():
        lax.fori_loop(0, tm, start_row, 0, unroll=8)
        lax.fori_loop(0, tm, wait_row, 0, unroll=8)

    if rem:
        @pl.when((i + 1) * tm > n_tokens)
        def _():
            lax.fori_loop(0, rem, start_row, 0, unroll=8)
            lax.fori_loop(0, rem, wait_row, 0, unroll=8)


def _dispatch(dest, pstart, counts, n_used, u_all, n_blocks, tm):
    T = u_all.shape[0]
    grid_spec = pltpu.PrefetchScalarGridSpec(
        num_scalar_prefetch=4, grid=(pl.cdiv(T, tm),),
        in_specs=[pl.BlockSpec((tm, D_MODEL), lambda i, *_: (i, 0))],
        out_specs=pl.BlockSpec(memory_space=pl.ANY),
        scratch_shapes=[pltpu.VMEM((MOE_ROWS, D_MODEL), F32), pltpu.SemaphoreType.DMA(()),
                        pltpu.SemaphoreType.DMA(())])
    return pl.pallas_call(
        functools.partial(_dispatch_kernel, n_tokens=T, n_blocks=n_blocks), grid_spec=grid_spec,
        out_shape=jax.ShapeDtypeStruct((n_blocks * MOE_ROWS, D_MODEL), F32),
        compiler_params=_params(("arbitrary",)), name="moe_dispatch",
    )(dest, pstart, counts, n_used, u_all)


def _expert_kernel(blkexp_ref, blkslot_ref, nxtexp_ref, nblk_ref, x_ref, wg_hbm, wu_hbm, wd_hbm, o_ref,
                   wg_f, wu_f, wd_f, wg_s, wu_s, wd_s, sem):
    i = pl.program_id(0)

    def fetch(e, slot):
        return [pltpu.make_async_copy(src.at[0, e], dst.at[slot], sem.at[slot])
                for src, dst in ((wg_hbm, wg_f), (wu_hbm, wu_f), (wd_hbm, wd_f))]

    @pl.when(i < nblk_ref[0])
    def _():
        e = blkexp_ref[i]
        slot = blkslot_ref[i]
        nxt = nxtexp_ref[i]

        @pl.when(i == 0)
        def _():
            for cp in fetch(e, slot):
                cp.start()

        @pl.when((i == 0) | (e != blkexp_ref[jnp.maximum(i - 1, 0)]))
        def _():
            for cp in fetch(e, slot):
                cp.wait()

            @pl.when(nxt >= 0)
            def _():
                for cp in fetch(nxt, 1 - slot):
                    cp.start()

            wg_s[...] = wg_f[slot].astype(BF16)
            wu_s[...] = wu_f[slot].astype(BF16)
            wd_s[...] = wd_f[slot].astype(BF16)

        x = x_ref[...].astype(BF16)
        hg = jnp.dot(x, wg_s[...], preferred_element_type=F32)
        hu = jnp.dot(x, wu_s[...], preferred_element_type=F32)
        h = (hg * jax.nn.sigmoid(hg) * hu).astype(BF16)
        o_ref[...] = jnp.dot(h, wd_s[...], preferred_element_type=F32)

    @pl.when(i >= nblk_ref[0])
    def _():
        o_ref[...] = jnp.zeros(o_ref.shape, F32)


def _experts(xs, blk_exp, blk_slot, nxt_exp, n_used, we_g, we_u, we_d, n_blocks):
    hbm = pl.BlockSpec(memory_space=pl.ANY)
    up, down = (D_MODEL, EXPERT_FF), (EXPERT_FF, D_MODEL)
    grid_spec = pltpu.PrefetchScalarGridSpec(
        num_scalar_prefetch=4, grid=(n_blocks,),
        in_specs=[pl.BlockSpec((MOE_ROWS, D_MODEL),
                               lambda i, be, bs, nx, nb: (jnp.maximum(jnp.minimum(i, nb[0] - 1), 0), 0)),
                  hbm, hbm, hbm],
        out_specs=pl.BlockSpec((MOE_ROWS, D_MODEL), lambda i, *_: (i, 0)),
        scratch_shapes=[pltpu.VMEM((2,) + up, F32), pltpu.VMEM((2,) + up, F32), pltpu.VMEM((2,) + down, F32),
                        pltpu.VMEM(up, BF16), pltpu.VMEM(up, BF16), pltpu.VMEM(down, BF16),
                        pltpu.SemaphoreType.DMA((2,))])
    return pl.pallas_call(
        _expert_kernel, grid_spec=grid_spec,
        out_shape=jax.ShapeDtypeStruct((n_blocks * MOE_ROWS, D_MODEL), F32),
        compiler_params=_params(("arbitrary",)), name="moe_experts",
    )(blk_exp, blk_slot, nxt_exp, n_used, xs, we_g, we_u, we_d)


def _combine_kernel(dest_ref, h_ref, wt_ref, gf_ref, yb_hbm, o_ref, ybuf0, ybuf1, sem, *, dest_off):
    i = pl.program_id(0)
    tm = h_ref.shape[0]
    base = dest_off + i * (2 * tm)

    def row_copy(r, k, buf):
        d = dest_ref[base + 2 * r + k]
        return pltpu.make_async_copy(yb_hbm.at[pl.ds(d, 1), :], buf.at[pl.ds(r, 1), :], sem)

    def start(r, c):
        row_copy(r, 0, ybuf0).start()
        row_copy(r, 1, ybuf1).start()
        return c

    def wait(r, c):
        row_copy(r, 0, ybuf0).wait()
        row_copy(r, 1, ybuf1).wait()
        return c

    lax.fori_loop(0, tm, start, 0, unroll=4)
    lax.fori_loop(0, tm, wait, 0, unroll=4)
    wt = wt_ref[...]
    h = h_ref[...] + (wt[:, 0:1] * ybuf0[...] + wt[:, 1:2] * ybuf1[...])
    o_ref[...] = _rms(h, gf_ref[...])


def _combine(h2, wts, dest, yb, norm_final, dest_off, tm, tag):
    M = h2.shape[0]
    grid_spec = pltpu.PrefetchScalarGridSpec(
        num_scalar_prefetch=1, grid=(M // tm,),
        in_specs=[pl.BlockSpec((tm, D_MODEL), lambda i, *_: (i, 0)), pl.BlockSpec((tm, LANES), lambda i, *_: (i, 0)),
                  pl.BlockSpec((1, D_MODEL), lambda i, *_: (0, 0)), pl.BlockSpec(memory_space=pl.ANY)],
        out_specs=pl.BlockSpec((tm, D_MODEL), lambda i, *_: (i, 0)),
        scratch_shapes=[pltpu.VMEM((tm, D_MODEL), F32), pltpu.VMEM((tm, D_MODEL), F32),
                        pltpu.SemaphoreType.DMA(())])
    return pl.pallas_call(
        functools.partial(_combine_kernel, dest_off=dest_off), grid_spec=grid_spec,
        out_shape=jax.ShapeDtypeStruct((M, D_MODEL), F32),
        compiler_params=_params(("arbitrary",)), name=f"moe_combine_{tag}",
    )(dest, h2, wts, norm_final.reshape(1, -1), yb)


def _moe_and_final_norm(h2_p, h2_s, g_ffn, wr_g, br_g, wr_e, br_e, we_g, we_u, we_d, norm_final, tm_p):
    Tp, Ts = h2_p.shape[0], h2_s.shape[0]
    T = Tp + Ts
    pad = LANES - N_GROUPS - N_EXPERTS
    w_router = jnp.concatenate([wr_g, wr_e, jnp.zeros((D_MODEL, pad), F32)], axis=1).astype(BF16)
    b_router = jnp.concatenate([br_g, br_e, jnp.zeros((pad,), F32)]).reshape(1, LANES)
    u_all = jnp.zeros((T, D_MODEL), F32)
    zero_counts = jnp.zeros((1, LANES), F32)
    u_all, eid_p, wt_p, rank_p, counts = _route(h2_p, g_ffn, w_router, b_router, zero_counts, u_all, 0, tm_p, "p")
    u_all, eid_s, wt_s, rank_s, counts = _route(h2_s, g_ffn, w_router, b_router, counts, u_all, Tp, Ts, "s")
    A = 2 * T
    n_blocks = (A + N_EXPERTS * (MOE_ROWS - 1)) // MOE_ROWS + 1
    cnt = counts[0, :N_EXPERTS].astype(I32)
    pstart, blk_exp, blk_slot, nxt_exp, n_used = _plan(cnt, n_blocks)
    dest = jnp.concatenate([_dest_rows(pstart, eid_p, rank_p, 1024)[:, :2].reshape(-1),
                            _dest_rows(pstart, eid_s, rank_s, Ts)[:, :2].reshape(-1)])
    xs = _dispatch(dest, pstart, cnt, n_used, u_all, n_blocks, 256)
    yb = _experts(xs, blk_exp, blk_slot, nxt_exp, n_used, we_g, we_u, we_d, n_blocks)
    y_p = _combine(h2_p, wt_p, dest, yb, norm_final, 0, 256, "p")
    y_s = _combine(h2_s, wt_s, dest, yb, norm_final, 2 * Tp, Ts, "s")
    return y_p, y_s


def kernel(x_prompt, x_sample, mem_prompt, cache_k, cache_v, cache_mem_k, cache_mem_v, state_hgrn, page_table,
           norm_mix, w_in, lambda_q1, lambda_k1, lambda_q2, lambda_k2, subln, hgrn_lb, hgrn_norm, w_pa, w_pb,
           w_out, norm_cross, w_cq, w_ck, w_cv, w_co, norm_ffn, w_router_group, b_router_group,
           w_router_expert, b_router_expert, w_e_gate, w_e_up, w_e_down, norm_final):
    assert w_in.shape[0] == 1, "single-layer step"
    Bp, S, D = x_prompt.shape
    Bd, Ld, _ = x_sample.shape
    assert Bp == 1 and Ld == 1
    n_pages = page_table.shape[1]
    page = cache_k.shape[2]
    past_len = n_pages * page
    xp = x_prompt.reshape(S, D)
    xs = x_sample.reshape(Bd, D)
    lam = (lambda_q1, lambda_k1, lambda_q2, lambda_k2)
    TM = 1024
    w_in = w_in.astype(BF16)

    q, k, k_bf, v, v_bf, hg, gates = _in_projection(xp, norm_mix[0], w_in, jnp.arange(S, dtype=I32), TM, 512, "p",
                                                    BF16, ATT_HEAD_DIM ** -0.5 * math.log2(math.e), 2)
    oa = _attention_prompt(q, k_bf, v_bf, *lam, subln, ATTN_TQ, ATTN_TK)
    oh, st_p = _hgrn_prompt(hg, hgrn_lb, hgrn_norm, 2048)
    merged = _merge(oa, oh, w_pa, w_pb, gates, TM, 512, "p")
    (h1_p,) = _mm(merged, w_out, col_off=0, n_cols=D, tm=TM, tn=512, epilogue=_epi_residual, out_dtypes=[F32],
                  extras=[(xp, pl.BlockSpec((TM, 512), lambda i, j: (i, j)))], name="outproj_p")

    pos_s = jnp.full((Bd,), past_len, I32)
    qs, ks, ks_bf, vs, vs_bf, hgs, gates_s = _in_projection(xs, norm_mix[0], w_in, pos_s, Bd, 512, "s", F32,
                                                            ATT_HEAD_DIM ** -0.5, 1)
    ck = jnp.transpose(cache_k[0], (0, 2, 3, 4, 1)).reshape(cache_k.shape[1], ATT_WIDTH, page)
    cv = cache_v[0].reshape(cache_v.shape[1], page * ATT_HEADS, 2 * ATT_HEAD_DIM)
    oa_s = _attention_decode(qs, ks_bf, vs_bf, ck, cv, page_table, *lam, subln)
    oh_s, st_s = _hgrn_decode(hgs, state_hgrn[0], hgrn_lb, hgrn_norm)
    merged_s = _merge(oa_s, oh_s, w_pa, w_pb, gates_s, Bd, 512, "s")
    (h1_s,) = _mm(merged_s, w_out, col_off=0, n_cols=D, tm=Bd, tn=512, epilogue=_epi_residual, out_dtypes=[F32],
                  extras=[(xs, pl.BlockSpec((Bd, 512), lambda i, j: (i, j)))], name="outproj_s")

    mem = mem_prompt.reshape(-1, D)
    mk, mk_bf = _mm(mem, w_ck, col_off=0, n_cols=X_WIDTH, tm=mem.shape[0], tn=X_WIDTH, epilogue=_epi_v,
                    out_dtypes=[F32, BF16], name="mem_k")
    mv, mv_bf = _mm(mem, w_cv, col_off=0, n_cols=X_WIDTH, tm=mem.shape[0], tn=X_WIDTH, epilogue=_epi_v,
                    out_dtypes=[F32, BF16], name="mem_v")
    w_cq_bf = w_cq[0].astype(BF16)
    w_co_bf = w_co[0].astype(BF16)
    h2_p = _cross_prompt(h1_p, norm_cross[0], w_cq_bf, mk_bf, mv_bf, w_co_bf, 256)
    (qc_s,) = _mm(h1_s, w_cq, col_off=0, n_cols=X_WIDTH, tm=Bd, tn=X_WIDTH, epilogue=_epi_plain,
                  out_dtypes=[F32], norm_g=norm_cross[0], name="cross_q_s")
    oc_s = _cross_decode(qc_s, cache_mem_k[0], cache_mem_v[0])
    (h2_s,) = _mm(oc_s, w_co, col_off=0, n_cols=D, tm=Bd, tn=512, epilogue=_epi_residual, out_dtypes=[F32],
                  extras=[(h1_s, pl.BlockSpec((Bd, 512), lambda i, j: (i, j)))], name="cross_o_s")

    y_p, y_s = _moe_and_final_norm(h2_p, h2_s, norm_ffn[0], w_router_group[0], b_router_group[0],
                                   w_router_expert[0], b_router_expert[0], w_e_gate, w_e_up, w_e_down,
                                   norm_final, 256)

    return (y_p.reshape(Bp, S, D), y_s.reshape(Bd, Ld, D),
            k.reshape(1, Bp, S, ATT_HEADS, 2, ATT_HEAD_DIM), v.reshape(1, Bp, S, ATT_HEADS, 2 * ATT_HEAD_DIM),
            ks.reshape(1, Bd, Ld, ATT_HEADS, 2, ATT_HEAD_DIM), vs.reshape(1, Bd, Ld, ATT_HEADS, 2 * ATT_HEAD_DIM),
            st_p.reshape(1, Bp, HG_HEADS, HG_KEY, HG_VAL), st_s.reshape(1, Bd, HG_HEADS, HG_KEY, HG_VAL),
            mk.reshape(1, Bp, -1, X_HEADS, X_HEAD_DIM), mv.reshape(1, Bp, -1, X_HEADS, X_HEAD_DIM))
```
